```python
import jax, jax.numpy as jnp
from jax import lax
import numpy as np

D_MODEL = 2048
BATCH = 8
SEQ = 4096
DEPTH = 2

HEAD_DIM = 64
MOBA_HEADS = 8
MOBA_BLOCK = 256
MOBA_TOPK = 3
MOBA_QCHUNK = 32
SB_HEADS = 8
SB_QBLOCK = 128
SWA_HEADS = 8
SWA_KV_HEADS = 2
SWA_WINDOW = 128
CONV_CH = 512
CONV_K = 3
D_FF = 5632
N_BRANCH = 4
EPS = 1e-6

MOBA_W = MOBA_HEADS * HEAD_DIM
SB_W = SB_HEADS * HEAD_DIM
SWA_QW = SWA_HEADS * HEAD_DIM
SWA_KVW = SWA_KV_HEADS * HEAD_DIM
BRANCH_W = 512
GATE_W = N_BRANCH * D_MODEL
IN_COLS = 3 * MOBA_W + 3 * SB_W + SWA_QW + 2 * SWA_KVW + 3 * CONV_CH + GATE_W

kernel_name = "hybrid_parallel_moba_stickbreak_swa_conv_macaron"


def _in_proj_splits():
    widths = [MOBA_W, MOBA_W, MOBA_W, SB_W, SB_W, SB_W, SWA_QW, SWA_KVW, SWA_KVW,
              CONV_CH, CONV_CH, CONV_CH]
    return [int(v) for v in np.cumsum(widths)]


def rms_norm(x, g):
    xf = x.astype(jnp.float32)
    y = xf * lax.rsqrt(jnp.mean(xf * xf, axis=-1, keepdims=True) + EPS)
    return (y * g.astype(jnp.float32)).astype(x.dtype)


def swiglu(x, w1, w3, w2):
    return (jax.nn.silu(x @ w1) * (x @ w3)) @ w2


def alibi_slopes(n):
    return 2.0 ** (-8.0 * jnp.arange(1, n + 1, dtype=jnp.float32) / n)


def moba_attention(q, k, v, slopes):
    bsz, seq, nh, dh = q.shape
    L = MOBA_BLOCK
    nb = -(-seq // L)
    pad = nb * L - seq
    n_sel = min(MOBA_TOPK, nb)
    qcl = MOBA_QCHUNK
    n_chunks = seq // qcl
    scale = dh ** -0.5
    q = q.transpose(0, 2, 1, 3)
    k = jnp.pad(k.transpose(0, 2, 1, 3), ((0, 0), (0, 0), (0, pad), (0, 0)))
    v = jnp.pad(v.transpose(0, 2, 1, 3), ((0, 0), (0, 0), (0, pad), (0, 0)))
    k_blocks = k.reshape(bsz, nh, nb, L, dh)
    v_blocks = v.reshape(bsz, nh, nb, L, dh)
    k_mean = jnp.mean(k_blocks.astype(jnp.float32), axis=3)
    gate = jnp.einsum('bhsd,bhnd->bhsn', q.astype(jnp.float32), k_mean)
    q_block = jnp.arange(seq) // L
    fully_past = jnp.arange(nb)[None, :] < q_block[:, None]
    gate = jnp.where(fully_past, gate, -jnp.inf)
    _, sel = lax.top_k(gate, n_sel)
    q_chunks = q.reshape(bsz, nh, n_chunks, qcl, dh).transpose(2, 0, 1, 3, 4)
    sel_chunks = sel.reshape(bsz, nh, n_chunks, qcl, n_sel).transpose(2, 0, 1, 3, 4)
    gather_blocks = jax.vmap(jax.vmap(lambda blocks, idx: blocks[idx]))

    def attend_chunk(args):
        c, q_c, sel_c = args
        t = c * qcl + jnp.arange(qcl)
        own = (c * qcl) // L
        k_own = lax.dynamic_slice_in_dim(k, own * L, L, axis=2)
        v_own = lax.dynamic_slice_in_dim(v, own * L, L, axis=2)
        d_own = t[:, None] - (own * L + jnp.arange(L))[None, :]
        lg_own = (jnp.einsum('bhqd,bhkd->bhqk', q_c, k_own,
                             preferred_element_type=jnp.float32) * scale
                  - slopes[:, None, None] * d_own)
        lg_own = jnp.where(d_own >= 0, lg_own, -jnp.inf)
        k_sel = gather_blocks(k_blocks, sel_c)
        v_sel = gather_blocks(v_blocks, sel_c)
        d_sel = t[:, None, None] - (sel_c[..., None] * L + jnp.arange(L))
        lg_sel = (jnp.einsum('bhqd,bhqrkd->bhqrk', q_c, k_sel,
                             preferred_element_type=jnp.float32) * scale
                  - slopes[:, None, None, None] * d_sel)
        keep = jnp.arange(n_sel) < own
        lg_sel = jnp.where(keep[:, None], lg_sel, -jnp.inf).reshape(bsz, nh, qcl, n_sel * L)
        p = jax.nn.softmax(jnp.concatenate([lg_sel, lg_own], axis=-1), axis=-1).astype(v.dtype)
        p_sel = p[..., :n_sel * L].reshape(bsz, nh, qcl, n_sel, L)
        p_own = p[..., n_sel * L:]
        return (jnp.einsum('bhqrk,bhqrkd->bhqd', p_sel, v_sel)
                + jnp.einsum('bhqk,bhkd->bhqd', p_own, v_own))

    out = lax.map(attend_chunk, (jnp.arange(n_chunks), q_chunks, sel_chunks))
    return out.transpose(1, 0, 3, 2, 4).reshape(bsz, seq, nh * dh)


def stick_breaking_attention(q, k, v):
    bsz, seq, nh, dh = q.shape
    scale = dh ** -0.5
    q = q.transpose(0, 2, 1, 3)
    k = k.transpose(0, 2, 1, 3)
    v = v.transpose(0, 2, 1, 3)
    outs = []
    for qb in range(seq // SB_QBLOCK):
        t0 = qb * SB_QBLOCK
        kl = t0 + SB_QBLOCK
        z = jnp.einsum('bhqd,bhkd->bhqk', q[:, :, t0:kl], k[:, :, :kl],
                       preferred_element_type=jnp.float32) * scale
        t = t0 + jnp.arange(SB_QBLOCK)
        causal = jnp.arange(kl)[None, :] < t[:, None]
        log_1m_beta = jnp.where(causal, jax.nn.log_sigmoid(-z), 0.0)
        after = lax.cumsum(log_1m_beta, axis=3, reverse=True) - log_1m_beta
        a = jnp.where(causal, jnp.exp(jax.nn.log_sigmoid(z) + after), 0.0)
        outs.append(jnp.einsum('bhqk,bhkd->bhqd', a.astype(v.dtype), v[:, :, :kl]))
    o = jnp.concatenate(outs, axis=2)
    return o.transpose(0, 2, 1, 3).reshape(bsz, seq, nh * dh)


def sliding_window_gqa(q, k, v, sinks, slopes):
    bsz, seq, nq, dh = q.shape
    nkv = k.shape[2]
    grp = nq // nkv
    W = SWA_WINDOW
    nblk = seq // W
    scale = dh ** -0.5
    qb = q.reshape(bsz, nblk, W, nkv, grp, dh)

    def with_prev(x):
        xb = x.reshape(bsz, nblk, W, nkv, dh)
        prev = jnp.pad(xb, ((0, 0), (1, 0), (0, 0), (0, 0), (0, 0)))[:, :-1]
        return jnp.concatenate([prev, xb], axis=2)

    kb = with_prev(k)
    vb = with_prev(v)
    dist = jnp.arange(W)[:, None] + W - jnp.arange(2 * W)[None, :]
    key_pos = (jnp.arange(nblk)[:, None] - 1) * W + jnp.arange(2 * W)[None, :]
    valid = (dist >= 0)[None] & (dist < W)[None] & (key_pos >= 0)[:, None, :]
    slope = slopes.reshape(nkv, grp)[None, None, :, :, None, None]
    logits = (jnp.einsum('bnqhgd,bnkhd->bnhgqk', qb, kb,
                         preferred_element_type=jnp.float32) * scale - slope * dist)
    logits = jnp.where(valid[None, :, None, None], logits, -jnp.inf)
    sink = sinks.astype(jnp.float32).reshape(nkv, grp)[None, None, :, :, None, None]
    m = jnp.maximum(jnp.max(logits, axis=-1, keepdims=True), sink)
    e = jnp.exp(logits - m)
    p = e / (jnp.sum(e, axis=-1, keepdims=True) + jnp.exp(sink - m))
    out = jnp.einsum('bnhgqk,bnkhd->bnqhgd', p.astype(v.dtype), vb)
    return out.reshape(bsz, seq, nq * dh)


def short_gated_conv(b_gate, c_gate, h, conv_w):
    u = c_gate * h
    y = lax.conv_general_dilated(u, conv_w, window_strides=(1,),
                                 padding=((CONV_K - 1, 0),),
                                 dimension_numbers=('NWC', 'WIO', 'NWC'),
                                 feature_group_count=u.shape[-1])
    return b_gate * y


def hybrid_mixer(xn, w_in, conv_w, sinks, w_branch, w_out):
    bsz, seq, _ = xn.shape
    proj = xn @ w_in
    (qa, ka, va, qb, kb, vb, qc, kc, vc, bg, cg, hd, gates) = jnp.split(
        proj, _in_proj_splits(), axis=-1)

    def heads(t, n):
        return t.reshape(bsz, seq, n, HEAD_DIM)

    slopes = alibi_slopes(SWA_HEADS + MOBA_HEADS)
    swa_slopes = slopes[:SWA_HEADS]
    moba_slopes = slopes[SWA_HEADS:]
    y_a = moba_attention(heads(qa, MOBA_HEADS), heads(ka, MOBA_HEADS),
                         heads(va, MOBA_HEADS), moba_slopes)
    y_b = stick_breaking_attention(heads(qb, SB_HEADS), heads(kb, SB_HEADS),
                                   heads(vb, SB_HEADS))
    y_c = sliding_window_gqa(heads(qc, SWA_HEADS), heads(kc, SWA_KV_HEADS),
                             heads(vc, SWA_KV_HEADS), sinks, swa_slopes)
    y_d = short_gated_conv(bg, cg, hd, conv_w)
    g = gates.reshape(bsz, seq, N_BRANCH, D_MODEL)
    branches = (y_a, y_b, y_c, y_d)
    merged = jax.nn.sigmoid(g[:, :, 0]) * (branches[0] @ w_branch[0])
    for n in range(1, N_BRANCH):
        merged = merged + jax.nn.sigmoid(g[:, :, n]) * (branches[n] @ w_branch[n])
    return merged @ w_out


def setup_inputs(seed: int = 0) -> dict:
    key = jax.random.key(seed)
    ks = jax.random.split(key, 16)

    def nrm(k, shape, scale):
        return jax.random.normal(k, shape, jnp.float32) * scale

    def gain(k, shape):
        return 1.0 + 0.02 * jax.random.normal(k, shape, jnp.float32)

    return {
        "x": nrm(ks[0], (BATCH, SEQ, D_MODEL), 1.0),
        "ffn1_norm": gain(ks[1], (DEPTH, D_MODEL)),
        "ffn1_w1": nrm(ks[2], (DEPTH, D_MODEL, D_FF), D_MODEL ** -0.5),
        "ffn1_w3": nrm(ks[3], (DEPTH, D_MODEL, D_FF), D_MODEL ** -0.5),
        "ffn1_w2": nrm(ks[4], (DEPTH, D_FF, D_MODEL), D_FF ** -0.5),
        "mix_norm": gain(ks[5], (DEPTH, D_MODEL)),
        "w_in": nrm(ks[6], (DEPTH, D_MODEL, IN_COLS), D_MODEL ** -0.5),
        "conv_w": nrm(ks[7], (DEPTH, CONV_K, 1, CONV_CH), CONV_K ** -0.5),
        "attn_sinks": nrm(ks[8], (DEPTH, SWA_HEADS), 1.0),
        "w_branch": nrm(ks[9], (DEPTH, N_BRANCH, BRANCH_W, D_MODEL), BRANCH_W ** -0.5),
        "w_out": nrm(ks[10], (DEPTH, D_MODEL, D_MODEL), D_MODEL ** -0.5),
        "ffn2_norm": gain(ks[11], (DEPTH, D_MODEL)),
        "ffn2_w1": nrm(ks[12], (DEPTH, D_MODEL, D_FF), D_MODEL ** -0.5),
        "ffn2_w3": nrm(ks[13], (DEPTH, D_MODEL, D_FF), D_MODEL ** -0.5),
        "ffn2_w2": nrm(ks[14], (DEPTH, D_FF, D_MODEL), D_FF ** -0.5),
        "final_norm": gain(ks[15], (D_MODEL,)),
    }


def reference(x, ffn1_norm, ffn1_w1, ffn1_w3, ffn1_w2, mix_norm, w_in, conv_w,
              attn_sinks, w_branch, w_out, ffn2_norm, ffn2_w1, ffn2_w3, ffn2_w2,
              final_norm):
    h = x
    for l in range(DEPTH):
        h = h + 0.5 * swiglu(rms_norm(h, ffn1_norm[l]), ffn1_w1[l], ffn1_w3[l], ffn1_w2[l])
        h = h + hybrid_mixer(rms_norm(h, mix_norm[l]), w_in[l], conv_w[l],
                             attn_sinks[l], w_branch[l], w_out[l])
        h = h + 0.5 * swiglu(rms_norm(h, ffn2_norm[l]), ffn2_w1[l], ffn2_w3[l], ffn2_w2[l])
    return rms_norm(h, final_norm)
```

```python
import functools

import jax
import jax.numpy as jnp
from jax import lax
from jax.experimental import pallas as pl
from jax.experimental.pallas import tpu as pltpu

F32 = jnp.float32
BF16 = jnp.bfloat16

EPS = 1e-6
HEAD_DIM = 64
LANES = 128
HEADS_PER_BLOCK = LANES // HEAD_DIM
MOBA_BLOCK = 256
MOBA_TOPK = 3
SWA_WINDOW = 128
CONV_K = 3
N_BRANCH = 4
BRANCH_W = 512
N_HEADS = 8
SWA_KV_HEADS = 2
SB_TILE = 256

COL_MOBA_Q, COL_MOBA_K, COL_MOBA_V = 0, 512, 1024
COL_SB_Q, COL_SB_K, COL_SB_V = 1536, 2048, 2560
COL_SWA_Q, COL_SWA_K, COL_SWA_V = 3072, 3584, 3712
COL_CONV_B, COL_CONV_C, COL_CONV_H = 3840, 4352, 4864
QKV_COLS = 5376

V7X_VMEM_BYTES = 64 * 1024 * 1024
VMEM_LIMIT = V7X_VMEM_BYTES - 8 * 1024 * 1024
NEG_BIG = -1e30

TOKEN_TILE = 512
FF_TILE = 512
QKV_COL_TILE = 768
MERGE_COL_TILE = 256


def _params(*semantics):
    return pltpu.CompilerParams(dimension_semantics=semantics, vmem_limit_bytes=VMEM_LIMIT)


def _rms_norm(x, g):
    ms = jnp.mean(x * x, axis=-1, keepdims=True)
    return x * lax.rsqrt(ms + EPS) * g


def _dot(a, b):
    return jnp.dot(a, b, preferred_element_type=F32)


def _dot_nt(a, b):
    return lax.dot_general(a, b, (((1,), (1,)), ((), ())), preferred_element_type=F32)


def _split_bf16(x):
    hi = x.astype(BF16)
    lo = (x - hi.astype(F32)).astype(BF16)
    return hi, lo


def _ffn_kernel(x_ref, g_ref, w1_ref, w3_ref, w2_ref, fg_ref, o_ref, xn_ref, *, final_norm):
    f = pl.program_id(1)

    @pl.when(f == 0)
    def _():
        xn_ref[...] = _rms_norm(x_ref[...], g_ref[...]).astype(BF16)

    xn = xn_ref[...]
    a = _dot(xn, w1_ref[...])
    b = _dot(xn, w3_ref[...])
    hidden = (a * jax.nn.sigmoid(a) * b).astype(BF16)
    d = 0.5 * _dot(hidden, w2_ref[...])

    @pl.when(f == 0)
    def _():
        o_ref[...] = x_ref[...] + d

    @pl.when(f > 0)
    def _():
        o_ref[...] += d

    if final_norm:
        @pl.when(f == pl.num_programs(1) - 1)
        def _():
            o_ref[...] = _rms_norm(o_ref[...], fg_ref[...])


def _ffn(x, g, w1, w3, w2, final_g, *, final_norm):
    n, d = x.shape
    d_ff = w1.shape[1]
    tm, tf = min(TOKEN_TILE, n), FF_TILE
    return pl.pallas_call(
        functools.partial(_ffn_kernel, final_norm=final_norm),
        grid=(n // tm, d_ff // tf),
        in_specs=[
            pl.BlockSpec((tm, d), lambda i, f: (i, 0)),
            pl.BlockSpec((1, d), lambda i, f: (0, 0)),
            pl.BlockSpec((d, tf), lambda i, f: (0, f)),
            pl.BlockSpec((d, tf), lambda i, f: (0, f)),
            pl.BlockSpec((tf, d), lambda i, f: (f, 0)),
            pl.BlockSpec((1, d), lambda i, f: (0, 0)),
        ],
        out_specs=pl.BlockSpec((tm, d), lambda i, f: (i, 0)),
        out_shape=jax.ShapeDtypeStruct((n, d), F32),
        scratch_shapes=[pltpu.VMEM((tm, d), BF16)],
        compiler_params=_params("parallel", "arbitrary"),
        name="ffn",
    )(x, g, w1, w3, w2, final_g)


def _qkv_kernel(x_ref, g_ref, w_ref, o_ref, xn_ref):
    @pl.when(pl.program_id(1) == 0)
    def _():
        xn_ref[...] = _rms_norm(x_ref[...], g_ref[...]).astype(BF16)

    o_ref[...] = _dot(xn_ref[...], w_ref[...]).astype(BF16)


def _qkv_proj(x, g, w):
    n, d = x.shape
    cols = w.shape[1]
    tm, tn = min(2 * TOKEN_TILE, n), QKV_COL_TILE
    return pl.pallas_call(
        _qkv_kernel,
        grid=(n // tm, cols // tn),
        in_specs=[
            pl.BlockSpec((tm, d), lambda i, j: (i, 0)),
            pl.BlockSpec((1, d), lambda i, j: (0, 0)),
            pl.BlockSpec((d, tn), lambda i, j: (0, j)),
        ],
        out_specs=pl.BlockSpec((tm, tn), lambda i, j: (i, j)),
        out_shape=jax.ShapeDtypeStruct((n, cols), BF16),
        scratch_shapes=[pltpu.VMEM((tm, d), BF16)],
        compiler_params=_params("parallel", "arbitrary"),
        name="qkv_proj",
    )(x, g, w)


def _head_lane_mask(head_in_block):
    lane = lax.broadcasted_iota(jnp.int32, (1, LANES), 1)
    return (lane // HEAD_DIM) == head_in_block


def _select_head(x, head_in_block):
    return jnp.where(_head_lane_mask(head_in_block), x, jnp.zeros_like(x))


def _softplus(z):
    return jnp.maximum(z, 0.0) + jnp.log1p(jnp.exp(-jnp.abs(z)))


def _sb_tile(qh, k, v, suffix_ones, run, causal):
    z = _dot_nt(qh, k)
    sp = _softplus(z)
    log1m = -sp
    if causal is not None:
        log1m = jnp.where(causal, log1m, 0.0)
    hi, lo = _split_bf16(log1m)
    after = _dot(hi, suffix_ones) + _dot(lo, suffix_ones) + run
    a = jnp.exp(z - sp + after)
    if causal is not None:
        a = jnp.where(causal, a, 0.0)
    return _dot(a.astype(BF16), v), jnp.sum(log1m, axis=1, keepdims=True)


def _sb_kernel(q_ref, k_ref, v_ref, o_ref, acc_ref, run_ref):
    t = SB_TILE
    i = pl.program_id(2)
    row = lax.broadcasted_iota(jnp.int32, (t, t), 0)
    col = lax.broadcasted_iota(jnp.int32, (t, t), 1)
    suffix_ones = (row > col).astype(BF16)
    causal = col < row
    q = q_ref[...] * (HEAD_DIM ** -0.5)
    diag = pl.multiple_of(i * t, t)
    for hh in range(HEADS_PER_BLOCK):
        qh = _select_head(q, hh)
        pv, rs = _sb_tile(qh, k_ref[pl.ds(diag, t), :], v_ref[pl.ds(diag, t), :],
                          suffix_ones, jnp.zeros((t, 1), F32), causal)
        acc_ref[hh] = pv
        run_ref[hh] = rs

        def body(j, carry, hh=hh, qh=qh):
            start = pl.multiple_of((i - j) * t, t)
            pv, rs = _sb_tile(qh, k_ref[pl.ds(start, t), :], v_ref[pl.ds(start, t), :],
                              suffix_ones, run_ref[hh], None)
            acc_ref[hh] += pv
            run_ref[hh] += rs
            return carry

        lax.fori_loop(1, i + 1, body, 0)
    o_ref[...] = jnp.where(_head_lane_mask(0), acc_ref[0], acc_ref[1]).astype(o_ref.dtype)


def _sb_attention(qkv, bsz, seq):
    t = SB_TILE
    cb = lambda off: off // LANES
    return pl.pallas_call(
        _sb_kernel,
        grid=(bsz, N_HEADS // HEADS_PER_BLOCK, seq // t),
        in_specs=[
            pl.BlockSpec((None, t, LANES), lambda b, hp, i: (b, i, cb(COL_SB_Q) + hp)),
            pl.BlockSpec((None, seq, LANES), lambda b, hp, i: (b, 0, cb(COL_SB_K) + hp)),
            pl.BlockSpec((None, seq, LANES), lambda b, hp, i: (b, 0, cb(COL_SB_V) + hp)),
        ],
        out_specs=pl.BlockSpec((None, t, LANES), lambda b, hp, i: (b, i, hp)),
        out_shape=jax.ShapeDtypeStruct((bsz, seq, BRANCH_W), BF16),
        scratch_shapes=[pltpu.VMEM((HEADS_PER_BLOCK, t, LANES), F32),
                        pltpu.VMEM((HEADS_PER_BLOCK, t, 1), F32)],
        compiler_params=_params("parallel", "parallel", "arbitrary"),
        name="stick_breaking",
    )(qkv, qkv, qkv)


def _moba_kernel(slopes_ref, q_ref, k_ref, v_ref, o_ref, kmean_ref, acc_ref, m_ref, l_ref, *, n_blocks):
    t = MOBA_BLOCK
    hp = pl.program_id(1)
    i = pl.program_id(2)

    @pl.when(i == 0)
    def _():
        kmean_ref[...] = jnp.zeros_like(kmean_ref)
        for n in range(n_blocks):
            kb = k_ref[n * t:(n + 1) * t, :].astype(F32)
            kmean_ref[n:n + 1, :] = jnp.mean(kb, axis=0, keepdims=True)

    row = lax.broadcasted_iota(jnp.int32, (t, t), 0)
    col = lax.broadcasted_iota(jnp.int32, (t, t), 1)
    dist0 = (row - col).astype(F32)
    lane = lax.broadcasted_iota(jnp.int32, (t, LANES), 1)
    past = lane < i
    q_raw = q_ref[...]
    q = q_raw * (HEAD_DIM ** -0.5)
    km_hi, km_lo = _split_bf16(kmean_ref[...])
    diag = pl.multiple_of(i * t, t)
    for hh in range(HEADS_PER_BLOCK):
        slope = slopes_ref[hp * HEADS_PER_BLOCK + hh]
        qh = _select_head(q, hh)
        qg = _select_head(q_raw, hh)
        gate = _dot_nt(qg, km_hi) + _dot_nt(qg, km_lo)
        gate = jnp.where(past, gate, -jnp.inf)
        rank = jnp.zeros((t, LANES), jnp.int32)
        for m in range(n_blocks):
            gm = gate[:, m:m + 1]
            ahead = (gm > gate) | ((gm == gate) & (lane > m))
            rank += ahead.astype(jnp.int32)
        chosen = jnp.where(past & (rank < MOBA_TOPK), 1.0, 0.0)

        s = _dot_nt(qh, k_ref[pl.ds(diag, t), :]) - slope * dist0
        s = jnp.where(col <= row, s, NEG_BIG)
        m0 = jnp.max(s, axis=1, keepdims=True)
        p = jnp.exp(s - m0)
        m_ref[hh] = m0
        l_ref[hh] = jnp.sum(p, axis=1, keepdims=True)
        acc_ref[hh] = _dot(p.astype(BF16), v_ref[pl.ds(diag, t), :])

        def body(n, carry, hh=hh, qh=qh, slope=slope, chosen=chosen):
            start = pl.multiple_of(n * t, t)
            picked = jnp.sum(jnp.where(lane == n, chosen, 0.0), axis=1, keepdims=True)
            offset = ((i - n) * t).astype(F32)
            s = _dot_nt(qh, k_ref[pl.ds(start, t), :]) - slope * (dist0 + offset)
            s = jnp.where(picked > 0.0, s, NEG_BIG)
            m_old = m_ref[hh]
            m_new = jnp.maximum(m_old, jnp.max(s, axis=1, keepdims=True))
            alpha = jnp.exp(m_old - m_new)
            p = jnp.exp(s - m_new)
            m_ref[hh] = m_new
            l_ref[hh] = alpha * l_ref[hh] + jnp.sum(p, axis=1, keepdims=True)
            acc_ref[hh] = alpha * acc_ref[hh] + _dot(p.astype(BF16), v_ref[pl.ds(start, t), :])
            return carry

        lax.fori_loop(0, i, body, 0)
    out0 = acc_ref[0] / l_ref[0]
    out1 = acc_ref[1] / l_ref[1]
    o_ref[...] = jnp.where(_head_lane_mask(0), out0, out1).astype(o_ref.dtype)


def _moba_attention(qkv, slopes, bsz, seq):
    t = MOBA_BLOCK
    n_blocks = seq // t
    assert seq % t == 0 and n_blocks <= LANES
    cb = lambda off: off // LANES
    return pl.pallas_call(
        functools.partial(_moba_kernel, n_blocks=n_blocks),
        grid=(bsz, N_HEADS // HEADS_PER_BLOCK, n_blocks),
        in_specs=[
            pl.BlockSpec(memory_space=pltpu.SMEM),
            pl.BlockSpec((None, t, LANES), lambda b, hp, i: (b, i, cb(COL_MOBA_Q) + hp)),
            pl.BlockSpec((None, seq, LANES), lambda b, hp, i: (b, 0, cb(COL_MOBA_K) + hp)),
            pl.BlockSpec((None, seq, LANES), lambda b, hp, i: (b, 0, cb(COL_MOBA_V) + hp)),
        ],
        out_specs=pl.BlockSpec((None, t, LANES), lambda b, hp, i: (b, i, hp)),
        out_shape=jax.ShapeDtypeStruct((bsz, seq, BRANCH_W), BF16),
        scratch_shapes=[pltpu.VMEM((LANES, LANES), F32),
                        pltpu.VMEM((HEADS_PER_BLOCK, t, LANES), F32),
                        pltpu.VMEM((HEADS_PER_BLOCK, t, 1), F32),
                        pltpu.VMEM((HEADS_PER_BLOCK, t, 1), F32)],
        compiler_params=_params("parallel", "parallel", "arbitrary"),
        name="moba",
    )(slopes, qkv, qkv, qkv)


def _swap_halves(x):
    return jnp.concatenate([x[:, HEAD_DIM:], x[:, :HEAD_DIM]], axis=1)


def _swa_kernel(slopes_ref, sinks_ref, q_ref, k_ref, v_ref, o_ref, *, q_blocks):
    w = SWA_WINDOW
    tile = pl.program_id(1)
    group = N_HEADS // SWA_KV_HEADS
    row = lax.broadcasted_iota(jnp.int32, (w, w), 0)
    col = lax.broadcasted_iota(jnp.int32, (w, w), 1)
    dist_cur = (row - col).astype(F32)
    dist_prev = dist_cur + float(w)
    ok_cur = col <= row
    in_window = col > row
    for qb in range(q_blocks):
        blk = tile * q_blocks + qb
        cur = pl.multiple_of(blk * w, w)
        prev = pl.multiple_of(jnp.maximum(blk - 1, 0) * w, w)
        ok_prev = in_window & (blk > 0)
        k_cur, v_cur = k_ref[pl.ds(cur, w), :], v_ref[pl.ds(cur, w), :]
        k_prev, v_prev = k_ref[pl.ds(prev, w), :], v_ref[pl.ds(prev, w), :]
        for cblk in range(N_HEADS // HEADS_PER_BLOCK):
            q = q_ref[qb * w:(qb + 1) * w, cblk * LANES:(cblk + 1) * LANES] * (HEAD_DIM ** -0.5)
            q_sw = _swap_halves(q)
            outs = []
            for hh in range(HEADS_PER_BLOCK):
                h = cblk * HEADS_PER_BLOCK + hh
                kv = h // group
                qh = _select_head(q if hh == kv else q_sw, kv)
                slope, sink = slopes_ref[h], sinks_ref[h]
                s_cur = jnp.where(ok_cur, _dot_nt(qh, k_cur) - slope * dist_cur, NEG_BIG)
                s_prev = jnp.where(ok_prev, _dot_nt(qh, k_prev) - slope * dist_prev, NEG_BIG)
                m = jnp.maximum(jnp.max(s_cur, axis=1, keepdims=True),
                                jnp.max(s_prev, axis=1, keepdims=True))
                m = jnp.maximum(m, sink)
                e_cur, e_prev = jnp.exp(s_cur - m), jnp.exp(s_prev - m)
                denom = (jnp.sum(e_cur, axis=1, keepdims=True) + jnp.sum(e_prev, axis=1, keepdims=True)
                         + jnp.exp(sink - m))
                p_cur, p_prev = e_cur / denom, e_prev / denom
                o = _dot(p_cur.astype(BF16), v_cur) + _dot(p_prev.astype(BF16), v_prev)
                outs.append(o if hh == kv else pltpu.roll(o, HEAD_DIM, 1))
            o_ref[qb * w:(qb + 1) * w, cblk * LANES:(cblk + 1) * LANES] = jnp.where(
                _head_lane_mask(0), outs[0], outs[1]).astype(o_ref.dtype)


def _swa_attention(qkv, slopes, sinks, bsz, seq):
    w = SWA_WINDOW
    q_blocks = 2
    tq = q_blocks * w
    qw = N_HEADS * HEAD_DIM
    assert SWA_KV_HEADS * HEAD_DIM == LANES
    return pl.pallas_call(
        functools.partial(_swa_kernel, q_blocks=q_blocks),
        grid=(bsz, seq // tq),
        in_specs=[
            pl.BlockSpec(memory_space=pltpu.SMEM),
            pl.BlockSpec(memory_space=pltpu.SMEM),
            pl.BlockSpec((None, tq, qw), lambda b, i: (b, i, COL_SWA_Q // qw)),
            pl.BlockSpec((None, seq, LANES), lambda b, i: (b, 0, COL_SWA_K // LANES)),
            pl.BlockSpec((None, seq, LANES), lambda b, i: (b, 0, COL_SWA_V // LANES)),
        ],
        out_specs=pl.BlockSpec((None, tq, qw), lambda b, i: (b, i, 0)),
        out_shape=jax.ShapeDtypeStruct((bsz, seq, BRANCH_W), BF16),
        compiler_params=_params("parallel", "arbitrary"),
        name="swa",
    )(slopes, sinks, qkv, qkv, qkv)


def _conv_kernel(w_ref, b_ref, c_ref, h_ref, o_ref):
    u = c_ref[...].astype(F32) * h_ref[...].astype(F32)
    pos = lax.broadcasted_iota(jnp.int32, u.shape, 0)
    y = w_ref[CONV_K - 1:CONV_K, :] * u
    for back in range(1, CONV_K):
        shifted = jnp.where(pos >= back, pltpu.roll(u, back, 0), 0.0)
        y += w_ref[CONV_K - 1 - back:CONV_K - back, :] * shifted
    o_ref[...] = (b_ref[...].astype(F32) * y).astype(o_ref.dtype)


def _gated_conv(qkv, conv_w, bsz, seq):
    ch = conv_w.shape[1]
    cb = lambda off: off // LANES
    return pl.pallas_call(
        _conv_kernel,
        grid=(bsz, ch // LANES),
        in_specs=[
            pl.BlockSpec((CONV_K, LANES), lambda b, c: (0, c)),
            pl.BlockSpec((None, seq, LANES), lambda b, c: (b, 0, cb(COL_CONV_B) + c)),
            pl.BlockSpec((None, seq, LANES), lambda b, c: (b, 0, cb(COL_CONV_C) + c)),
            pl.BlockSpec((None, seq, LANES), lambda b, c: (b, 0, cb(COL_CONV_H) + c)),
        ],
        out_specs=pl.BlockSpec((None, seq, LANES), lambda b, c: (b, 0, c)),
        out_shape=jax.ShapeDtypeStruct((bsz, seq, ch), BF16),
        compiler_params=_params("parallel", "parallel"),
        name="gated_conv",
    )(conv_w, qkv, qkv, qkv)


def _merge_kernel(x_ref, g_ref, ya_ref, yb_ref, yc_ref, yd_ref, wg0_ref, wg1_ref, wg2_ref, wg3_ref,
                  wb_ref, wo_ref, o_ref, xn_ref):
    j = pl.program_id(1)

    @pl.when(j == 0)
    def _():
        xn_ref[...] = _rms_norm(x_ref[...], g_ref[...]).astype(BF16)

    xn = xn_ref[...]
    merged = None
    for n, (y_ref, wg_ref) in enumerate(((ya_ref, wg0_ref), (yb_ref, wg1_ref),
                                         (yc_ref, wg2_ref), (yd_ref, wg3_ref))):
        term = jax.nn.sigmoid(_dot(xn, wg_ref[...])) * _dot(y_ref[...], wb_ref[n])
        merged = term if merged is None else merged + term
    d = _dot(merged.astype(BF16), wo_ref[...])

    @pl.when(j == 0)
    def _():
        o_ref[...] = x_ref[...] + d

    @pl.when(j > 0)
    def _():
        o_ref[...] += d


def _merge(x, g, ys, w_gate, w_branch, w_out):
    n, d = x.shape
    tm, tn = min(TOKEN_TILE, n), MERGE_COL_TILE
    col_tiles = d // tn
    y_spec = pl.BlockSpec((tm, BRANCH_W), lambda i, j: (i, 0))
    gate_spec = lambda br: pl.BlockSpec((d, tn), lambda i, j: (0, br * col_tiles + j))
    return pl.pallas_call(
        _merge_kernel,
        grid=(n // tm, col_tiles),
        in_specs=[
            pl.BlockSpec((tm, d), lambda i, j: (i, 0)),
            pl.BlockSpec((1, d), lambda i, j: (0, 0)),
            y_spec, y_spec, y_spec, y_spec,
            gate_spec(0), gate_spec(1), gate_spec(2), gate_spec(3),
            pl.BlockSpec((N_BRANCH, BRANCH_W, tn), lambda i, j: (0, 0, j)),
            pl.BlockSpec((tn, d), lambda i, j: (j, 0)),
        ],
        out_specs=pl.BlockSpec((tm, d), lambda i, j: (i, 0)),
        out_shape=jax.ShapeDtypeStruct((n, d), F32),
        scratch_shapes=[pltpu.VMEM((tm, d), BF16)],
        compiler_params=_params("parallel", "arbitrary"),
        name="merge",
    )(x, g, *ys, w_gate, w_gate, w_gate, w_gate, w_branch, w_out)


def _alibi_slopes(n):
    return 2.0 ** (-8.0 * jnp.arange(1, n + 1, dtype=F32) / n)


def _mixer(h, bsz, seq, norm_g, w_in, conv_w, sinks, w_branch, w_out):
    d = h.shape[1]
    w_qkv = w_in[:, :QKV_COLS].astype(BF16)
    w_gate = w_in[:, QKV_COLS:].astype(BF16)
    slopes = _alibi_slopes(2 * N_HEADS)
    qkv = _qkv_proj(h, norm_g, w_qkv).reshape(bsz, seq, QKV_COLS)
    y_a = _moba_attention(qkv, slopes[N_HEADS:], bsz, seq)
    y_b = _sb_attention(qkv, bsz, seq)
    y_c = _swa_attention(qkv, slopes[:N_HEADS], sinks.astype(F32), bsz, seq)
    y_d = _gated_conv(qkv, conv_w.reshape(CONV_K, -1), bsz, seq)
    ys = [y.reshape(bsz * seq, BRANCH_W) for y in (y_a, y_b, y_c, y_d)]
    return _merge(h, norm_g, ys, w_gate, w_branch.astype(BF16), w_out.astype(BF16))


def kernel(x, ffn1_norm, ffn1_w1, ffn1_w3, ffn1_w2, mix_norm, w_in, conv_w, attn_sinks, w_branch, w_out,
           ffn2_norm, ffn2_w1, ffn2_w3, ffn2_w2, final_norm):
    bsz, seq, d = x.shape
    depth = w_in.shape[0]
    h = x.reshape(bsz * seq, d)
    row = lambda v: v.reshape(1, d)
    final_g = row(final_norm)
    for l in range(depth):
        h = _ffn(h, row(ffn1_norm[l]), ffn1_w1[l].astype(BF16), ffn1_w3[l].astype(BF16),
                 ffn1_w2[l].astype(BF16), final_g, final_norm=False)
        h = _mixer(h, bsz, seq, row(mix_norm[l]), w_in[l], conv_w[l], attn_sinks[l], w_branch[l], w_out[l])
        h = _ffn(h, row(ffn2_norm[l]), ffn2_w1[l].astype(BF16), ffn2_w3[l].astype(BF16),
                 ffn2_w2[l].astype(BF16), final_g, final_norm=(l == depth - 1))
    return h.reshape(bsz, seq, d)
```

```python
import functools

import jax
import jax.numpy as jnp
from jax import lax
from jax.experimental import pallas as pl
from jax.experimental.pallas import tpu as pltpu

F32 = jnp.float32
BF16 = jnp.bfloat16

EPS = 1e-6
HEAD_DIM = 64
LANES = 128
BF16_SUBLANES = 16
HEADS_PER_BLOCK = LANES // HEAD_DIM
MOBA_BLOCK = 256
MOBA_TOPK = 3
SWA_WINDOW = 128
CONV_K = 3
N_BRANCH = 4
BRANCH_W = 512
N_HEADS = 8
SWA_KV_HEADS = 2
SB_TILE = 256
SB_LOG_FLOOR = 110.0

COL_MOBA_Q, COL_MOBA_K, COL_MOBA_V = 0, 512, 1024
COL_SB_Q, COL_SB_K, COL_SB_V = 1536, 2048, 2560
COL_SWA_Q, COL_SWA_K, COL_SWA_V = 3072, 3584, 3712
COL_CONV_B, COL_CONV_C, COL_CONV_H = 3840, 4352, 4864
QKV_COLS = 5376

V7X_VMEM_BYTES = 64 * 1024 * 1024
VMEM_LIMIT = V7X_VMEM_BYTES - 8 * 1024 * 1024
NEG_BIG = -1e30

TOKEN_TILE = 512
FF_TILE = 512
QKV_COL_TILE = 768
MERGE_COL_TILE = 256


def _params(*semantics):
    return pltpu.CompilerParams(dimension_semantics=semantics, vmem_limit_bytes=VMEM_LIMIT)


def _rms_norm(x, g):
    ms = jnp.mean(x * x, axis=-1, keepdims=True)
    return x * lax.rsqrt(ms + EPS) * g


def _dot(a, b):
    return jnp.dot(a, b, preferred_element_type=F32)


def _dot_nt(a, b):
    return lax.dot_general(a, b, (((1,), (1,)), ((), ())), preferred_element_type=F32)


def _split_bf16(x):
    hi = x.astype(BF16)
    lo = (x - hi.astype(F32)).astype(BF16)
    return hi, lo


def _ffn_kernel(x_ref, g_ref, w1_ref, w3_ref, w2_ref, fg_ref, o_ref, xn_ref, *, final_norm):
    f = pl.program_id(1)

    @pl.when(f == 0)
    def _():
        x = x_ref[...]
        xn_ref[...] = _rms_norm(x, g_ref[...]).astype(BF16)
        o_ref[...] = x

    xn = xn_ref[...]
    a = _dot(xn, w1_ref[...])
    b = _dot(xn, w3_ref[...])
    hidden = (a * jax.nn.sigmoid(a) * b).astype(BF16)
    o_ref[...] += 0.5 * _dot(hidden, w2_ref[...])

    if final_norm:
        @pl.when(f == pl.num_programs(1) - 1)
        def _():
            o_ref[...] = _rms_norm(o_ref[...], fg_ref[...])


def _ffn(x, g, w1, w3, w2, final_g, *, final_norm):
    n, d = x.shape
    d_ff = w1.shape[1]
    tm, tf = min(TOKEN_TILE, n), FF_TILE
    return pl.pallas_call(
        functools.partial(_ffn_kernel, final_norm=final_norm),
        grid=(n // tm, d_ff // tf),
        in_specs=[
            pl.BlockSpec((tm, d), lambda i, f: (i, 0)),
            pl.BlockSpec((1, d), lambda i, f: (0, 0)),
            pl.BlockSpec((d, tf), lambda i, f: (0, f)),
            pl.BlockSpec((d, tf), lambda i, f: (0, f)),
            pl.BlockSpec((tf, d), lambda i, f: (f, 0)),
            pl.BlockSpec((1, d), lambda i, f: (0, 0)),
        ],
        out_specs=pl.BlockSpec((tm, d), lambda i, f: (i, 0)),
        out_shape=jax.ShapeDtypeStruct((n, d), F32),
        scratch_shapes=[pltpu.VMEM((tm, d), BF16)],
        compiler_params=_params("parallel", "arbitrary"),
        name="ffn",
    )(x, g, w1, w3, w2, final_g)


def _qkv_kernel(x_ref, g_ref, w_ref, o_ref, xn_ref):
    @pl.when(pl.program_id(1) == 0)
    def _():
        xn_ref[...] = _rms_norm(x_ref[...], g_ref[...]).astype(BF16)

    o_ref[...] = _dot(xn_ref[...], w_ref[...]).astype(BF16)


def _qkv_proj(x, g, w):
    n, d = x.shape
    cols = w.shape[1]
    tm, tn = min(2 * TOKEN_TILE, n), QKV_COL_TILE
    return pl.pallas_call(
        _qkv_kernel,
        grid=(n // tm, cols // tn),
        in_specs=[
            pl.BlockSpec((tm, d), lambda i, j: (i, 0)),
            pl.BlockSpec((1, d), lambda i, j: (0, 0)),
            pl.BlockSpec((d, tn), lambda i, j: (0, j)),
        ],
        out_specs=pl.BlockSpec((tm, tn), lambda i, j: (i, j)),
        out_shape=jax.ShapeDtypeStruct((n, cols), BF16),
        scratch_shapes=[pltpu.VMEM((tm, d), BF16)],
        compiler_params=_params("parallel", "arbitrary"),
        name="qkv_proj",
    )(x, g, w)


def _head_lane_mask(head_in_block):
    lane = lax.broadcasted_iota(jnp.int32, (1, LANES), 1)
    return (lane // HEAD_DIM) == head_in_block


def _select_head(x, head_in_block):
    return jnp.where(_head_lane_mask(head_in_block), x, jnp.zeros_like(x))


def _store_transposed_values(v_ref, vt_ref, tile):
    for n in range(vt_ref.shape[0]):
        vt_ref[n] = v_ref[n * tile:(n + 1) * tile, :].astype(F32).T.astype(BF16)


def _head_rows(vt, head_in_block):
    return vt[head_in_block * HEAD_DIM:(head_in_block + 1) * HEAD_DIM, :]


def _transposed_heads(q, scale):
    qt = q.astype(F32).T * scale
    row = lax.broadcasted_iota(jnp.int32, qt.shape, 0)
    return [jnp.where((row // HEAD_DIM) == hh, qt, 0.0).astype(BF16) for hh in range(HEADS_PER_BLOCK)]


def _store_head_pair(o_ref, out0_t, out1_t):
    o_ref[...] = jnp.concatenate([out0_t, out1_t], axis=0).T.astype(o_ref.dtype)


def _softplus(z):
    return jnp.maximum(z, 0.0) + jnp.log(1.0 + jnp.exp(-jnp.abs(z)))


def _sb_tiles(qh_t, tiles, runs, later_ones, causal):
    chains = [(ti, hh) for ti in range(len(tiles)) for hh in range(HEADS_PER_BLOCK)]
    z = {c: _dot(tiles[c[0]][0], qh_t[c[1]]) for c in chains}
    runs = list(runs)
    sp, parts, later = {}, {}, {}
    for c in chains:
        ti, hh = c
        sp[c] = _softplus(z[c])
        spm = jnp.where(causal, sp[c], 0.0) if tiles[ti][2] else sp[c]
        parts[c] = _split_bf16(spm)
        later[c] = runs[hh]
        runs[hh] = runs[hh] + jnp.sum(spm, axis=0, keepdims=True)
    after = {c: _dot(later_ones, parts[c][0]) + _dot(later_ones, parts[c][1]) + later[c] for c in chains}
    weights = {}
    for c in chains:
        a = jnp.exp(z[c] - sp[c] - after[c])
        if tiles[c[0]][2]:
            a = jnp.where(causal, a, 0.0)
        weights[c] = a.astype(BF16)
    return {c: _dot(_head_rows(tiles[c[0]][1], c[1]), weights[c]) for c in chains}, runs


def _sb_kernel(q_ref, k_ref, v_ref, o_ref, vt_ref, acc_ref):
    t = SB_TILE
    i = pl.program_id(2)

    @pl.when(i == 0)
    def _():
        _store_transposed_values(v_ref, vt_ref, t)

    key = lax.broadcasted_iota(jnp.int32, (t, t), 0)
    qry = lax.broadcasted_iota(jnp.int32, (t, t), 1)
    later_ones = (qry > key).astype(BF16)
    causal = key < qry
    qh_t = _transposed_heads(q_ref[...], HEAD_DIM ** -0.5)
    prev = jnp.maximum(i - 1, 0)
    has_prev = jnp.where(i > 0, 1.0, 0.0)
    k_diag, vt_diag = k_ref[pl.ds(pl.multiple_of(i * t, t), t), :], vt_ref[i]
    k_prev, vt_prev = k_ref[pl.ds(pl.multiple_of(prev * t, t), t), :], vt_ref[prev]
    zeros = jnp.zeros((1, t), F32)
    pv, runs = _sb_tiles(qh_t, [(k_diag, vt_diag, True), (k_prev, vt_prev, False)], [zeros, zeros],
                         later_ones, causal)
    for hh in range(HEADS_PER_BLOCK):
        acc_ref[hh] = pv[0, hh] + has_prev * pv[1, hh]

    def still_visible(runs):
        return jnp.min(jnp.minimum(runs[0], runs[1])) < SB_LOG_FLOOR

    def cond(carry):
        j, _, _, go = carry
        return jnp.logical_and(j <= i, go)

    def body(carry):
        j, run0, run1, _ = carry
        blk = i - j
        k, vt = k_ref[pl.ds(pl.multiple_of(blk * t, t), t), :], vt_ref[blk]
        pv, new_runs = _sb_tiles(qh_t, [(k, vt, False)], [run0, run1], later_ones, causal)
        for hh in range(HEADS_PER_BLOCK):
            acc_ref[hh] += pv[0, hh]
        return j + 1, new_runs[0], new_runs[1], still_visible(new_runs)

    lax.while_loop(cond, body, (jnp.int32(2), runs[0], runs[1], still_visible(runs)))
    _store_head_pair(o_ref, acc_ref[0], acc_ref[1])


def _sb_attention(qkv, bsz, seq):
    t = SB_TILE
    cb = lambda off: off // LANES
    return pl.pallas_call(
        _sb_kernel,
        grid=(bsz, N_HEADS // HEADS_PER_BLOCK, seq // t),
        in_specs=[
            pl.BlockSpec((None, t, LANES), lambda b, hp, i: (b, i, cb(COL_SB_Q) + hp)),
            pl.BlockSpec((None, seq, LANES), lambda b, hp, i: (b, 0, cb(COL_SB_K) + hp)),
            pl.BlockSpec((None, seq, LANES), lambda b, hp, i: (b, 0, cb(COL_SB_V) + hp)),
        ],
        out_specs=pl.BlockSpec((None, t, LANES), lambda b, hp, i: (b, i, hp)),
        out_shape=jax.ShapeDtypeStruct((bsz, seq, BRANCH_W), BF16),
        scratch_shapes=[pltpu.VMEM((seq // t, LANES, t), BF16),
                        pltpu.VMEM((HEADS_PER_BLOCK, HEAD_DIM, t), F32)],
        compiler_params=_params("parallel", "parallel", "arbitrary"),
        name="stick_breaking",
    )(qkv, qkv, qkv)


def _moba_kernel(slopes_ref, q_ref, k_ref, v_ref, o_ref, kmean_ref, vt_ref, bias_ref, chosen_ref, acc_ref,
                 *, n_blocks):
    t = MOBA_BLOCK
    hp = pl.program_id(1)
    i = pl.program_id(2)
    nb_pad = kmean_ref.shape[0]

    @pl.when(i == 0)
    def _():
        _store_transposed_values(v_ref, vt_ref, t)
        kmean_ref[...] = jnp.zeros_like(kmean_ref)
        for n in range(n_blocks):
            kb = k_ref[n * t:(n + 1) * t, :].astype(F32)
            kmean_ref[n:n + 1, :] = jnp.mean(kb, axis=0, keepdims=True)

    key = lax.broadcasted_iota(jnp.int32, (t, t), 0)
    qry = lax.broadcasted_iota(jnp.int32, (t, t), 1)
    dist0 = (qry - key).astype(F32)
    blk_id = lax.broadcasted_iota(jnp.int32, (nb_pad, t), 0)
    past = blk_id < i
    q_raw = q_ref[...]
    qh_t = _transposed_heads(q_raw, HEAD_DIM ** -0.5)
    qg_t = _transposed_heads(q_raw, 1.0)
    km_hi, km_lo = _split_bf16(kmean_ref[...])
    heads = range(HEADS_PER_BLOCK)
    slopes = [slopes_ref[hp * HEADS_PER_BLOCK + hh] for hh in heads]

    def logits(tile):
        k = k_ref[pl.ds(pl.multiple_of(tile * t, t), t), :]
        return [_dot(k, qh_t[hh]) for hh in heads]

    def accumulate(tile, p, alpha):
        vt = vt_ref[tile]
        for hh in heads:
            acc_ref[hh] = alpha[hh] * acc_ref[hh] + _dot(_head_rows(vt, hh), p[hh])

    s_own = logits(i)
    s_first = logits(0)
    gates = [_dot(km_hi, qg_t[hh]) + _dot(km_lo, qg_t[hh]) for hh in heads]
    p, m, l = [], [], []
    for hh in heads:
        bias_ref[hh] = slopes[hh] * dist0
        acc_ref[hh] = jnp.zeros((HEAD_DIM, t), F32)
        gate = jnp.where(past, gates[hh], -jnp.inf)
        rank = jnp.zeros((nb_pad, t), jnp.int32)
        for blk in range(n_blocks):
            gb = gate[blk:blk + 1, :]
            ahead = (gb > gate) | ((gb == gate) & (blk_id > blk))
            rank += ahead.astype(jnp.int32)
        chosen_ref[hh] = jnp.where(past & (rank < MOBA_TOPK), 1.0, 0.0)

        s = jnp.where(key <= qry, s_own[hh] - bias_ref[hh], NEG_BIG)
        m0 = jnp.max(s, axis=0, keepdims=True)
        e = jnp.exp(s - m0)
        p.append(e.astype(BF16))
        m.append(m0)
        l.append(jnp.sum(e, axis=0, keepdims=True))
    ones = jnp.ones((1, t), F32)

    def body(n, carry):
        s_cur, p_prev, alpha_prev, m_old, l_old, prev_tile = carry
        accumulate(prev_tile, p_prev, alpha_prev)
        s_next = logits(jnp.minimum(n + 1, i))
        offset = ((i - n) * t).astype(F32)
        p_new, alpha, m_new, l_new = [], [], [], []
        for hh in heads:
            picked = chosen_ref[hh, pl.ds(n, 1), :]
            shift = jnp.where(picked > 0.0, -slopes[hh] * offset, NEG_BIG)
            s = s_cur[hh] - bias_ref[hh] + shift
            m_new.append(jnp.maximum(m_old[hh], jnp.max(s, axis=0, keepdims=True)))
            alpha.append(jnp.exp(m_old[hh] - m_new[hh]))
            e = jnp.exp(s - m_new[hh])
            l_new.append(alpha[hh] * l_old[hh] + jnp.sum(e, axis=0, keepdims=True))
            p_new.append(e.astype(BF16))
        return tuple(s_next), tuple(p_new), tuple(alpha), tuple(m_new), tuple(l_new), n

    init = (tuple(s_first), tuple(p), (ones, ones), tuple(m), tuple(l), i)
    _, p_last, alpha_last, _, l_last, last_tile = lax.fori_loop(0, i, body, init)
    accumulate(last_tile, p_last, alpha_last)
    _store_head_pair(o_ref, acc_ref[0] / l_last[0], acc_ref[1] / l_last[1])


def _moba_attention(qkv, slopes, bsz, seq):
    t = MOBA_BLOCK
    n_blocks = seq // t
    nb_pad = -(-n_blocks // BF16_SUBLANES) * BF16_SUBLANES
    assert seq % t == 0
    cb = lambda off: off // LANES
    return pl.pallas_call(
        functools.partial(_moba_kernel, n_blocks=n_blocks),
        grid=(bsz, N_HEADS // HEADS_PER_BLOCK, n_blocks),
        in_specs=[
            pl.BlockSpec(memory_space=pltpu.SMEM),
            pl.BlockSpec((None, t, LANES), lambda b, hp, i: (b, i, cb(COL_MOBA_Q) + hp)),
            pl.BlockSpec((None, seq, LANES), lambda b, hp, i: (b, 0, cb(COL_MOBA_K) + hp)),
            pl.BlockSpec((None, seq, LANES), lambda b, hp, i: (b, 0, cb(COL_MOBA_V) + hp)),
        ],
        out_specs=pl.BlockSpec((None, t, LANES), lambda b, hp, i: (b, i, hp)),
        out_shape=jax.ShapeDtypeStruct((bsz, seq, BRANCH_W), BF16),
        scratch_shapes=[pltpu.VMEM((nb_pad, LANES), F32),
                        pltpu.VMEM((n_blocks, LANES, t), BF16),
                        pltpu.VMEM((HEADS_PER_BLOCK, t, t), F32),
                        pltpu.VMEM((HEADS_PER_BLOCK, nb_pad, t), F32),
                        pltpu.VMEM((HEADS_PER_BLOCK, HEAD_DIM, t), F32)],
        compiler_params=_params("parallel", "parallel", "arbitrary"),
        name="moba",
    )(slopes, qkv, qkv, qkv)


def _swap_halves(x):
    return jnp.concatenate([x[:, HEAD_DIM:], x[:, :HEAD_DIM]], axis=1)


def _swa_kernel(slopes_ref, sinks_ref, q_ref, k_ref, v_ref, o_ref, *, q_blocks):
    w = SWA_WINDOW
    tile = pl.program_id(1)
    group = N_HEADS // SWA_KV_HEADS
    row = lax.broadcasted_iota(jnp.int32, (w, w), 0)
    col = lax.broadcasted_iota(jnp.int32, (w, w), 1)
    dist_cur = (row - col).astype(F32)
    dist_prev = dist_cur + float(w)
    ok_cur = col <= row
    in_window = col > row
    for qb in range(q_blocks):
        blk = tile * q_blocks + qb
        cur = pl.multiple_of(blk * w, w)
        prev = pl.multiple_of(jnp.maximum(blk - 1, 0) * w, w)
        ok_prev = in_window & (blk > 0)
        k_cur, v_cur = k_ref[pl.ds(cur, w), :], v_ref[pl.ds(cur, w), :]
        k_prev, v_prev = k_ref[pl.ds(prev, w), :], v_ref[pl.ds(prev, w), :]
        for cblk in range(N_HEADS // HEADS_PER_BLOCK):
            q = q_ref[qb * w:(qb + 1) * w, cblk * LANES:(cblk + 1) * LANES] * (HEAD_DIM ** -0.5)
            q_sw = _swap_halves(q)
            outs = []
            for hh in range(HEADS_PER_BLOCK):
                h = cblk * HEADS_PER_BLOCK + hh
                kv = h // group
                qh = _select_head(q if hh == kv else q_sw, kv)
                slope, sink = slopes_ref[h], sinks_ref[h]
                s_cur = jnp.where(ok_cur, _dot_nt(qh, k_cur) - slope * dist_cur, NEG_BIG)
                s_prev = jnp.where(ok_prev, _dot_nt(qh, k_prev) - slope * dist_prev, NEG_BIG)
                m = jnp.maximum(jnp.max(s_cur, axis=1, keepdims=True),
                                jnp.max(s_prev, axis=1, keepdims=True))
                m = jnp.maximum(m, sink)
                e_cur, e_prev = jnp.exp(s_cur - m), jnp.exp(s_prev - m)
                denom = (jnp.sum(e_cur, axis=1, keepdims=True) + jnp.sum(e_prev, axis=1, keepdims=True)
                         + jnp.exp(sink - m))
                p_cur, p_prev = e_cur / denom, e_prev / denom
                o = _dot(p_cur.astype(BF16), v_cur) + _dot(p_prev.astype(BF16), v_prev)
                outs.append(o if hh == kv else pltpu.roll(o, HEAD_DIM, 1))
            o_ref[qb * w:(qb + 1) * w, cblk * LANES:(cblk + 1) * LANES] = jnp.where(
                _head_lane_mask(0), outs[0], outs[1]).astype(o_ref.dtype)


def _swa_attention(qkv, slopes, sinks, bsz, seq):
    w = SWA_WINDOW
    q_blocks = 2
    tq = q_blocks * w
    qw = N_HEADS * HEAD_DIM
    assert SWA_KV_HEADS * HEAD_DIM == LANES
    return pl.pallas_call(
        functools.partial(_swa_kernel, q_blocks=q_blocks),
        grid=(bsz, seq // tq),
        in_specs=[
            pl.BlockSpec(memory_space=pltpu.SMEM),
            pl.BlockSpec(memory_space=pltpu.SMEM),
            pl.BlockSpec((None, tq, qw), lambda b, i: (b, i, COL_SWA_Q // qw)),
            pl.BlockSpec((None, seq, LANES), lambda b, i: (b, 0, COL_SWA_K // LANES)),
            pl.BlockSpec((None, seq, LANES), lambda b, i: (b, 0, COL_SWA_V // LANES)),
        ],
        out_specs=pl.BlockSpec((None, tq, qw), lambda b, i: (b, i, 0)),
        out_shape=jax.ShapeDtypeStruct((bsz, seq, BRANCH_W), BF16),
        compiler_params=_params("parallel", "arbitrary"),
        name="swa",
    )(slopes, sinks, qkv, qkv, qkv)


def _conv_kernel(w_ref, b_ref, c_ref, h_ref, o_ref):
    u = c_ref[...].astype(F32) * h_ref[...].astype(F32)
    pos = lax.broadcasted_iota(jnp.int32, u.shape, 0)
    y = w_ref[CONV_K - 1:CONV_K, :] * u
    for back in range(1, CONV_K):
        shifted = jnp.where(pos >= back, pltpu.roll(u, back, 0), 0.0)
        y += w_ref[CONV_K - 1 - back:CONV_K - back, :] * shifted
    o_ref[...] = (b_ref[...].astype(F32) * y).astype(o_ref.dtype)


def _gated_conv(qkv, conv_w, bsz, seq):
    ch = conv_w.shape[1]
    cb = lambda off: off // LANES
    return pl.pallas_call(
        _conv_kernel,
        grid=(bsz, ch // LANES),
        in_specs=[
            pl.BlockSpec((CONV_K, LANES), lambda b, c: (0, c)),
            pl.BlockSpec((None, seq, LANES), lambda b, c: (b, 0, cb(COL_CONV_B) + c)),
            pl.BlockSpec((None, seq, LANES), lambda b, c: (b, 0, cb(COL_CONV_C) + c)),
            pl.BlockSpec((None, seq, LANES), lambda b, c: (b, 0, cb(COL_CONV_H) + c)),
        ],
        out_specs=pl.BlockSpec((None, seq, LANES), lambda b, c: (b, 0, c)),
        out_shape=jax.ShapeDtypeStruct((bsz, seq, ch), BF16),
        compiler_params=_params("parallel", "parallel"),
        name="gated_conv",
    )(conv_w, qkv, qkv, qkv)


def _merge_kernel(x_ref, g_ref, ya_ref, yb_ref, yc_ref, yd_ref, wg0_ref, wg1_ref, wg2_ref, wg3_ref,
                  wb_ref, wo_ref, o_ref, xn_ref):
    j = pl.program_id(1)

    @pl.when(j == 0)
    def _():
        x = x_ref[...]
        xn_ref[...] = _rms_norm(x, g_ref[...]).astype(BF16)
        o_ref[...] = x

    xn = xn_ref[...]
    merged = None
    for n, (y_ref, wg_ref) in enumerate(((ya_ref, wg0_ref), (yb_ref, wg1_ref),
                                         (yc_ref, wg2_ref), (yd_ref, wg3_ref))):
        term = jax.nn.sigmoid(_dot(xn, wg_ref[...])) * _dot(y_ref[...], wb_ref[n])
        merged = term if merged is None else merged + term
    o_ref[...] += _dot(merged.astype(BF16), wo_ref[...])


def _merge(x, g, ys, w_gate, w_branch, w_out):
    n, d = x.shape
    tm, tn = min(TOKEN_TILE, n), MERGE_COL_TILE
    col_tiles = d // tn
    y_spec = pl.BlockSpec((tm, BRANCH_W), lambda i, j: (i, 0))
    gate_spec = lambda br: pl.BlockSpec((d, tn), lambda i, j: (0, br * col_tiles + j))
    return pl.pallas_call(
        _merge_kernel,
        grid=(n // tm, col_tiles),
        in_specs=[
            pl.BlockSpec((tm, d), lambda i, j: (i, 0)),
            pl.BlockSpec((1, d), lambda i, j: (0, 0)),
            y_spec, y_spec, y_spec, y_spec,
            gate_spec(0), gate_spec(1), gate_spec(2), gate_spec(3),
            pl.BlockSpec((N_BRANCH, BRANCH_W, tn), lambda i, j: (0, 0, j)),
            pl.BlockSpec((tn, d), lambda i, j: (j, 0)),
        ],
        out_specs=pl.BlockSpec((tm, d), lambda i, j: (i, 0)),
        out_shape=jax.ShapeDtypeStruct((n, d), F32),
        scratch_shapes=[pltpu.VMEM((tm, d), BF16)],
        compiler_params=_params("parallel", "arbitrary"),
        name="merge",
    )(x, g, *ys, w_gate, w_gate, w_gate, w_gate, w_branch, w_out)


def _alibi_slopes(n):
    return 2.0 ** (-8.0 * jnp.arange(1, n + 1, dtype=F32) / n)


def _mixer(h, bsz, seq, norm_g, w_in, conv_w, sinks, w_branch, w_out):
    w_qkv = w_in[:, :QKV_COLS].astype(BF16)
    w_gate = w_in[:, QKV_COLS:].astype(BF16)
    slopes = _alibi_slopes(2 * N_HEADS)
    qkv = _qkv_proj(h, norm_g, w_qkv).reshape(bsz, seq, QKV_COLS)
    y_a = _moba_attention(qkv, slopes[N_HEADS:], bsz, seq)
    y_b = _sb_attention(qkv, bsz, seq)
    y_c = _swa_attention(qkv, slopes[:N_HEADS], sinks.astype(F32), bsz, seq)
    y_d = _gated_conv(qkv, conv_w.reshape(CONV_K, -1), bsz, seq)
    ys = [y.reshape(bsz * seq, BRANCH_W) for y in (y_a, y_b, y_c, y_d)]
    return _merge(h, norm_g, ys, w_gate, w_branch.astype(BF16), w_out.astype(BF16))


def kernel(x, ffn1_norm, ffn1_w1, ffn1_w3, ffn1_w2, mix_norm, w_in, conv_w, attn_sinks, w_branch, w_out,
           ffn2_norm, ffn2_w1, ffn2_w3, ffn2_w2, final_norm):
    bsz, seq, d = x.shape
    depth = w_in.shape[0]
    h = x.reshape(bsz * seq, d)
    row = lambda v: v.reshape(1, d)
    final_g = row(final_norm)
    for l in range(depth):
        h = _ffn(h, row(ffn1_norm[l]), ffn1_w1[l].astype(BF16), ffn1_w3[l].astype(BF16),
                 ffn1_w2[l].astype(BF16), final_g, final_norm=False)
        h = _mixer(h, bsz, seq, row(mix_norm[l]), w_in[l], conv_w[l], attn_sinks[l], w_branch[l], w_out[l])
        h = _ffn(h, row(ffn2_norm[l]), ffn2_w1[l].astype(BF16), ffn2_w3[l].astype(BF16),
                 ffn2_w2[l].astype(BF16), final_g, final_norm=(l == depth - 1))
    return h.reshape(bsz, seq, d)
```

```python
import functools

import jax
import jax.numpy as jnp
from jax import lax
from jax.experimental import pallas as pl
from jax.experimental.pallas import tpu as pltpu

F32 = jnp.float32
BF16 = jnp.bfloat16

EPS = 1e-6
HEAD_DIM = 64
LANES = 128
BF16_SUBLANES = 16
HEADS_PER_BLOCK = LANES // HEAD_DIM
MOBA_BLOCK = 256
MOBA_TOPK = 3
SWA_WINDOW = 128
CONV_K = 3
N_BRANCH = 4
BRANCH_W = 512
N_HEADS = 8
SWA_KV_HEADS = 2
SB_TILE = 256
SB_LOG_FLOOR = 110.0

COL_MOBA_Q, COL_MOBA_K, COL_MOBA_V = 0, 512, 1024
COL_SB_Q, COL_SB_K, COL_SB_V = 1536, 2048, 2560
COL_SWA_Q, COL_SWA_K, COL_SWA_V = 3072, 3584, 3712
COL_CONV_B, COL_CONV_C, COL_CONV_H = 3840, 4352, 4864
QKV_COLS = 5376

V7X_VMEM_BYTES = 64 * 1024 * 1024
VMEM_LIMIT = V7X_VMEM_BYTES - 8 * 1024 * 1024
NEG_BIG = -1e30

FFN_TOKEN_TILE = 1024
FF_TILE = 512
QKV_TOKEN_TILE = 1024
QKV_COL_TILE = 1792
MERGE_TOKEN_TILE = 512
MERGE_COL_TILE = 512


def _params(*semantics):
    return pltpu.CompilerParams(dimension_semantics=semantics, vmem_limit_bytes=VMEM_LIMIT)


def _rms_norm(x, g):
    ms = jnp.mean(x * x, axis=-1, keepdims=True)
    return x * lax.rsqrt(ms + EPS) * g


def _dot(a, b):
    return jnp.dot(a, b, preferred_element_type=F32)


def _split_bf16(x):
    hi = x.astype(BF16)
    lo = (x - hi.astype(F32)).astype(BF16)
    return hi, lo


def _ffn_kernel(x_ref, g_ref, w1_ref, w3_ref, w2_ref, fg_ref, o_ref, xn_ref, *, final_norm):
    f = pl.program_id(1)

    @pl.when(f == 0)
    def _():
        x = x_ref[...]
        xn_ref[...] = _rms_norm(x, g_ref[...]).astype(BF16)
        o_ref[...] = x

    xn = xn_ref[...]
    a = _dot(xn, w1_ref[...])
    b = _dot(xn, w3_ref[...])
    hidden = (a * jax.nn.sigmoid(a) * b).astype(BF16)
    o_ref[...] += 0.5 * _dot(hidden, w2_ref[...])

    if final_norm:
        @pl.when(f == pl.num_programs(1) - 1)
        def _():
            o_ref[...] = _rms_norm(o_ref[...], fg_ref[...])


def _ffn(x, g, w1, w3, w2, final_g, *, final_norm):
    n, d = x.shape
    d_ff = w1.shape[1]
    tm, tf = min(FFN_TOKEN_TILE, n), FF_TILE
    return pl.pallas_call(
        functools.partial(_ffn_kernel, final_norm=final_norm),
        grid=(n // tm, d_ff // tf),
        in_specs=[
            pl.BlockSpec((tm, d), lambda i, f: (i, 0)),
            pl.BlockSpec((1, d), lambda i, f: (0, 0)),
            pl.BlockSpec((d, tf), lambda i, f: (0, f)),
            pl.BlockSpec((d, tf), lambda i, f: (0, f)),
            pl.BlockSpec((tf, d), lambda i, f: (f, 0)),
            pl.BlockSpec((1, d), lambda i, f: (0, 0)),
        ],
        out_specs=pl.BlockSpec((tm, d), lambda i, f: (i, 0)),
        out_shape=jax.ShapeDtypeStruct((n, d), F32),
        scratch_shapes=[pltpu.VMEM((tm, d), BF16)],
        compiler_params=_params("parallel", "arbitrary"),
        name="ffn",
    )(x, g, w1, w3, w2, final_g)


def _qkv_kernel(x_ref, g_ref, w_ref, o_ref, xn_ref):
    @pl.when(pl.program_id(1) == 0)
    def _():
        xn_ref[...] = _rms_norm(x_ref[...], g_ref[...]).astype(BF16)

    o_ref[...] = _dot(xn_ref[...], w_ref[...]).astype(BF16)


def _qkv_proj(x, g, w):
    n, d = x.shape
    cols = w.shape[1]
    tm, tn = min(QKV_TOKEN_TILE, n), QKV_COL_TILE
    return pl.pallas_call(
        _qkv_kernel,
        grid=(n // tm, cols // tn),
        in_specs=[
            pl.BlockSpec((tm, d), lambda i, j: (i, 0)),
            pl.BlockSpec((1, d), lambda i, j: (0, 0)),
            pl.BlockSpec((d, tn), lambda i, j: (0, j)),
        ],
        out_specs=pl.BlockSpec((tm, tn), lambda i, j: (i, j)),
        out_shape=jax.ShapeDtypeStruct((n, cols), BF16),
        scratch_shapes=[pltpu.VMEM((tm, d), BF16)],
        compiler_params=_params("parallel", "arbitrary"),
        name="qkv_proj",
    )(x, g, w)


def _store_transposed_values(v_ref, vt_ref, tile):
    for n in range(vt_ref.shape[0]):
        vt_ref[n] = v_ref[n * tile:(n + 1) * tile, :].astype(F32).T.astype(BF16)


def _head_rows(vt, head_in_block):
    return vt[head_in_block * HEAD_DIM:(head_in_block + 1) * HEAD_DIM, :]


def _transposed_heads(q, scale):
    qt = q.astype(F32).T * scale
    row = lax.broadcasted_iota(jnp.int32, qt.shape, 0)
    return [jnp.where((row // HEAD_DIM) == hh, qt, 0.0).astype(BF16) for hh in range(HEADS_PER_BLOCK)]


def _store_head_pair(o_ref, out0_t, out1_t):
    o_ref[...] = jnp.concatenate([out0_t, out1_t], axis=0).T.astype(o_ref.dtype)


def _softplus(z):
    return jnp.maximum(z, 0.0) + jnp.log(1.0 + jnp.exp(-jnp.abs(z)))


def _sb_tiles(qh_t, tiles, runs, later_ones, causal):
    chains = [(ti, hh) for ti in range(len(tiles)) for hh in range(HEADS_PER_BLOCK)]
    z = {c: _dot(tiles[c[0]][0], qh_t[c[1]]) for c in chains}
    runs = list(runs)
    sp, parts, later = {}, {}, {}
    for c in chains:
        ti, hh = c
        sp[c] = _softplus(z[c])
        spm = jnp.where(causal, sp[c], 0.0) if tiles[ti][2] else sp[c]
        parts[c] = _split_bf16(spm)
        later[c] = runs[hh]
        runs[hh] = runs[hh] + jnp.sum(spm, axis=0, keepdims=True)
    after = {c: _dot(later_ones, parts[c][0]) + _dot(later_ones, parts[c][1]) + later[c] for c in chains}
    weights = {}
    for c in chains:
        a = jnp.exp(z[c] - sp[c] - after[c])
        if tiles[c[0]][2]:
            a = jnp.where(causal, a, 0.0)
        weights[c] = a.astype(BF16)
    return {c: _dot(_head_rows(tiles[c[0]][1], c[1]), weights[c]) for c in chains}, runs


def _sb_kernel(q_ref, k_ref, v_ref, o_ref, vt_ref, acc_ref):
    t = SB_TILE
    i = pl.program_id(2)

    @pl.when(i == 0)
    def _():
        _store_transposed_values(v_ref, vt_ref, t)

    key = lax.broadcasted_iota(jnp.int32, (t, t), 0)
    qry = lax.broadcasted_iota(jnp.int32, (t, t), 1)
    later_ones = (qry > key).astype(BF16)
    causal = key < qry
    qh_t = _transposed_heads(q_ref[...], HEAD_DIM ** -0.5)
    prev = jnp.maximum(i - 1, 0)
    has_prev = jnp.where(i > 0, 1.0, 0.0)
    k_diag, vt_diag = k_ref[pl.ds(pl.multiple_of(i * t, t), t), :], vt_ref[i]
    k_prev, vt_prev = k_ref[pl.ds(pl.multiple_of(prev * t, t), t), :], vt_ref[prev]
    zeros = jnp.zeros((1, t), F32)
    pv, runs = _sb_tiles(qh_t, [(k_diag, vt_diag, True), (k_prev, vt_prev, False)], [zeros, zeros],
                         later_ones, causal)
    for hh in range(HEADS_PER_BLOCK):
        acc_ref[hh] = pv[0, hh] + has_prev * pv[1, hh]

    def still_visible(runs):
        return jnp.min(jnp.minimum(runs[0], runs[1])) < SB_LOG_FLOOR

    def cond(carry):
        j, _, _, go = carry
        return jnp.logical_and(j <= i, go)

    def body(carry):
        j, run0, run1, _ = carry
        blk = i - j
        k, vt = k_ref[pl.ds(pl.multiple_of(blk * t, t), t), :], vt_ref[blk]
        pv, new_runs = _sb_tiles(qh_t, [(k, vt, False)], [run0, run1], later_ones, causal)
        for hh in range(HEADS_PER_BLOCK):
            acc_ref[hh] += pv[0, hh]
        return j + 1, new_runs[0], new_runs[1], still_visible(new_runs)

    lax.while_loop(cond, body, (jnp.int32(2), runs[0], runs[1], still_visible(runs)))
    _store_head_pair(o_ref, acc_ref[0], acc_ref[1])


def _sb_attention(qkv, bsz, seq):
    t = SB_TILE
    cb = lambda off: off // LANES
    return pl.pallas_call(
        _sb_kernel,
        grid=(bsz, N_HEADS // HEADS_PER_BLOCK, seq // t),
        in_specs=[
            pl.BlockSpec((None, t, LANES), lambda b, hp, i: (b, i, cb(COL_SB_Q) + hp)),
            pl.BlockSpec((None, seq, LANES), lambda b, hp, i: (b, 0, cb(COL_SB_K) + hp)),
            pl.BlockSpec((None, seq, LANES), lambda b, hp, i: (b, 0, cb(COL_SB_V) + hp)),
        ],
        out_specs=pl.BlockSpec((None, t, LANES), lambda b, hp, i: (b, i, hp)),
        out_shape=jax.ShapeDtypeStruct((bsz, seq, BRANCH_W), BF16),
        scratch_shapes=[pltpu.VMEM((seq // t, LANES, t), BF16),
                        pltpu.VMEM((HEADS_PER_BLOCK, HEAD_DIM, t), F32)],
        compiler_params=_params("parallel", "parallel", "arbitrary"),
        name="stick_breaking",
    )(qkv, qkv, qkv)


def _moba_kernel(slopes_ref, q_ref, k_ref, v_ref, o_ref, kmean_ref, vt_ref, bias_ref, chosen_ref, acc_ref,
                 *, n_blocks):
    t = MOBA_BLOCK
    hp = pl.program_id(1)
    i = pl.program_id(2)
    nb_pad = kmean_ref.shape[0]

    @pl.when(i == 0)
    def _():
        _store_transposed_values(v_ref, vt_ref, t)
        kmean_ref[...] = jnp.zeros_like(kmean_ref)
        for n in range(n_blocks):
            kb = k_ref[n * t:(n + 1) * t, :].astype(F32)
            kmean_ref[n:n + 1, :] = jnp.mean(kb, axis=0, keepdims=True)

    key = lax.broadcasted_iota(jnp.int32, (t, t), 0)
    qry = lax.broadcasted_iota(jnp.int32, (t, t), 1)
    dist0 = (qry - key).astype(F32)
    blk_id = lax.broadcasted_iota(jnp.int32, (nb_pad, t), 0)
    past = blk_id < i
    q_raw = q_ref[...]
    qh_t = _transposed_heads(q_raw, HEAD_DIM ** -0.5)
    qg_t = _transposed_heads(q_raw, 1.0)
    km_hi, km_lo = _split_bf16(kmean_ref[...])
    heads = range(HEADS_PER_BLOCK)
    slopes = [slopes_ref[hp * HEADS_PER_BLOCK + hh] for hh in heads]

    def logits(tile):
        k = k_ref[pl.ds(pl.multiple_of(tile * t, t), t), :]
        return [_dot(k, qh_t[hh]) for hh in heads]

    def accumulate(tile, p, alpha):
        vt = vt_ref[tile]
        for hh in heads:
            acc_ref[hh] = alpha[hh] * acc_ref[hh] + _dot(_head_rows(vt, hh), p[hh])

    s_own = logits(i)
    s_first = logits(0)
    gates = [_dot(km_hi, qg_t[hh]) + _dot(km_lo, qg_t[hh]) for hh in heads]
    p, m, l = [], [], []
    for hh in heads:
        bias_ref[hh] = slopes[hh] * dist0
        acc_ref[hh] = jnp.zeros((HEAD_DIM, t), F32)
        gate = jnp.where(past, gates[hh], -jnp.inf)
        rank = jnp.zeros((nb_pad, t), jnp.int32)
        for blk in range(n_blocks):
            gb = gate[blk:blk + 1, :]
            ahead = (gb > gate) | ((gb == gate) & (blk_id > blk))
            rank += ahead.astype(jnp.int32)
        chosen_ref[hh] = jnp.where(past & (rank < MOBA_TOPK), 1.0, 0.0)

        s = jnp.where(key <= qry, s_own[hh] - bias_ref[hh], NEG_BIG)
        m0 = jnp.max(s, axis=0, keepdims=True)
        e = jnp.exp(s - m0)
        p.append(e.astype(BF16))
        m.append(m0)
        l.append(jnp.sum(e, axis=0, keepdims=True))
    ones = jnp.ones((1, t), F32)

    def body(n, carry):
        s_cur, p_prev, alpha_prev, m_old, l_old, prev_tile = carry
        accumulate(prev_tile, p_prev, alpha_prev)
        s_next = logits(jnp.minimum(n + 1, i))
        offset = ((i - n) * t).astype(F32)
        p_new, alpha, m_new, l_new = [], [], [], []
        for hh in heads:
            picked = chosen_ref[hh, pl.ds(n, 1), :]
            shift = jnp.where(picked > 0.0, -slopes[hh] * offset, NEG_BIG)
            s = s_cur[hh] - bias_ref[hh] + shift
            m_new.append(jnp.maximum(m_old[hh], jnp.max(s, axis=0, keepdims=True)))
            alpha.append(jnp.exp(m_old[hh] - m_new[hh]))
            e = jnp.exp(s - m_new[hh])
            l_new.append(alpha[hh] * l_old[hh] + jnp.sum(e, axis=0, keepdims=True))
            p_new.append(e.astype(BF16))
        return tuple(s_next), tuple(p_new), tuple(alpha), tuple(m_new), tuple(l_new), n

    init = (tuple(s_first), tuple(p), (ones, ones), tuple(m), tuple(l), i)
    _, p_last, alpha_last, _, l_last, last_tile = lax.fori_loop(0, i, body, init)
    accumulate(last_tile, p_last, alpha_last)
    _store_head_pair(o_ref, acc_ref[0] / l_last[0], acc_ref[1] / l_last[1])


def _moba_attention(qkv, slopes, bsz, seq):
    t = MOBA_BLOCK
    n_blocks = seq // t
    nb_pad = -(-n_blocks // BF16_SUBLANES) * BF16_SUBLANES
    assert seq % t == 0
    cb = lambda off: off // LANES
    return pl.pallas_call(
        functools.partial(_moba_kernel, n_blocks=n_blocks),
        grid=(bsz, N_HEADS // HEADS_PER_BLOCK, n_blocks),
        in_specs=[
            pl.BlockSpec(memory_space=pltpu.SMEM),
            pl.BlockSpec((None, t, LANES), lambda b, hp, i: (b, i, cb(COL_MOBA_Q) + hp)),
            pl.BlockSpec((None, seq, LANES), lambda b, hp, i: (b, 0, cb(COL_MOBA_K) + hp)),
            pl.BlockSpec((None, seq, LANES), lambda b, hp, i: (b, 0, cb(COL_MOBA_V) + hp)),
        ],
        out_specs=pl.BlockSpec((None, t, LANES), lambda b, hp, i: (b, i, hp)),
        out_shape=jax.ShapeDtypeStruct((bsz, seq, BRANCH_W), BF16),
        scratch_shapes=[pltpu.VMEM((nb_pad, LANES), F32),
                        pltpu.VMEM((n_blocks, LANES, t), BF16),
                        pltpu.VMEM((HEADS_PER_BLOCK, t, t), F32),
                        pltpu.VMEM((HEADS_PER_BLOCK, nb_pad, t), F32),
                        pltpu.VMEM((HEADS_PER_BLOCK, HEAD_DIM, t), F32)],
        compiler_params=_params("parallel", "parallel", "arbitrary"),
        name="moba",
    )(slopes, qkv, qkv, qkv)


def _swa_kernel(slope_ref, sink_ref, q_ref, k_ref, v_ref, o_ref, vt_ref, *, q_blocks):
    w = SWA_WINDOW
    tile = pl.program_id(1)
    group = N_HEADS // SWA_KV_HEADS
    gw = group * w

    @pl.when(tile == 0)
    def _():
        _store_transposed_values(v_ref, vt_ref, w)

    key = lax.broadcasted_iota(jnp.int32, (w, gw), 0)
    qry = lax.broadcasted_iota(jnp.int32, (w, gw), 1) % w
    dist = (qry - key).astype(F32)
    ok_cur = key <= qry
    in_window = key > qry
    no_dims = jnp.zeros((HEAD_DIM, w), F32)
    units = [(qb, kv) for qb in range(q_blocks) for kv in range(SWA_KV_HEADS)]

    blk, s_cur, s_prev = {}, {}, {}
    for qb in range(q_blocks):
        blk[qb] = tile * q_blocks + qb
        prev = jnp.maximum(blk[qb] - 1, 0)
        k_cur = k_ref[pl.ds(pl.multiple_of(blk[qb] * w, w), w), :]
        k_prev = k_ref[pl.ds(pl.multiple_of(prev * w, w), w), :]
        qt = q_ref[qb * w:(qb + 1) * w, :].astype(F32).T * (HEAD_DIM ** -0.5)
        for kv in range(SWA_KV_HEADS):
            cols = []
            for j in range(group):
                h = kv * group + j
                qh = qt[h * HEAD_DIM:(h + 1) * HEAD_DIM, :]
                cols.append(jnp.concatenate([qh, no_dims] if kv == 0 else [no_dims, qh], axis=0))
            qg = jnp.concatenate(cols, axis=1).astype(BF16)
            s_cur[qb, kv] = _dot(k_cur, qg)
            s_prev[qb, kv] = _dot(k_prev, qg)

    e_cur, e_prev, denom = {}, {}, {}
    for u in units:
        qb, kv = u
        slope, sink = slope_ref[kv], sink_ref[kv]
        bias = slope * dist
        sc = jnp.where(ok_cur, s_cur[u] - bias, NEG_BIG)
        sp = jnp.where(in_window & (blk[qb] > 0), s_prev[u] - bias - slope * float(w), NEG_BIG)
        m = jnp.maximum(jnp.maximum(jnp.max(sc, axis=0, keepdims=True), jnp.max(sp, axis=0, keepdims=True)), sink)
        ec, ep = jnp.exp(sc - m), jnp.exp(sp - m)
        denom[u] = jnp.sum(ec, axis=0, keepdims=True) + jnp.sum(ep, axis=0, keepdims=True) + jnp.exp(sink - m)
        e_cur[u], e_prev[u] = ec.astype(BF16), ep.astype(BF16)

    out_t = {}
    for u in units:
        qb, kv = u
        vt_cur = _head_rows(vt_ref[blk[qb]], kv)
        vt_prev = _head_rows(vt_ref[jnp.maximum(blk[qb] - 1, 0)], kv)
        out_t[u] = (_dot(vt_cur, e_cur[u]) + _dot(vt_prev, e_prev[u])) / denom[u]
    for qb in range(q_blocks):
        heads_t = [out_t[qb, kv][:, j * w:(j + 1) * w] for kv in range(SWA_KV_HEADS) for j in range(group)]
        o_ref[qb * w:(qb + 1) * w, :] = jnp.concatenate(heads_t, axis=0).T.astype(o_ref.dtype)


def _swa_attention(qkv, slopes, sinks, bsz, seq):
    w = SWA_WINDOW
    q_blocks = 2
    tq = q_blocks * w
    qw = N_HEADS * HEAD_DIM
    assert SWA_KV_HEADS * HEAD_DIM == LANES
    per_col = lambda v: jnp.repeat(v.reshape(SWA_KV_HEADS, N_HEADS // SWA_KV_HEADS), w, axis=1)[:, None, :]
    const_spec = pl.BlockSpec((SWA_KV_HEADS, 1, (N_HEADS // SWA_KV_HEADS) * w), lambda b, i: (0, 0, 0))
    return pl.pallas_call(
        functools.partial(_swa_kernel, q_blocks=q_blocks),
        grid=(bsz, seq // tq),
        in_specs=[
            const_spec,
            const_spec,
            pl.BlockSpec((None, tq, qw), lambda b, i: (b, i, COL_SWA_Q // qw)),
            pl.BlockSpec((None, seq, LANES), lambda b, i: (b, 0, COL_SWA_K // LANES)),
            pl.BlockSpec((None, seq, LANES), lambda b, i: (b, 0, COL_SWA_V // LANES)),
        ],
        out_specs=pl.BlockSpec((None, tq, qw), lambda b, i: (b, i, 0)),
        out_shape=jax.ShapeDtypeStruct((bsz, seq, BRANCH_W), BF16),
        scratch_shapes=[pltpu.VMEM((seq // w, LANES, w), BF16)],
        compiler_params=_params("parallel", "arbitrary"),
        name="swa",
    )(per_col(slopes), per_col(sinks), qkv, qkv, qkv)


def _conv_kernel(w_ref, b_ref, c_ref, h_ref, o_ref):
    u = c_ref[...].astype(F32) * h_ref[...].astype(F32)
    pos = lax.broadcasted_iota(jnp.int32, u.shape, 0)
    y = w_ref[CONV_K - 1:CONV_K, :] * u
    for back in range(1, CONV_K):
        shifted = jnp.where(pos >= back, pltpu.roll(u, back, 0), 0.0)
        y += w_ref[CONV_K - 1 - back:CONV_K - back, :] * shifted
    o_ref[...] = (b_ref[...].astype(F32) * y).astype(o_ref.dtype)


def _gated_conv(qkv, conv_w, bsz, seq):
    ch = conv_w.shape[1]
    cb = lambda off: off // LANES
    return pl.pallas_call(
        _conv_kernel,
        grid=(bsz, ch // LANES),
        in_specs=[
            pl.BlockSpec((CONV_K, LANES), lambda b, c: (0, c)),
            pl.BlockSpec((None, seq, LANES), lambda b, c: (b, 0, cb(COL_CONV_B) + c)),
            pl.BlockSpec((None, seq, LANES), lambda b, c: (b, 0, cb(COL_CONV_C) + c)),
            pl.BlockSpec((None, seq, LANES), lambda b, c: (b, 0, cb(COL_CONV_H) + c)),
        ],
        out_specs=pl.BlockSpec((None, seq, LANES), lambda b, c: (b, 0, c)),
        out_shape=jax.ShapeDtypeStruct((bsz, seq, ch), BF16),
        compiler_params=_params("parallel", "parallel"),
        name="gated_conv",
    )(conv_w, qkv, qkv, qkv)


def _merge_kernel(x_ref, g_ref, ya_ref, yb_ref, yc_ref, yd_ref, wg0_ref, wg1_ref, wg2_ref, wg3_ref,
                  wb_ref, wo_ref, o_ref, xn_ref):
    j = pl.program_id(1)

    @pl.when(j == 0)
    def _():
        x = x_ref[...]
        xn_ref[...] = _rms_norm(x, g_ref[...]).astype(BF16)
        o_ref[...] = x

    xn = xn_ref[...]
    merged = None
    for n, (y_ref, wg_ref) in enumerate(((ya_ref, wg0_ref), (yb_ref, wg1_ref),
                                         (yc_ref, wg2_ref), (yd_ref, wg3_ref))):
        term = jax.nn.sigmoid(_dot(xn, wg_ref[...])) * _dot(y_ref[...], wb_ref[n])
        merged = term if merged is None else merged + term
    o_ref[...] += _dot(merged.astype(BF16), wo_ref[...])


def _merge(x, g, ys, w_gate, w_branch, w_out):
    n, d = x.shape
    tm, tn = min(MERGE_TOKEN_TILE, n), MERGE_COL_TILE
    col_tiles = d // tn
    y_spec = pl.BlockSpec((tm, BRANCH_W), lambda i, j: (i, 0))
    gate_spec = lambda br: pl.BlockSpec((d, tn), lambda i, j: (0, br * col_tiles + j))
    return pl.pallas_call(
        _merge_kernel,
        grid=(n // tm, col_tiles),
        in_specs=[
            pl.BlockSpec((tm, d), lambda i, j: (i, 0)),
            pl.BlockSpec((1, d), lambda i, j: (0, 0)),
            y_spec, y_spec, y_spec, y_spec,
            gate_spec(0), gate_spec(1), gate_spec(2), gate_spec(3),
            pl.BlockSpec((N_BRANCH, BRANCH_W, tn), lambda i, j: (0, 0, j)),
            pl.BlockSpec((tn, d), lambda i, j: (j, 0)),
        ],
        out_specs=pl.BlockSpec((tm, d), lambda i, j: (i, 0)),
        out_shape=jax.ShapeDtypeStruct((n, d), F32),
        scratch_shapes=[pltpu.VMEM((tm, d), BF16)],
        compiler_params=_params("parallel", "arbitrary"),
        name="merge",
    )(x, g, *ys, w_gate, w_gate, w_gate, w_gate, w_branch, w_out)


def _alibi_slopes(n):
    return 2.0 ** (-8.0 * jnp.arange(1, n + 1, dtype=F32) / n)


def _mixer(h, bsz, seq, norm_g, w_in, conv_w, sinks, w_branch, w_out):
    w_qkv = w_in[:, :QKV_COLS].astype(BF16)
    w_gate = w_in[:, QKV_COLS:].astype(BF16)
    slopes = _alibi_slopes(2 * N_HEADS)
    qkv = _qkv_proj(h, norm_g, w_qkv).reshape(bsz, seq, QKV_COLS)
    y_a = _moba_attention(qkv, slopes[N_HEADS:], bsz, seq)
    y_b = _sb_attention(qkv, bsz, seq)
    y_c = _swa_attention(qkv, slopes[:N_HEADS], sinks.astype(F32), bsz, seq)
    y_d = _gated_conv(qkv, conv_w.reshape(CONV_K, -1), bsz, seq)
    ys = [y.reshape(bsz * seq, BRANCH_W) for y in (y_a, y_b, y_c, y_d)]
    return _merge(h, norm_g, ys, w_gate, w_branch.astype(BF16), w_out.astype(BF16))


def kernel(x, ffn1_norm, ffn1_w1, ffn1_w3, ffn1_w2, mix_norm, w_in, conv_w, attn_sinks, w_branch, w_out,
           ffn2_norm, ffn2_w1, ffn2_w3, ffn2_w2, final_norm):
    bsz, seq, d = x.shape
    depth = w_in.shape[0]
    h = x.reshape(bsz * seq, d)
    row = lambda v: v.reshape(1, d)
    final_g = row(final_norm)
    for l in range(depth):
        h = _ffn(h, row(ffn1_norm[l]), ffn1_w1[l].astype(BF16), ffn1_w3[l].astype(BF16),
                 ffn1_w2[l].astype(BF16), final_g, final_norm=False)
        h = _mixer(h, bsz, seq, row(mix_norm[l]), w_in[l], conv_w[l], attn_sinks[l], w_branch[l], w_out[l])
        h = _ffn(h, row(ffn2_norm[l]), ffn2_w1[l].astype(BF16), ffn2_w3[l].astype(BF16),
                 ffn2_w2[l].astype(BF16), final_g, final_norm=(l == depth - 1))
    return h.reshape(bsz, seq, d)
```

```python
import functools

import jax
import jax.numpy as jnp
from jax import lax
from jax.experimental import pallas as pl
from jax.experimental.pallas import tpu as pltpu

F32 = jnp.float32
BF16 = jnp.bfloat16

EPS = 1e-6
HEAD_DIM = 64
LANES = 128
BF16_SUBLANES = 16
HEADS_PER_BLOCK = LANES // HEAD_DIM
MOBA_BLOCK = 256
MOBA_TOPK = 3
SWA_WINDOW = 128
CONV_K = 3
N_BRANCH = 4
BRANCH_W = 512
N_HEADS = 8
SWA_KV_HEADS = 2
SB_TILE = 256
SB_LOG_FLOOR = 110.0

COL_MOBA_Q, COL_MOBA_K, COL_MOBA_V = 0, 512, 1024
COL_SB_Q, COL_SB_K, COL_SB_V = 1536, 2048, 2560
COL_SWA_Q, COL_SWA_K, COL_SWA_V = 3072, 3584, 3712
COL_CONV_B, COL_CONV_C, COL_CONV_H = 3840, 4352, 4864
QKV_COLS = 5376

V7X_VMEM_BYTES = 64 * 1024 * 1024
VMEM_LIMIT = V7X_VMEM_BYTES - 8 * 1024 * 1024
NEG_BIG = -1e30

FFN_TOKEN_TILE = 1024
FF_TILE = 512
QKV_TOKEN_TILE = 1024
QKV_COL_TILE = 1792
MERGE_TOKEN_TILE = 512
MERGE_COL_TILE = 512


def _params(*semantics):
    return pltpu.CompilerParams(dimension_semantics=semantics, vmem_limit_bytes=VMEM_LIMIT)


def _rms_norm(x, g):
    ms = jnp.mean(x * x, axis=-1, keepdims=True)
    return x * lax.rsqrt(ms + EPS) * g


def _dot(a, b):
    return jnp.dot(a, b, preferred_element_type=F32)


def _split_bf16(x):
    hi = x.astype(BF16)
    lo = (x - hi.astype(F32)).astype(BF16)
    return hi, lo


def _ffn_kernel(x_ref, g_ref, w1_ref, w3_ref, w2_ref, fg_ref, o_ref, xn_ref, *, final_norm):
    f = pl.program_id(1)

    @pl.when(f == 0)
    def _():
        x = x_ref[...]
        xn_ref[...] = _rms_norm(x, g_ref[...]).astype(BF16)
        o_ref[...] = x

    xn = xn_ref[...]
    a = _dot(xn, w1_ref[...])
    b = _dot(xn, w3_ref[...])
    hidden = (a * jax.nn.sigmoid(a) * b).astype(BF16)
    o_ref[...] += 0.5 * _dot(hidden, w2_ref[...])

    if final_norm:
        @pl.when(f == pl.num_programs(1) - 1)
        def _():
            o_ref[...] = _rms_norm(o_ref[...], fg_ref[...])


def _ffn(x, g, w1, w3, w2, final_g, *, final_norm):
    n, d = x.shape
    d_ff = w1.shape[1]
    tm, tf = min(FFN_TOKEN_TILE, n), FF_TILE
    return pl.pallas_call(
        functools.partial(_ffn_kernel, final_norm=final_norm),
        grid=(n // tm, d_ff // tf),
        in_specs=[
            pl.BlockSpec((tm, d), lambda i, f: (i, 0)),
            pl.BlockSpec((1, d), lambda i, f: (0, 0)),
            pl.BlockSpec((d, tf), lambda i, f: (0, f)),
            pl.BlockSpec((d, tf), lambda i, f: (0, f)),
            pl.BlockSpec((tf, d), lambda i, f: (f, 0)),
            pl.BlockSpec((1, d), lambda i, f: (0, 0)),
        ],
        out_specs=pl.BlockSpec((tm, d), lambda i, f: (i, 0)),
        out_shape=jax.ShapeDtypeStruct((n, d), F32),
        scratch_shapes=[pltpu.VMEM((tm, d), BF16)],
        compiler_params=_params("parallel", "arbitrary"),
        name="ffn",
    )(x, g, w1, w3, w2, final_g)


def _qkv_kernel(x_ref, g_ref, w_ref, o_ref, xn_ref):
    @pl.when(pl.program_id(1) == 0)
    def _():
        xn_ref[...] = _rms_norm(x_ref[...], g_ref[...]).astype(BF16)

    o_ref[...] = _dot(xn_ref[...], w_ref[...]).astype(BF16)


def _qkv_proj(x, g, w):
    n, d = x.shape
    cols = w.shape[1]
    tm, tn = min(QKV_TOKEN_TILE, n), QKV_COL_TILE
    return pl.pallas_call(
        _qkv_kernel,
        grid=(n // tm, cols // tn),
        in_specs=[
            pl.BlockSpec((tm, d), lambda i, j: (i, 0)),
            pl.BlockSpec((1, d), lambda i, j: (0, 0)),
            pl.BlockSpec((d, tn), lambda i, j: (0, j)),
        ],
        out_specs=pl.BlockSpec((tm, tn), lambda i, j: (i, j)),
        out_shape=jax.ShapeDtypeStruct((n, cols), BF16),
        scratch_shapes=[pltpu.VMEM((tm, d), BF16)],
        compiler_params=_params("parallel", "arbitrary"),
        name="qkv_proj",
    )(x, g, w)


def _store_transposed_values(v_ref, vt_ref, tile):
    for n in range(vt_ref.shape[0]):
        vt_ref[n] = v_ref[n * tile:(n + 1) * tile, :].astype(F32).T.astype(BF16)


def _head_rows(vt, head_in_block):
    return vt[head_in_block * HEAD_DIM:(head_in_block + 1) * HEAD_DIM, :]


def _transposed_heads(q, scale):
    qt = q.astype(F32).T * scale
    row = lax.broadcasted_iota(jnp.int32, qt.shape, 0)
    return [jnp.where((row // HEAD_DIM) == hh, qt, 0.0).astype(BF16) for hh in range(HEADS_PER_BLOCK)]


def _store_head_pair(o_ref, out0_t, out1_t):
    o_ref[...] = jnp.concatenate([out0_t, out1_t], axis=0).T.astype(o_ref.dtype)


def _softplus(z):
    return jnp.maximum(z, 0.0) + jnp.log(1.0 + jnp.exp(-jnp.abs(z)))


def _sb_tiles(qh_t, tiles, runs, later_ones, causal):
    chains = [(ti, hh) for ti in range(len(tiles)) for hh in range(HEADS_PER_BLOCK)]
    z = {c: _dot(tiles[c[0]][0], qh_t[c[1]]) for c in chains}
    runs = list(runs)
    sp, parts, later = {}, {}, {}
    for c in chains:
        ti, hh = c
        sp[c] = _softplus(z[c])
        spm = jnp.where(causal, sp[c], 0.0) if tiles[ti][2] else sp[c]
        parts[c] = _split_bf16(spm)
        later[c] = runs[hh]
        runs[hh] = runs[hh] + jnp.sum(spm, axis=0, keepdims=True)
    after = {c: _dot(later_ones, parts[c][0]) + _dot(later_ones, parts[c][1]) + later[c] for c in chains}
    weights = {}
    for c in chains:
        a = jnp.exp(z[c] - sp[c] - after[c])
        if tiles[c[0]][2]:
            a = jnp.where(causal, a, 0.0)
        weights[c] = a.astype(BF16)
    return {c: _dot(_head_rows(tiles[c[0]][1], c[1]), weights[c]) for c in chains}, runs


def _sb_kernel(q_ref, k_ref, v_ref, o_ref, vt_ref, acc_ref):
    t = SB_TILE
    i = pl.program_id(2)

    @pl.when(i == 0)
    def _():
        _store_transposed_values(v_ref, vt_ref, t)

    key = lax.broadcasted_iota(jnp.int32, (t, t), 0)
    qry = lax.broadcasted_iota(jnp.int32, (t, t), 1)
    later_ones = (qry > key).astype(BF16)
    causal = key < qry
    qh_t = _transposed_heads(q_ref[...], HEAD_DIM ** -0.5)
    prev = jnp.maximum(i - 1, 0)
    has_prev = jnp.where(i > 0, 1.0, 0.0)
    k_diag, vt_diag = k_ref[pl.ds(pl.multiple_of(i * t, t), t), :], vt_ref[i]
    k_prev, vt_prev = k_ref[pl.ds(pl.multiple_of(prev * t, t), t), :], vt_ref[prev]
    zeros = jnp.zeros((1, t), F32)
    pv, runs = _sb_tiles(qh_t, [(k_diag, vt_diag, True), (k_prev, vt_prev, False)], [zeros, zeros],
                         later_ones, causal)
    for hh in range(HEADS_PER_BLOCK):
        acc_ref[hh] = pv[0, hh] + has_prev * pv[1, hh]

    def still_visible(runs):
        return jnp.min(jnp.minimum(runs[0], runs[1])) < SB_LOG_FLOOR

    def cond(carry):
        j, _, _, go = carry
        return jnp.logical_and(j <= i, go)

    def body(carry):
        j, run0, run1, _ = carry
        blk = i - j
        k, vt = k_ref[pl.ds(pl.multiple_of(blk * t, t), t), :], vt_ref[blk]
        pv, new_runs = _sb_tiles(qh_t, [(k, vt, False)], [run0, run1], later_ones, causal)
        for hh in range(HEADS_PER_BLOCK):
            acc_ref[hh] += pv[0, hh]
        return j + 1, new_runs[0], new_runs[1], still_visible(new_runs)

    lax.while_loop(cond, body, (jnp.int32(2), runs[0], runs[1], still_visible(runs)))
    _store_head_pair(o_ref, acc_ref[0], acc_ref[1])


def _sb_attention(qkv, bsz, seq):
    t = SB_TILE
    cb = lambda off: off // LANES
    return pl.pallas_call(
        _sb_kernel,
        grid=(bsz, N_HEADS // HEADS_PER_BLOCK, seq // t),
        in_specs=[
            pl.BlockSpec((None, t, LANES), lambda b, hp, i: (b, i, cb(COL_SB_Q) + hp)),
            pl.BlockSpec((None, seq, LANES), lambda b, hp, i: (b, 0, cb(COL_SB_K) + hp)),
            pl.BlockSpec((None, seq, LANES), lambda b, hp, i: (b, 0, cb(COL_SB_V) + hp)),
        ],
        out_specs=pl.BlockSpec((None, t, LANES), lambda b, hp, i: (b, i, hp)),
        out_shape=jax.ShapeDtypeStruct((bsz, seq, BRANCH_W), BF16),
        scratch_shapes=[pltpu.VMEM((seq // t, LANES, t), BF16),
                        pltpu.VMEM((HEADS_PER_BLOCK, HEAD_DIM, t), F32)],
        compiler_params=_params("parallel", "parallel", "arbitrary"),
        name="stick_breaking",
    )(qkv, qkv, qkv)


def _moba_kernel(slopes_ref, q_ref, k_ref, v_ref, o_ref, kmean_ref, vt_ref, bias_ref, chosen_ref, acc_ref,
                 s_ref, p_ref, *, n_blocks):
    t = MOBA_BLOCK
    hp = pl.program_id(1)
    i = pl.program_id(2)
    nb_pad = kmean_ref.shape[0]

    heads = range(HEADS_PER_BLOCK)
    slopes = [slopes_ref[hp * HEADS_PER_BLOCK + hh] for hh in heads]
    key = lax.broadcasted_iota(jnp.int32, (t, t), 0)
    qry = lax.broadcasted_iota(jnp.int32, (t, t), 1)

    @pl.when(i == 0)
    def _():
        _store_transposed_values(v_ref, vt_ref, t)
        kmean_ref[...] = jnp.zeros_like(kmean_ref)
        for n in range(n_blocks):
            kb = k_ref[n * t:(n + 1) * t, :].astype(F32)
            kmean_ref[n:n + 1, :] = jnp.mean(kb, axis=0, keepdims=True)
        dist0 = (qry - key).astype(F32)
        for hh in heads:
            bias_ref[hh] = slopes[hh] * dist0

    blk_id = lax.broadcasted_iota(jnp.int32, (nb_pad, t), 0)
    past = blk_id < i
    q_raw = q_ref[...]
    qh_t = _transposed_heads(q_raw, HEAD_DIM ** -0.5)
    qg_t = _transposed_heads(q_raw, 1.0)
    km_hi, km_lo = _split_bf16(kmean_ref[...])

    def store_logits(slot, tile):
        k = k_ref[pl.ds(pl.multiple_of(tile * t, t), t), :]
        for hh in heads:
            s_ref[slot, hh] = _dot(k, qh_t[hh])

    def accumulate(slot, tile, alpha):
        vt = vt_ref[tile]
        for hh in heads:
            acc_ref[hh] = alpha[hh] * acc_ref[hh] + _dot(_head_rows(vt, hh), p_ref[slot, hh])

    store_logits(0, 0)
    k_own = k_ref[pl.ds(pl.multiple_of(i * t, t), t), :]
    s_own = [_dot(k_own, qh_t[hh]) for hh in heads]
    gates = [_dot(km_hi, qg_t[hh]) + _dot(km_lo, qg_t[hh]) for hh in heads]
    m, l = [], []
    for hh in heads:
        acc_ref[hh] = jnp.zeros((HEAD_DIM, t), F32)
        gate = jnp.where(past, gates[hh], -jnp.inf)
        rank = jnp.zeros((nb_pad, t), jnp.int32)
        for blk in range(n_blocks):
            gb = gate[blk:blk + 1, :]
            ahead = (gb > gate) | ((gb == gate) & (blk_id > blk))
            rank += ahead.astype(jnp.int32)
        chosen_ref[hh] = jnp.where(past & (rank < MOBA_TOPK), 1.0, 0.0)

        s = jnp.where(key <= qry, s_own[hh] - bias_ref[hh], NEG_BIG)
        m0 = jnp.max(s, axis=0, keepdims=True)
        e = jnp.exp(s - m0)
        p_ref[1, hh] = e.astype(BF16)
        m.append(m0)
        l.append(jnp.sum(e, axis=0, keepdims=True))
    ones = jnp.ones((1, t), F32)

    def step(slot, n, carry):
        alpha_prev, m_old, l_old, prev_tile = carry
        accumulate(1 - slot, prev_tile, alpha_prev)
        store_logits(1 - slot, jnp.minimum(n + 1, i))
        offset = ((i - n) * t).astype(F32)
        alpha, m_new, l_new = [], [], []
        for hh in heads:
            picked = chosen_ref[hh, pl.ds(n, 1), :]
            shift = jnp.where(picked > 0.0, -slopes[hh] * offset, NEG_BIG)
            s = s_ref[slot, hh] - bias_ref[hh]
            m_new.append(jnp.maximum(m_old[hh], jnp.max(s, axis=0, keepdims=True) + shift))
            alpha.append(jnp.exp(m_old[hh] - m_new[hh]))
            e = jnp.exp(s - (m_new[hh] - shift))
            l_new.append(alpha[hh] * l_old[hh] + jnp.sum(e, axis=0, keepdims=True))
            p_ref[slot, hh] = e.astype(BF16)
        return tuple(alpha), tuple(m_new), tuple(l_new), n

    def pair(j, carry):
        return step(1, 2 * j + 1, step(0, 2 * j, carry))

    init = ((ones, ones), tuple(m), tuple(l), i)
    alpha_last, _, l_last, last_tile = lax.fori_loop(0, (i + 1) // 2, pair, init)
    accumulate(1, last_tile, alpha_last)
    _store_head_pair(o_ref, acc_ref[0] / l_last[0], acc_ref[1] / l_last[1])


def _moba_attention(qkv, slopes, bsz, seq):
    t = MOBA_BLOCK
    n_blocks = seq // t
    nb_pad = -(-n_blocks // BF16_SUBLANES) * BF16_SUBLANES
    assert seq % t == 0
    cb = lambda off: off // LANES
    return pl.pallas_call(
        functools.partial(_moba_kernel, n_blocks=n_blocks),
        grid=(bsz, N_HEADS // HEADS_PER_BLOCK, n_blocks),
        in_specs=[
            pl.BlockSpec(memory_space=pltpu.SMEM),
            pl.BlockSpec((None, t, LANES), lambda b, hp, i: (b, i, cb(COL_MOBA_Q) + hp)),
            pl.BlockSpec((None, seq, LANES), lambda b, hp, i: (b, 0, cb(COL_MOBA_K) + hp)),
            pl.BlockSpec((None, seq, LANES), lambda b, hp, i: (b, 0, cb(COL_MOBA_V) + hp)),
        ],
        out_specs=pl.BlockSpec((None, t, LANES), lambda b, hp, i: (b, i, hp)),
        out_shape=jax.ShapeDtypeStruct((bsz, seq, BRANCH_W), BF16),
        scratch_shapes=[pltpu.VMEM((nb_pad, LANES), F32),
                        pltpu.VMEM((n_blocks, LANES, t), BF16),
                        pltpu.VMEM((HEADS_PER_BLOCK, t, t), F32),
                        pltpu.VMEM((HEADS_PER_BLOCK, nb_pad, t), F32),
                        pltpu.VMEM((HEADS_PER_BLOCK, HEAD_DIM, t), F32),
                        pltpu.VMEM((2, HEADS_PER_BLOCK, t, t), F32),
                        pltpu.VMEM((2, HEADS_PER_BLOCK, t, t), BF16)],
        compiler_params=_params("parallel", "parallel", "arbitrary"),
        name="moba",
    )(slopes, qkv, qkv, qkv)


def _swa_kernel(slope_ref, sink_ref, q_ref, k_ref, v_ref, o_ref, vt_ref, *, q_blocks):
    w = SWA_WINDOW
    tile = pl.program_id(1)
    group = N_HEADS // SWA_KV_HEADS
    gw = group * w

    @pl.when(tile == 0)
    def _():
        _store_transposed_values(v_ref, vt_ref, w)

    key = lax.broadcasted_iota(jnp.int32, (w, gw), 0)
    qry = lax.broadcasted_iota(jnp.int32, (w, gw), 1) % w
    dist = (qry - key).astype(F32)
    ok_cur = key <= qry
    in_window = key > qry
    no_dims = jnp.zeros((HEAD_DIM, w), F32)
    units = [(qb, kv) for qb in range(q_blocks) for kv in range(SWA_KV_HEADS)]

    blk, s_cur, s_prev = {}, {}, {}
    for qb in range(q_blocks):
        blk[qb] = tile * q_blocks + qb
        prev = jnp.maximum(blk[qb] - 1, 0)
        k_cur = k_ref[pl.ds(pl.multiple_of(blk[qb] * w, w), w), :]
        k_prev = k_ref[pl.ds(pl.multiple_of(prev * w, w), w), :]
        qt = q_ref[qb * w:(qb + 1) * w, :].astype(F32).T * (HEAD_DIM ** -0.5)
        for kv in range(SWA_KV_HEADS):
            cols = []
            for j in range(group):
                h = kv * group + j
                qh = qt[h * HEAD_DIM:(h + 1) * HEAD_DIM, :]
                cols.append(jnp.concatenate([qh, no_dims] if kv == 0 else [no_dims, qh], axis=0))
            qg = jnp.concatenate(cols, axis=1).astype(BF16)
            s_cur[qb, kv] = _dot(k_cur, qg)
            s_prev[qb, kv] = _dot(k_prev, qg)

    e_cur, e_prev, denom = {}, {}, {}
    for u in units:
        qb, kv = u
        slope, sink = slope_ref[kv], sink_ref[kv]
        bias = slope * dist
        sc = jnp.where(ok_cur, s_cur[u] - bias, NEG_BIG)
        sp = jnp.where(in_window & (blk[qb] > 0), s_prev[u] - bias - slope * float(w), NEG_BIG)
        m = jnp.maximum(jnp.maximum(jnp.max(sc, axis=0, keepdims=True), jnp.max(sp, axis=0, keepdims=True)), sink)
        ec, ep = jnp.exp(sc - m), jnp.exp(sp - m)
        denom[u] = jnp.sum(ec, axis=0, keepdims=True) + jnp.sum(ep, axis=0, keepdims=True) + jnp.exp(sink - m)
        e_cur[u], e_prev[u] = ec.astype(BF16), ep.astype(BF16)

    out_t = {}
    for u in units:
        qb, kv = u
        vt_cur = _head_rows(vt_ref[blk[qb]], kv)
        vt_prev = _head_rows(vt_ref[jnp.maximum(blk[qb] - 1, 0)], kv)
        out_t[u] = (_dot(vt_cur, e_cur[u]) + _dot(vt_prev, e_prev[u])) / denom[u]
    for qb in range(q_blocks):
        heads_t = [out_t[qb, kv][:, j * w:(j + 1) * w] for kv in range(SWA_KV_HEADS) for j in range(group)]
        o_ref[qb * w:(qb + 1) * w, :] = jnp.concatenate(heads_t, axis=0).T.astype(o_ref.dtype)


def _swa_attention(qkv, slopes, sinks, bsz, seq):
    w = SWA_WINDOW
    q_blocks = 2
    tq = q_blocks * w
    qw = N_HEADS * HEAD_DIM
    assert SWA_KV_HEADS * HEAD_DIM == LANES
    per_col = lambda v: jnp.repeat(v.reshape(SWA_KV_HEADS, N_HEADS // SWA_KV_HEADS), w, axis=1)[:, None, :]
    const_spec = pl.BlockSpec((SWA_KV_HEADS, 1, (N_HEADS // SWA_KV_HEADS) * w), lambda b, i: (0, 0, 0))
    return pl.pallas_call(
        functools.partial(_swa_kernel, q_blocks=q_blocks),
        grid=(bsz, seq // tq),
        in_specs=[
            const_spec,
            const_spec,
            pl.BlockSpec((None, tq, qw), lambda b, i: (b, i, COL_SWA_Q // qw)),
            pl.BlockSpec((None, seq, LANES), lambda b, i: (b, 0, COL_SWA_K // LANES)),
            pl.BlockSpec((None, seq, LANES), lambda b, i: (b, 0, COL_SWA_V // LANES)),
        ],
        out_specs=pl.BlockSpec((None, tq, qw), lambda b, i: (b, i, 0)),
        out_shape=jax.ShapeDtypeStruct((bsz, seq, BRANCH_W), BF16),
        scratch_shapes=[pltpu.VMEM((seq // w, LANES, w), BF16)],
        compiler_params=_params("parallel", "arbitrary"),
        name="swa",
    )(per_col(slopes), per_col(sinks), qkv, qkv, qkv)


def _conv_kernel(w_ref, b_ref, c_ref, h_ref, o_ref):
    u = c_ref[...].astype(F32) * h_ref[...].astype(F32)
    pos = lax.broadcasted_iota(jnp.int32, u.shape, 0)
    y = w_ref[CONV_K - 1:CONV_K, :] * u
    for back in range(1, CONV_K):
        shifted = jnp.where(pos >= back, pltpu.roll(u, back, 0), 0.0)
        y += w_ref[CONV_K - 1 - back:CONV_K - back, :] * shifted
    o_ref[...] = (b_ref[...].astype(F32) * y).astype(o_ref.dtype)


def _gated_conv(qkv, conv_w, bsz, seq):
    ch = conv_w.shape[1]
    cb = lambda off: off // LANES
    return pl.pallas_call(
        _conv_kernel,
        grid=(bsz, ch // LANES),
        in_specs=[
            pl.BlockSpec((CONV_K, LANES), lambda b, c: (0, c)),
            pl.BlockSpec((None, seq, LANES), lambda b, c: (b, 0, cb(COL_CONV_B) + c)),
            pl.BlockSpec((None, seq, LANES), lambda b, c: (b, 0, cb(COL_CONV_C) + c)),
            pl.BlockSpec((None, seq, LANES), lambda b, c: (b, 0, cb(COL_CONV_H) + c)),
        ],
        out_specs=pl.BlockSpec((None, seq, LANES), lambda b, c: (b, 0, c)),
        out_shape=jax.ShapeDtypeStruct((bsz, seq, ch), BF16),
        compiler_params=_params("parallel", "parallel"),
        name="gated_conv",
    )(conv_w, qkv, qkv, qkv)


def _merge_kernel(x_ref, g_ref, ya_ref, yb_ref, yc_ref, yd_ref, wg0_ref, wg1_ref, wg2_ref, wg3_ref,
                  wb_ref, wo_ref, o_ref, xn_ref):
    j = pl.program_id(1)

    @pl.when(j == 0)
    def _():
        x = x_ref[...]
        xn_ref[...] = _rms_norm(x, g_ref[...]).astype(BF16)
        o_ref[...] = x

    xn = xn_ref[...]
    merged = None
    for n, (y_ref, wg_ref) in enumerate(((ya_ref, wg0_ref), (yb_ref, wg1_ref),
                                         (yc_ref, wg2_ref), (yd_ref, wg3_ref))):
        term = jax.nn.sigmoid(_dot(xn, wg_ref[...])) * _dot(y_ref[...], wb_ref[n])
        merged = term if merged is None else merged + term
    o_ref[...] += _dot(merged.astype(BF16), wo_ref[...])


def _merge(x, g, ys, w_gate, w_branch, w_out):
    n, d = x.shape
    tm, tn = min(MERGE_TOKEN_TILE, n), MERGE_COL_TILE
    col_tiles = d // tn
    y_spec = pl.BlockSpec((tm, BRANCH_W), lambda i, j: (i, 0))
    gate_spec = lambda br: pl.BlockSpec((d, tn), lambda i, j: (0, br * col_tiles + j))
    return pl.pallas_call(
        _merge_kernel,
        grid=(n // tm, col_tiles),
        in_specs=[
            pl.BlockSpec((tm, d), lambda i, j: (i, 0)),
            pl.BlockSpec((1, d), lambda i, j: (0, 0)),
            y_spec, y_spec, y_spec, y_spec,
            gate_spec(0), gate_spec(1), gate_spec(2), gate_spec(3),
            pl.BlockSpec((N_BRANCH, BRANCH_W, tn), lambda i, j: (0, 0, j)),
            pl.BlockSpec((tn, d), lambda i, j: (j, 0)),
        ],
        out_specs=pl.BlockSpec((tm, d), lambda i, j: (i, 0)),
        out_shape=jax.ShapeDtypeStruct((n, d), F32),
        scratch_shapes=[pltpu.VMEM((tm, d), BF16)],
        compiler_params=_params("parallel", "arbitrary"),
        name="merge",
    )(x, g, *ys, w_gate, w_gate, w_gate, w_gate, w_branch, w_out)


def _alibi_slopes(n):
    return 2.0 ** (-8.0 * jnp.arange(1, n + 1, dtype=F32) / n)


def _mixer(h, bsz, seq, norm_g, w_in, conv_w, sinks, w_branch, w_out):
    w_qkv = w_in[:, :QKV_COLS].astype(BF16)
    w_gate = w_in[:, QKV_COLS:].astype(BF16)
    slopes = _alibi_slopes(2 * N_HEADS)
    qkv = _qkv_proj(h, norm_g, w_qkv).reshape(bsz, seq, QKV_COLS)
    y_a = _moba_attention(qkv, slopes[N_HEADS:], bsz, seq)
    y_b = _sb_attention(qkv, bsz, seq)
    y_c = _swa_attention(qkv, slopes[:N_HEADS], sinks.astype(F32), bsz, seq)
    y_d = _gated_conv(qkv, conv_w.reshape(CONV_K, -1), bsz, seq)
    ys = [y.reshape(bsz * seq, BRANCH_W) for y in (y_a, y_b, y_c, y_d)]
    return _merge(h, norm_g, ys, w_gate, w_branch.astype(BF16), w_out.astype(BF16))


def kernel(x, ffn1_norm, ffn1_w1, ffn1_w3, ffn1_w2, mix_norm, w_in, conv_w, attn_sinks, w_branch, w_out,
           ffn2_norm, ffn2_w1, ffn2_w3, ffn2_w2, final_norm):
    bsz, seq, d = x.shape
    depth = w_in.shape[0]
    h = x.reshape(bsz * seq, d)
    row = lambda v: v.reshape(1, d)
    final_g = row(final_norm)
    for l in range(depth):
        h = _ffn(h, row(ffn1_norm[l]), ffn1_w1[l].astype(BF16), ffn1_w3[l].astype(BF16),
                 ffn1_w2[l].astype(BF16), final_g, final_norm=False)
        h = _mixer(h, bsz, seq, row(mix_norm[l]), w_in[l], conv_w[l], attn_sinks[l], w_branch[l], w_out[l])
        h = _ffn(h, row(ffn2_norm[l]), ffn2_w1[l].astype(BF16), ffn2_w3[l].astype(BF16),
                 ffn2_w2[l].astype(BF16), final_g, final_norm=(l == depth - 1))
    return h.reshape(bsz, seq, d)
```

```python
import functools

import jax
import jax.numpy as jnp
from jax import lax
from jax.experimental import pallas as pl
from jax.experimental.pallas import tpu as pltpu

F32 = jnp.float32
BF16 = jnp.bfloat16

EPS = 1e-6
HEAD_DIM = 64
LANES = 128
BF16_SUBLANES = 16
ATTN_HEADS = 4
ATTN_LANES = ATTN_HEADS * HEAD_DIM
MOBA_BLOCK = 256
MOBA_TOPK = 3
SWA_WINDOW = 128
CONV_K = 3
N_BRANCH = 4
BRANCH_W = 512
N_HEADS = 8
SWA_KV_HEADS = 2
SB_TILE = 256
SB_LOG_FLOOR = 110.0

COL_MOBA_Q, COL_MOBA_K, COL_MOBA_V = 0, 512, 1024
COL_SB_Q, COL_SB_K, COL_SB_V = 1536, 2048, 2560
COL_SWA_Q, COL_SWA_K, COL_SWA_V = 3072, 3584, 3712
COL_CONV_B, COL_CONV_C, COL_CONV_H = 3840, 4352, 4864
QKV_COLS = 5376

V7X_VMEM_BYTES = 64 * 1024 * 1024
VMEM_LIMIT = V7X_VMEM_BYTES - 8 * 1024 * 1024
NEG_BIG = -1e30

FFN_TOKEN_TILE = 1024
FF_TILE = 512
QKV_TOKEN_TILE = 1024
QKV_COL_TILE = 1792
MERGE_TOKEN_TILE = 512
MERGE_COL_TILE = 512


def _params(*semantics):
    return pltpu.CompilerParams(dimension_semantics=semantics, vmem_limit_bytes=VMEM_LIMIT)


def _rms_norm(x, g):
    ms = jnp.mean(x * x, axis=-1, keepdims=True)
    return x * lax.rsqrt(ms + EPS) * g


def _dot(a, b):
    return jnp.dot(a, b, preferred_element_type=F32)


def _split_bf16(x):
    hi = x.astype(BF16)
    lo = (x - hi.astype(F32)).astype(BF16)
    return hi, lo


def _ffn_kernel(x_ref, g_ref, w1_ref, w3_ref, w2_ref, fg_ref, o_ref, xn_ref, *, final_norm):
    f = pl.program_id(1)

    @pl.when(f == 0)
    def _():
        x = x_ref[...]
        xn_ref[...] = _rms_norm(x, g_ref[...]).astype(BF16)
        o_ref[...] = x

    xn = xn_ref[...]
    a = _dot(xn, w1_ref[...])
    b = _dot(xn, w3_ref[...])
    hidden = (a * jax.nn.sigmoid(a) * b).astype(BF16)
    o_ref[...] += 0.5 * _dot(hidden, w2_ref[...])

    if final_norm:
        @pl.when(f == pl.num_programs(1) - 1)
        def _():
            o_ref[...] = _rms_norm(o_ref[...], fg_ref[...])


def _ffn(x, g, w1, w3, w2, final_g, *, final_norm):
    n, d = x.shape
    d_ff = w1.shape[1]
    tm, tf = min(FFN_TOKEN_TILE, n), FF_TILE
    return pl.pallas_call(
        functools.partial(_ffn_kernel, final_norm=final_norm),
        grid=(n // tm, d_ff // tf),
        in_specs=[
            pl.BlockSpec((tm, d), lambda i, f: (i, 0)),
            pl.BlockSpec((1, d), lambda i, f: (0, 0)),
            pl.BlockSpec((d, tf), lambda i, f: (0, f)),
            pl.BlockSpec((d, tf), lambda i, f: (0, f)),
            pl.BlockSpec((tf, d), lambda i, f: (f, 0)),
            pl.BlockSpec((1, d), lambda i, f: (0, 0)),
        ],
        out_specs=pl.BlockSpec((tm, d), lambda i, f: (i, 0)),
        out_shape=jax.ShapeDtypeStruct((n, d), F32),
        scratch_shapes=[pltpu.VMEM((tm, d), BF16)],
        compiler_params=_params("parallel", "arbitrary"),
        name="ffn",
    )(x, g, w1, w3, w2, final_g)


def _qkv_kernel(x_ref, g_ref, w_ref, o_ref, xn_ref):
    @pl.when(pl.program_id(1) == 0)
    def _():
        xn_ref[...] = _rms_norm(x_ref[...], g_ref[...]).astype(BF16)

    o_ref[...] = _dot(xn_ref[...], w_ref[...]).astype(BF16)


def _qkv_proj(x, g, w):
    n, d = x.shape
    cols = w.shape[1]
    tm, tn = min(QKV_TOKEN_TILE, n), QKV_COL_TILE
    return pl.pallas_call(
        _qkv_kernel,
        grid=(n // tm, cols // tn),
        in_specs=[
            pl.BlockSpec((tm, d), lambda i, j: (i, 0)),
            pl.BlockSpec((1, d), lambda i, j: (0, 0)),
            pl.BlockSpec((d, tn), lambda i, j: (0, j)),
        ],
        out_specs=pl.BlockSpec((tm, tn), lambda i, j: (i, j)),
        out_shape=jax.ShapeDtypeStruct((n, cols), BF16),
        scratch_shapes=[pltpu.VMEM((tm, d), BF16)],
        compiler_params=_params("parallel", "arbitrary"),
        name="qkv_proj",
    )(x, g, w)


def _store_transposed_values(v_ref, vt_ref, tile):
    for n in range(vt_ref.shape[0]):
        vt_ref[n] = v_ref[n * tile:(n + 1) * tile, :].astype(F32).T.astype(BF16)


def _head_rows(vt, head_in_block):
    return vt[head_in_block * HEAD_DIM:(head_in_block + 1) * HEAD_DIM, :]


def _transposed_heads(q, scale):
    qt = q.astype(F32).T * scale
    row = lax.broadcasted_iota(jnp.int32, qt.shape, 0)
    return [jnp.where((row // HEAD_DIM) == hh, qt, 0.0).astype(BF16) for hh in range(q.shape[1] // HEAD_DIM)]


def _store_heads(o_ref, outs_t):
    o_ref[...] = jnp.concatenate(outs_t, axis=0).T.astype(o_ref.dtype)


def _softplus(z):
    return jnp.maximum(z, 0.0) + jnp.log(1.0 + jnp.exp(-jnp.abs(z)))


def _sb_tiles(qh_t, tiles, runs, later_ones, causal):
    chains = [(ti, hh) for ti in range(len(tiles)) for hh in range(len(qh_t))]
    z = {c: _dot(tiles[c[0]][0], qh_t[c[1]]) for c in chains}
    runs = list(runs)
    sp, parts, later = {}, {}, {}
    for c in chains:
        ti, hh = c
        sp[c] = _softplus(z[c])
        spm = jnp.where(causal, sp[c], 0.0) if tiles[ti][2] else sp[c]
        parts[c] = _split_bf16(spm)
        later[c] = runs[hh]
        runs[hh] = runs[hh] + jnp.sum(spm, axis=0, keepdims=True)
    after = {c: _dot(later_ones, parts[c][0]) + _dot(later_ones, parts[c][1]) + later[c] for c in chains}
    weights = {}
    for c in chains:
        a = jnp.exp(z[c] - sp[c] - after[c])
        if tiles[c[0]][2]:
            a = jnp.where(causal, a, 0.0)
        weights[c] = a.astype(BF16)
    return {c: _dot(_head_rows(tiles[c[0]][1], c[1]), weights[c]) for c in chains}, runs


def _sb_kernel(q_ref, k_ref, v_ref, o_ref, vt_ref, acc_ref):
    t = SB_TILE
    i = pl.program_id(2)

    @pl.when(i == 0)
    def _():
        _store_transposed_values(v_ref, vt_ref, t)

    key = lax.broadcasted_iota(jnp.int32, (t, t), 0)
    qry = lax.broadcasted_iota(jnp.int32, (t, t), 1)
    later_ones = (qry > key).astype(BF16)
    causal = key < qry
    qh_t = _transposed_heads(q_ref[...], HEAD_DIM ** -0.5)
    heads = range(len(qh_t))
    prev = jnp.maximum(i - 1, 0)
    has_prev = jnp.where(i > 0, 1.0, 0.0)
    k_diag, vt_diag = k_ref[pl.ds(pl.multiple_of(i * t, t), t), :], vt_ref[i]
    k_prev, vt_prev = k_ref[pl.ds(pl.multiple_of(prev * t, t), t), :], vt_ref[prev]
    zeros = jnp.zeros((1, t), F32)
    pv, runs = _sb_tiles(qh_t, [(k_diag, vt_diag, True), (k_prev, vt_prev, False)], [zeros for _ in heads],
                         later_ones, causal)
    for hh in heads:
        acc_ref[hh] = pv[0, hh] + has_prev * pv[1, hh]

    def still_visible(runs):
        return jnp.min(functools.reduce(jnp.minimum, runs)) < SB_LOG_FLOOR

    def cond(carry):
        j, _, go = carry
        return jnp.logical_and(j <= i, go)

    def body(carry):
        j, runs, _ = carry
        blk = i - j
        k, vt = k_ref[pl.ds(pl.multiple_of(blk * t, t), t), :], vt_ref[blk]
        pv, new_runs = _sb_tiles(qh_t, [(k, vt, False)], runs, later_ones, causal)
        for hh in heads:
            acc_ref[hh] += pv[0, hh]
        return j + 1, tuple(new_runs), still_visible(new_runs)

    lax.while_loop(cond, body, (jnp.int32(2), tuple(runs), still_visible(runs)))
    _store_heads(o_ref, [acc_ref[hh] for hh in heads])


def _sb_attention(qkv, bsz, seq):
    t = SB_TILE
    cb = lambda off: off // ATTN_LANES
    return pl.pallas_call(
        _sb_kernel,
        grid=(bsz, N_HEADS // ATTN_HEADS, seq // t),
        in_specs=[
            pl.BlockSpec((None, t, ATTN_LANES), lambda b, hg, i: (b, i, cb(COL_SB_Q) + hg)),
            pl.BlockSpec((None, seq, ATTN_LANES), lambda b, hg, i: (b, 0, cb(COL_SB_K) + hg)),
            pl.BlockSpec((None, seq, ATTN_LANES), lambda b, hg, i: (b, 0, cb(COL_SB_V) + hg)),
        ],
        out_specs=pl.BlockSpec((None, t, ATTN_LANES), lambda b, hg, i: (b, i, hg)),
        out_shape=jax.ShapeDtypeStruct((bsz, seq, BRANCH_W), BF16),
        scratch_shapes=[pltpu.VMEM((seq // t, ATTN_LANES, t), BF16),
                        pltpu.VMEM((ATTN_HEADS, HEAD_DIM, t), F32)],
        compiler_params=_params("parallel", "parallel", "arbitrary"),
        name="stick_breaking",
    )(qkv, qkv, qkv)


def _moba_kernel(slopes_ref, q_ref, k_ref, v_ref, o_ref, kmean_ref, vt_ref, bias_ref, chosen_ref, acc_ref,
                 s_ref, p_ref, *, n_blocks):
    t = MOBA_BLOCK
    head_group = pl.program_id(1)
    i = pl.program_id(2)
    nb_pad = kmean_ref.shape[0]
    vt_rows = acc_ref.shape[1]

    heads = range(ATTN_HEADS)
    slopes = [slopes_ref[head_group * ATTN_HEADS + hh] for hh in heads]
    key = lax.broadcasted_iota(jnp.int32, (t, t), 0)
    qry = lax.broadcasted_iota(jnp.int32, (t, t), 1)

    @pl.when(i == 0)
    def _():
        for n in range(n_blocks):
            vt = v_ref[n * t:(n + 1) * t, :].astype(F32).T
            for hh in heads:
                vt_ref[n, hh * vt_rows:hh * vt_rows + HEAD_DIM] = _head_rows(vt, hh).astype(BF16)
                vt_ref[n, hh * vt_rows + HEAD_DIM:(hh + 1) * vt_rows] = jnp.ones((vt_rows - HEAD_DIM, t), BF16)
        kmean_ref[...] = jnp.zeros_like(kmean_ref)
        for n in range(n_blocks):
            kb = k_ref[n * t:(n + 1) * t, :].astype(F32)
            kmean_ref[n:n + 1, :] = jnp.mean(kb, axis=0, keepdims=True)
        dist0 = (qry - key).astype(F32)
        for hh in heads:
            bias_ref[hh] = slopes[hh] * dist0

    blk_id = lax.broadcasted_iota(jnp.int32, (nb_pad, t), 0)
    past = blk_id < i
    q_raw = q_ref[...]
    qh_t = _transposed_heads(q_raw, HEAD_DIM ** -0.5)
    qg_t = _transposed_heads(q_raw, 1.0)
    km_hi, km_lo = _split_bf16(kmean_ref[...])

    def store_logits(slot, tile):
        k = k_ref[pl.ds(pl.multiple_of(tile * t, t), t), :]
        for hh in heads:
            s_ref[slot, hh] = _dot(k, qh_t[hh])

    def accumulate(slot, tile, alpha):
        vt = vt_ref[tile]
        for hh in heads:
            acc_ref[hh] = alpha[hh] * acc_ref[hh] + _dot(vt[hh * vt_rows:(hh + 1) * vt_rows, :], p_ref[slot, hh])

    store_logits(0, 0)
    k_own = k_ref[pl.ds(pl.multiple_of(i * t, t), t), :]
    s_own = [_dot(k_own, qh_t[hh]) for hh in heads]
    gates = [_dot(km_hi, qg_t[hh]) + _dot(km_lo, qg_t[hh]) for hh in heads]
    m = []
    for hh in heads:
        acc_ref[hh] = jnp.zeros((vt_rows, t), F32)
        gate = jnp.where(past, gates[hh], -jnp.inf)
        rank = jnp.zeros((nb_pad, t), jnp.int32)
        for blk in range(n_blocks):
            gb = gate[blk:blk + 1, :]
            ahead = (gb > gate) | ((gb == gate) & (blk_id > blk))
            rank += ahead.astype(jnp.int32)
        chosen_ref[hh] = jnp.where(past & (rank < MOBA_TOPK), 1.0, 0.0)

        s = jnp.where(key <= qry, s_own[hh] - bias_ref[hh], NEG_BIG)
        m0 = jnp.max(s, axis=0, keepdims=True)
        p_ref[1, hh] = jnp.exp(s - m0).astype(BF16)
        m.append(m0)
    ones = jnp.ones((1, t), F32)

    def step(slot, n, carry):
        alpha_prev, m_old, prev_tile = carry
        accumulate(1 - slot, prev_tile, alpha_prev)
        store_logits(1 - slot, jnp.minimum(n + 1, i))
        offset = ((i - n) * t).astype(F32)
        alpha, m_new = [], []
        for hh in heads:
            picked = chosen_ref[hh, pl.ds(n, 1), :]
            shift = jnp.where(picked > 0.0, -slopes[hh] * offset, NEG_BIG)
            s = s_ref[slot, hh] - bias_ref[hh]
            m_new.append(jnp.maximum(m_old[hh], jnp.max(s, axis=0, keepdims=True) + shift))
            alpha.append(jnp.exp(m_old[hh] - m_new[hh]))
            p_ref[slot, hh] = jnp.exp(s - (m_new[hh] - shift)).astype(BF16)
        return tuple(alpha), tuple(m_new), n

    def pair(j, carry):
        return step(1, 2 * j + 1, step(0, 2 * j, carry))

    init = (tuple(ones for _ in heads), tuple(m), i)
    alpha_last, _, last_tile = lax.fori_loop(0, (i + 1) // 2, pair, init)
    accumulate(1, last_tile, alpha_last)
    _store_heads(o_ref, [acc_ref[hh, :HEAD_DIM] / acc_ref[hh, HEAD_DIM:HEAD_DIM + 1] for hh in heads])


def _moba_attention(qkv, slopes, bsz, seq):
    t = MOBA_BLOCK
    n_blocks = seq // t
    nb_pad = -(-n_blocks // BF16_SUBLANES) * BF16_SUBLANES
    vt_rows = HEAD_DIM + BF16_SUBLANES
    assert seq % t == 0
    cb = lambda off: off // ATTN_LANES
    return pl.pallas_call(
        functools.partial(_moba_kernel, n_blocks=n_blocks),
        grid=(bsz, N_HEADS // ATTN_HEADS, n_blocks),
        in_specs=[
            pl.BlockSpec(memory_space=pltpu.SMEM),
            pl.BlockSpec((None, t, ATTN_LANES), lambda b, hg, i: (b, i, cb(COL_MOBA_Q) + hg)),
            pl.BlockSpec((None, seq, ATTN_LANES), lambda b, hg, i: (b, 0, cb(COL_MOBA_K) + hg)),
            pl.BlockSpec((None, seq, ATTN_LANES), lambda b, hg, i: (b, 0, cb(COL_MOBA_V) + hg)),
        ],
        out_specs=pl.BlockSpec((None, t, ATTN_LANES), lambda b, hg, i: (b, i, hg)),
        out_shape=jax.ShapeDtypeStruct((bsz, seq, BRANCH_W), BF16),
        scratch_shapes=[pltpu.VMEM((nb_pad, ATTN_LANES), F32),
                        pltpu.VMEM((n_blocks, ATTN_HEADS * vt_rows, t), BF16),
                        pltpu.VMEM((ATTN_HEADS, t, t), F32),
                        pltpu.VMEM((ATTN_HEADS, nb_pad, t), F32),
                        pltpu.VMEM((ATTN_HEADS, vt_rows, t), F32),
                        pltpu.VMEM((2, ATTN_HEADS, t, t), F32),
                        pltpu.VMEM((2, ATTN_HEADS, t, t), BF16)],
        compiler_params=_params("parallel", "parallel", "arbitrary"),
        name="moba",
    )(slopes, qkv, qkv, qkv)


def _swa_kernel(slope_ref, sink_ref, q_ref, k_ref, v_ref, o_ref, vt_ref, *, q_blocks):
    w = SWA_WINDOW
    tile = pl.program_id(1)
    group = N_HEADS // SWA_KV_HEADS
    gw = group * w

    @pl.when(tile == 0)
    def _():
        _store_transposed_values(v_ref, vt_ref, w)

    key = lax.broadcasted_iota(jnp.int32, (w, gw), 0)
    qry = lax.broadcasted_iota(jnp.int32, (w, gw), 1) % w
    dist = (qry - key).astype(F32)
    ok_cur = key <= qry
    in_window = key > qry
    no_dims = jnp.zeros((HEAD_DIM, w), F32)
    units = [(qb, kv) for qb in range(q_blocks) for kv in range(SWA_KV_HEADS)]

    blk, s_cur, s_prev = {}, {}, {}
    for qb in range(q_blocks):
        blk[qb] = tile * q_blocks + qb
        prev = jnp.maximum(blk[qb] - 1, 0)
        k_cur = k_ref[pl.ds(pl.multiple_of(blk[qb] * w, w), w), :]
        k_prev = k_ref[pl.ds(pl.multiple_of(prev * w, w), w), :]
        qt = q_ref[qb * w:(qb + 1) * w, :].astype(F32).T * (HEAD_DIM ** -0.5)
        for kv in range(SWA_KV_HEADS):
            cols = []
            for j in range(group):
                h = kv * group + j
                qh = qt[h * HEAD_DIM:(h + 1) * HEAD_DIM, :]
                cols.append(jnp.concatenate([qh, no_dims] if kv == 0 else [no_dims, qh], axis=0))
            qg = jnp.concatenate(cols, axis=1).astype(BF16)
            s_cur[qb, kv] = _dot(k_cur, qg)
            s_prev[qb, kv] = _dot(k_prev, qg)

    e_cur, e_prev, denom = {}, {}, {}
    for u in units:
        qb, kv = u
        slope, sink = slope_ref[kv], sink_ref[kv]
        bias = slope * dist
        sc = jnp.where(ok_cur, s_cur[u] - bias, NEG_BIG)
        sp = jnp.where(in_window & (blk[qb] > 0), s_prev[u] - bias - slope * float(w), NEG_BIG)
        m = jnp.maximum(jnp.maximum(jnp.max(sc, axis=0, keepdims=True), jnp.max(sp, axis=0, keepdims=True)), sink)
        ec, ep = jnp.exp(sc - m), jnp.exp(sp - m)
        denom[u] = jnp.sum(ec, axis=0, keepdims=True) + jnp.sum(ep, axis=0, keepdims=True) + jnp.exp(sink - m)
        e_cur[u], e_prev[u] = ec.astype(BF16), ep.astype(BF16)

    out_t = {}
    for u in units:
        qb, kv = u
        vt_cur = _head_rows(vt_ref[blk[qb]], kv)
        vt_prev = _head_rows(vt_ref[jnp.maximum(blk[qb] - 1, 0)], kv)
        out_t[u] = (_dot(vt_cur, e_cur[u]) + _dot(vt_prev, e_prev[u])) / denom[u]
    for qb in range(q_blocks):
        heads_t = [out_t[qb, kv][:, j * w:(j + 1) * w] for kv in range(SWA_KV_HEADS) for j in range(group)]
        o_ref[qb * w:(qb + 1) * w, :] = jnp.concatenate(heads_t, axis=0).T.astype(o_ref.dtype)


def _swa_attention(qkv, slopes, sinks, bsz, seq):
    w = SWA_WINDOW
    q_blocks = 2
    tq = q_blocks * w
    qw = N_HEADS * HEAD_DIM
    assert SWA_KV_HEADS * HEAD_DIM == LANES
    per_col = lambda v: jnp.repeat(v.reshape(SWA_KV_HEADS, N_HEADS // SWA_KV_HEADS), w, axis=1)[:, None, :]
    const_spec = pl.BlockSpec((SWA_KV_HEADS, 1, (N_HEADS // SWA_KV_HEADS) * w), lambda b, i: (0, 0, 0))
    return pl.pallas_call(
        functools.partial(_swa_kernel, q_blocks=q_blocks),
        grid=(bsz, seq // tq),
        in_specs=[
            const_spec,
            const_spec,
            pl.BlockSpec((None, tq, qw), lambda b, i: (b, i, COL_SWA_Q // qw)),
            pl.BlockSpec((None, seq, LANES), lambda b, i: (b, 0, COL_SWA_K // LANES)),
            pl.BlockSpec((None, seq, LANES), lambda b, i: (b, 0, COL_SWA_V // LANES)),
        ],
        out_specs=pl.BlockSpec((None, tq, qw), lambda b, i: (b, i, 0)),
        out_shape=jax.ShapeDtypeStruct((bsz, seq, BRANCH_W), BF16),
        scratch_shapes=[pltpu.VMEM((seq // w, LANES, w), BF16)],
        compiler_params=_params("parallel", "arbitrary"),
        name="swa",
    )(per_col(slopes), per_col(sinks), qkv, qkv, qkv)


def _conv_kernel(w_ref, b_ref, c_ref, h_ref, o_ref):
    u = c_ref[...].astype(F32) * h_ref[...].astype(F32)
    pos = lax.broadcasted_iota(jnp.int32, u.shape, 0)
    y = w_ref[CONV_K - 1:CONV_K, :] * u
    for back in range(1, CONV_K):
        shifted = jnp.where(pos >= back, pltpu.roll(u, back, 0), 0.0)
        y += w_ref[CONV_K - 1 - back:CONV_K - back, :] * shifted
    o_ref[...] = (b_ref[...].astype(F32) * y).astype(o_ref.dtype)


def _gated_conv(qkv, conv_w, bsz, seq):
    ch = conv_w.shape[1]
    cb = lambda off: off // LANES
    return pl.pallas_call(
        _conv_kernel,
        grid=(bsz, ch // LANES),
        in_specs=[
            pl.BlockSpec((CONV_K, LANES), lambda b, c: (0, c)),
            pl.BlockSpec((None, seq, LANES), lambda b, c: (b, 0, cb(COL_CONV_B) + c)),
            pl.BlockSpec((None, seq, LANES), lambda b, c: (b, 0, cb(COL_CONV_C) + c)),
            pl.BlockSpec((None, seq, LANES), lambda b, c: (b, 0, cb(COL_CONV_H) + c)),
        ],
        out_specs=pl.BlockSpec((None, seq, LANES), lambda b, c: (b, 0, c)),
        out_shape=jax.ShapeDtypeStruct((bsz, seq, ch), BF16),
        compiler_params=_params("parallel", "parallel"),
        name="gated_conv",
    )(conv_w, qkv, qkv, qkv)


def _merge_kernel(x_ref, g_ref, ya_ref, yb_ref, yc_ref, yd_ref, wg0_ref, wg1_ref, wg2_ref, wg3_ref,
                  wb_ref, wo_ref, o_ref, xn_ref):
    j = pl.program_id(1)

    @pl.when(j == 0)
    def _():
        x = x_ref[...]
        xn_ref[...] = _rms_norm(x, g_ref[...]).astype(BF16)
        o_ref[...] = x

    xn = xn_ref[...]
    merged = None
    for n, (y_ref, wg_ref) in enumerate(((ya_ref, wg0_ref), (yb_ref, wg1_ref),
                                         (yc_ref, wg2_ref), (yd_ref, wg3_ref))):
        term = jax.nn.sigmoid(_dot(xn, wg_ref[...])) * _dot(y_ref[...], wb_ref[n])
        merged = term if merged is None else merged + term
    o_ref[...] += _dot(merged.astype(BF16), wo_ref[...])


def _merge(x, g, ys, w_gate, w_branch, w_out):
    n, d = x.shape
    tm, tn = min(MERGE_TOKEN_TILE, n), MERGE_COL_TILE
    col_tiles = d // tn
    y_spec = pl.BlockSpec((tm, BRANCH_W), lambda i, j: (i, 0))
    gate_spec = lambda br: pl.BlockSpec((d, tn), lambda i, j: (0, br * col_tiles + j))
    return pl.pallas_call(
        _merge_kernel,
        grid=(n // tm, col_tiles),
        in_specs=[
            pl.BlockSpec((tm, d), lambda i, j: (i, 0)),
            pl.BlockSpec((1, d), lambda i, j: (0, 0)),
            y_spec, y_spec, y_spec, y_spec,
            gate_spec(0), gate_spec(1), gate_spec(2), gate_spec(3),
            pl.BlockSpec((N_BRANCH, BRANCH_W, tn), lambda i, j: (0, 0, j)),
            pl.BlockSpec((tn, d), lambda i, j: (j, 0)),
        ],
        out_specs=pl.BlockSpec((tm, d), lambda i, j: (i, 0)),
        out_shape=jax.ShapeDtypeStruct((n, d), F32),
        scratch_shapes=[pltpu.VMEM((tm, d), BF16)],
        compiler_params=_params("parallel", "arbitrary"),
        name="merge",
    )(x, g, *ys, w_gate, w_gate, w_gate, w_gate, w_branch, w_out)


def _alibi_slopes(n):
    return 2.0 ** (-8.0 * jnp.arange(1, n + 1, dtype=F32) / n)


def _mixer(h, bsz, seq, norm_g, w_in, conv_w, sinks, w_branch, w_out):
    w_qkv = w_in[:, :QKV_COLS].astype(BF16)
    w_gate = w_in[:, QKV_COLS:].astype(BF16)
    slopes = _alibi_slopes(2 * N_HEADS)
    qkv = _qkv_proj(h, norm_g, w_qkv).reshape(bsz, seq, QKV_COLS)
    y_a = _moba_attention(qkv, slopes[N_HEADS:], bsz, seq)
    y_b = _sb_attention(qkv, bsz, seq)
    y_c = _swa_attention(qkv, slopes[:N_HEADS], sinks.astype(F32), bsz, seq)
    y_d = _gated_conv(qkv, conv_w.reshape(CONV_K, -1), bsz, seq)
    ys = [y.reshape(bsz * seq, BRANCH_W) for y in (y_a, y_b, y_c, y_d)]
    return _merge(h, norm_g, ys, w_gate, w_branch.astype(BF16), w_out.astype(BF16))


def kernel(x, ffn1_norm, ffn1_w1, ffn1_w3, ffn1_w2, mix_norm, w_in, conv_w, attn_sinks, w_branch, w_out,
           ffn2_norm, ffn2_w1, ffn2_w3, ffn2_w2, final_norm):
    bsz, seq, d = x.shape
    depth = w_in.shape[0]
    h = x.reshape(bsz * seq, d)
    row = lambda v: v.reshape(1, d)
    final_g = row(final_norm)
    for l in range(depth):
        h = _ffn(h, row(ffn1_norm[l]), ffn1_w1[l].astype(BF16), ffn1_w3[l].astype(BF16),
                 ffn1_w2[l].astype(BF16), final_g, final_norm=False)
        h = _mixer(h, bsz, seq, row(mix_norm[l]), w_in[l], conv_w[l], attn_sinks[l], w_branch[l], w_out[l])
        h = _ffn(h, row(ffn2_norm[l]), ffn2_w1[l].astype(BF16), ffn2_w3[l].astype(BF16),
                 ffn2_w2[l].astype(BF16), final_g, final_norm=(l == depth - 1))
    return h.reshape(bsz, seq, d)
```

```python
import functools

import jax
import jax.numpy as jnp
from jax import lax
from jax.experimental import pallas as pl
from jax.experimental.pallas import tpu as pltpu

F32 = jnp.float32
BF16 = jnp.bfloat16

EPS = 1e-6
HEAD_DIM = 64
LANES = 128
BF16_SUBLANES = 16
ATTN_HEADS = 4
ATTN_LANES = ATTN_HEADS * HEAD_DIM
MOBA_BLOCK = 256
MOBA_TOPK = 3
SWA_WINDOW = 128
CONV_K = 3
N_BRANCH = 4
BRANCH_W = 512
N_HEADS = 8
SWA_KV_HEADS = 2
SB_TILE = 256
SB_LOG_FLOOR = 110.0

COL_MOBA_Q, COL_MOBA_K, COL_MOBA_V = 0, 512, 1024
COL_SB_Q, COL_SB_K, COL_SB_V = 1536, 2048, 2560
COL_SWA_Q, COL_SWA_K, COL_SWA_V = 3072, 3584, 3712
COL_CONV_B, COL_CONV_C, COL_CONV_H = 3840, 4352, 4864
QKV_COLS = 5376

V7X_VMEM_BYTES = 64 * 1024 * 1024
VMEM_LIMIT = V7X_VMEM_BYTES - 8 * 1024 * 1024
NEG_BIG = -1e30

NORM_CHUNKS = 4
FFN_TOKEN_TILE = 1024
FF_TILE = 512
QKV_TOKEN_TILE = 1024
QKV_COL_TILE = 1792
MERGE_TOKEN_TILE = 512
MERGE_COL_TILE = 512
GATE_BLOCK = 256


def _params(*semantics):
    return pltpu.CompilerParams(dimension_semantics=semantics, vmem_limit_bytes=VMEM_LIMIT)


def _rms_norm(x, g):
    ms = jnp.mean(x * x, axis=-1, keepdims=True)
    return x * lax.rsqrt(ms + EPS) * g


def _dot(a, b):
    return jnp.dot(a, b, preferred_element_type=F32)


def _row_chunks(rows):
    chunk = rows // NORM_CHUNKS if rows % NORM_CHUNKS == 0 else rows
    return [pl.ds(start, chunk) for start in range(0, rows, chunk)]


def _split_bf16(x):
    hi = x.astype(BF16)
    lo = (x - hi.astype(F32)).astype(BF16)
    return hi, lo


def _ffn_kernel(x_ref, g_ref, w1_ref, w3_ref, w2_ref, fg_ref, o_ref, xn_ref, *, final_norm):
    f = pl.program_id(1)

    def half_swiglu(xn):
        a = _dot(xn, w1_ref[...])
        b = _dot(xn, w3_ref[...])
        hidden = (a * jax.nn.sigmoid(a) * b).astype(BF16)
        return 0.5 * _dot(hidden, w2_ref[...])

    @pl.when(f == 0)
    def _():
        for rows in _row_chunks(x_ref.shape[0]):
            x = x_ref[rows, :]
            xn = _rms_norm(x, g_ref[...]).astype(BF16)
            xn_ref[rows, :] = xn
            o_ref[rows, :] = x + half_swiglu(xn)

    @pl.when(f > 0)
    def _():
        o_ref[...] += half_swiglu(xn_ref[...])

    if final_norm:
        @pl.when(f == pl.num_programs(1) - 1)
        def _():
            o_ref[...] = _rms_norm(o_ref[...], fg_ref[...])


def _ffn(x, g, w1, w3, w2, final_g, *, layer, final_norm):
    n, d = x.shape
    d_ff = w1.shape[2]
    tm, tf = min(FFN_TOKEN_TILE, n), FF_TILE
    return pl.pallas_call(
        functools.partial(_ffn_kernel, final_norm=final_norm),
        grid=(n // tm, d_ff // tf),
        in_specs=[
            pl.BlockSpec((tm, d), lambda i, f: (i, 0)),
            pl.BlockSpec((1, d), lambda i, f: (0, 0)),
            pl.BlockSpec((None, d, tf), lambda i, f: (layer, 0, f)),
            pl.BlockSpec((None, d, tf), lambda i, f: (layer, 0, f)),
            pl.BlockSpec((None, tf, d), lambda i, f: (layer, f, 0)),
            pl.BlockSpec((1, d), lambda i, f: (0, 0)),
        ],
        out_specs=pl.BlockSpec((tm, d), lambda i, f: (i, 0)),
        out_shape=jax.ShapeDtypeStruct((n, d), F32),
        scratch_shapes=[pltpu.VMEM((tm, d), BF16)],
        compiler_params=_params("parallel", "arbitrary"),
        name="ffn",
    )(x, g, w1, w3, w2, final_g)


def _qkv_kernel(x_ref, g_ref, w_ref, o_ref, xn_ref):
    j = pl.program_id(1)

    @pl.when(j == 0)
    def _():
        for rows in _row_chunks(x_ref.shape[0]):
            xn = _rms_norm(x_ref[rows, :], g_ref[...]).astype(BF16)
            xn_ref[rows, :] = xn
            o_ref[rows, :] = _dot(xn, w_ref[...]).astype(BF16)

    @pl.when(j > 0)
    def _():
        o_ref[...] = _dot(xn_ref[...], w_ref[...]).astype(BF16)


def _qkv_proj(x, g, w_in, layer):
    n, d = x.shape
    tm, tn = min(QKV_TOKEN_TILE, n), QKV_COL_TILE
    assert QKV_COLS % tn == 0
    return pl.pallas_call(
        _qkv_kernel,
        grid=(n // tm, QKV_COLS // tn),
        in_specs=[
            pl.BlockSpec((tm, d), lambda i, j: (i, 0)),
            pl.BlockSpec((1, d), lambda i, j: (0, 0)),
            pl.BlockSpec((None, d, tn), lambda i, j: (layer, 0, j)),
        ],
        out_specs=pl.BlockSpec((tm, tn), lambda i, j: (i, j)),
        out_shape=jax.ShapeDtypeStruct((n, QKV_COLS), BF16),
        scratch_shapes=[pltpu.VMEM((tm, d), BF16)],
        compiler_params=_params("parallel", "arbitrary"),
        name="qkv_proj",
    )(x, g, w_in)


def _store_transposed_values(v_ref, vt_ref, tile):
    for n in range(vt_ref.shape[0]):
        vt_ref[n] = v_ref[n * tile:(n + 1) * tile, :].astype(F32).T.astype(BF16)


def _head_rows(vt, head_in_block):
    return vt[head_in_block * HEAD_DIM:(head_in_block + 1) * HEAD_DIM, :]


def _transposed_heads(q, scale):
    qt = q.astype(F32).T * scale
    row = lax.broadcasted_iota(jnp.int32, qt.shape, 0)
    return [jnp.where((row // HEAD_DIM) == hh, qt, 0.0).astype(BF16) for hh in range(q.shape[1] // HEAD_DIM)]


def _store_heads(o_ref, outs_t):
    o_ref[...] = jnp.concatenate(outs_t, axis=0).T.astype(o_ref.dtype)


def _softplus(z):
    return jnp.maximum(z, 0.0) + jnp.log(1.0 + jnp.exp(-jnp.abs(z)))


def _sb_tiles(qh_t, tiles, runs, later_ones, causal):
    chains = [(ti, hh) for ti in range(len(tiles)) for hh in range(len(qh_t))]
    z = {c: _dot(tiles[c[0]][0], qh_t[c[1]]) for c in chains}
    runs = list(runs)
    sp, parts, later = {}, {}, {}
    for c in chains:
        ti, hh = c
        sp[c] = _softplus(z[c])
        spm = jnp.where(causal, sp[c], 0.0) if tiles[ti][2] else sp[c]
        parts[c] = _split_bf16(spm)
        later[c] = runs[hh]
        runs[hh] = runs[hh] + jnp.sum(spm, axis=0, keepdims=True)
    after = {c: _dot(later_ones, parts[c][0]) + _dot(later_ones, parts[c][1]) + later[c] for c in chains}
    weights = {}
    for c in chains:
        a = jnp.exp(z[c] - sp[c] - after[c])
        if tiles[c[0]][2]:
            a = jnp.where(causal, a, 0.0)
        weights[c] = a.astype(BF16)
    return {c: _dot(_head_rows(tiles[c[0]][1], c[1]), weights[c]) for c in chains}, runs


def _sb_kernel(q_ref, k_ref, v_ref, o_ref, vt_ref, acc_ref):
    t = SB_TILE
    i = pl.program_id(2)

    @pl.when(i == 0)
    def _():
        _store_transposed_values(v_ref, vt_ref, t)

    key = lax.broadcasted_iota(jnp.int32, (t, t), 0)
    qry = lax.broadcasted_iota(jnp.int32, (t, t), 1)
    later_ones = (qry > key).astype(BF16)
    causal = key < qry
    qh_t = _transposed_heads(q_ref[...], HEAD_DIM ** -0.5)
    heads = range(len(qh_t))
    prev = jnp.maximum(i - 1, 0)
    has_prev = jnp.where(i > 0, 1.0, 0.0)
    k_diag, vt_diag = k_ref[pl.ds(pl.multiple_of(i * t, t), t), :], vt_ref[i]
    k_prev, vt_prev = k_ref[pl.ds(pl.multiple_of(prev * t, t), t), :], vt_ref[prev]
    zeros = jnp.zeros((1, t), F32)
    pv, runs = _sb_tiles(qh_t, [(k_diag, vt_diag, True), (k_prev, vt_prev, False)], [zeros for _ in heads],
                         later_ones, causal)
    for hh in heads:
        acc_ref[hh] = pv[0, hh] + has_prev * pv[1, hh]

    def still_visible(runs):
        return jnp.min(functools.reduce(jnp.minimum, runs)) < SB_LOG_FLOOR

    def cond(carry):
        j, _, go = carry
        return jnp.logical_and(j <= i, go)

    def body(carry):
        j, runs, _ = carry
        blk = i - j
        k, vt = k_ref[pl.ds(pl.multiple_of(blk * t, t), t), :], vt_ref[blk]
        pv, new_runs = _sb_tiles(qh_t, [(k, vt, False)], runs, later_ones, causal)
        for hh in heads:
            acc_ref[hh] += pv[0, hh]
        return j + 1, tuple(new_runs), still_visible(new_runs)

    lax.while_loop(cond, body, (jnp.int32(2), tuple(runs), still_visible(runs)))
    _store_heads(o_ref, [acc_ref[hh] for hh in heads])


def _sb_attention(qkv, bsz, seq):
    t = SB_TILE
    cb = lambda off: off // ATTN_LANES
    return pl.pallas_call(
        _sb_kernel,
        grid=(bsz, N_HEADS // ATTN_HEADS, seq // t),
        in_specs=[
            pl.BlockSpec((None, t, ATTN_LANES), lambda b, hg, i: (b, i, cb(COL_SB_Q) + hg)),
            pl.BlockSpec((None, seq, ATTN_LANES), lambda b, hg, i: (b, 0, cb(COL_SB_K) + hg)),
            pl.BlockSpec((None, seq, ATTN_LANES), lambda b, hg, i: (b, 0, cb(COL_SB_V) + hg)),
        ],
        out_specs=pl.BlockSpec((None, t, ATTN_LANES), lambda b, hg, i: (b, i, hg)),
        out_shape=jax.ShapeDtypeStruct((bsz, seq, BRANCH_W), BF16),
        scratch_shapes=[pltpu.VMEM((seq // t, ATTN_LANES, t), BF16),
                        pltpu.VMEM((ATTN_HEADS, HEAD_DIM, t), F32)],
        compiler_params=_params("parallel", "parallel", "arbitrary"),
        name="stick_breaking",
    )(qkv, qkv, qkv)


def _moba_kernel(slopes_ref, q_ref, k_ref, v_ref, o_ref, kmean_ref, vt_ref, bias_ref, chosen_ref, acc_ref,
                 s_ref, p_ref, *, n_blocks):
    t = MOBA_BLOCK
    head_group = pl.program_id(1)
    i = pl.program_id(2)
    nb_pad = kmean_ref.shape[0]
    vt_rows = acc_ref.shape[1]

    heads = range(ATTN_HEADS)
    slopes = [slopes_ref[head_group * ATTN_HEADS + hh] for hh in heads]
    key = lax.broadcasted_iota(jnp.int32, (t, t), 0)
    qry = lax.broadcasted_iota(jnp.int32, (t, t), 1)

    @pl.when(i == 0)
    def _():
        for n in range(n_blocks):
            vt = v_ref[n * t:(n + 1) * t, :].astype(F32).T
            for hh in heads:
                vt_ref[n, hh * vt_rows:hh * vt_rows + HEAD_DIM] = _head_rows(vt, hh).astype(BF16)
                vt_ref[n, hh * vt_rows + HEAD_DIM:(hh + 1) * vt_rows] = jnp.ones((vt_rows - HEAD_DIM, t), BF16)
        kmean_ref[...] = jnp.zeros_like(kmean_ref)
        for n in range(n_blocks):
            kb = k_ref[n * t:(n + 1) * t, :].astype(F32)
            kmean_ref[n:n + 1, :] = jnp.mean(kb, axis=0, keepdims=True)
        dist0 = (qry - key).astype(F32)
        for hh in heads:
            bias_ref[hh] = slopes[hh] * dist0

    blk_id = lax.broadcasted_iota(jnp.int32, (nb_pad, t), 0)
    past = blk_id < i
    q_raw = q_ref[...]
    qh_t = _transposed_heads(q_raw, HEAD_DIM ** -0.5)
    qg_t = _transposed_heads(q_raw, 1.0)
    km_hi, km_lo = _split_bf16(kmean_ref[...])

    def store_logits(slot, tile):
        k = k_ref[pl.ds(pl.multiple_of(tile * t, t), t), :]
        for hh in heads:
            s_ref[slot, hh] = _dot(k, qh_t[hh])

    def accumulate(slot, tile, alpha):
        vt = vt_ref[tile]
        for hh in heads:
            acc_ref[hh] = alpha[hh] * acc_ref[hh] + _dot(vt[hh * vt_rows:(hh + 1) * vt_rows, :], p_ref[slot, hh])

    store_logits(0, 0)
    k_own = k_ref[pl.ds(pl.multiple_of(i * t, t), t), :]
    s_own = [_dot(k_own, qh_t[hh]) for hh in heads]
    gates = [_dot(km_hi, qg_t[hh]) + _dot(km_lo, qg_t[hh]) for hh in heads]
    m = []
    for hh in heads:
        acc_ref[hh] = jnp.zeros((vt_rows, t), F32)
        gate = jnp.where(past, gates[hh], -jnp.inf)
        rank = jnp.zeros((nb_pad, t), jnp.int32)
        for blk in range(n_blocks):
            gb = gate[blk:blk + 1, :]
            ahead = (gb > gate) | ((gb == gate) & (blk_id > blk))
            rank += ahead.astype(jnp.int32)
        chosen_ref[hh] = jnp.where(past & (rank < MOBA_TOPK), 1.0, 0.0)

        s = jnp.where(key <= qry, s_own[hh] - bias_ref[hh], NEG_BIG)
        m0 = jnp.max(s, axis=0, keepdims=True)
        p_ref[1, hh] = jnp.exp(s - m0).astype(BF16)
        m.append(m0)
    ones = jnp.ones((1, t), F32)

    def step(slot, n, carry):
        alpha_prev, m_old, prev_tile = carry
        accumulate(1 - slot, prev_tile, alpha_prev)
        store_logits(1 - slot, jnp.minimum(n + 1, i))
        offset = ((i - n) * t).astype(F32)
        alpha, m_new = [], []
        for hh in heads:
            picked = chosen_ref[hh, pl.ds(n, 1), :]
            shift = jnp.where(picked > 0.0, -slopes[hh] * offset, NEG_BIG)
            s = s_ref[slot, hh] - bias_ref[hh]
            m_new.append(jnp.maximum(m_old[hh], jnp.max(s, axis=0, keepdims=True) + shift))
            alpha.append(jnp.exp(m_old[hh] - m_new[hh]))
            p_ref[slot, hh] = jnp.exp(s - (m_new[hh] - shift)).astype(BF16)
        return tuple(alpha), tuple(m_new), n

    def pair(j, carry):
        return step(1, 2 * j + 1, step(0, 2 * j, carry))

    init = (tuple(ones for _ in heads), tuple(m), i)
    alpha_last, _, last_tile = lax.fori_loop(0, (i + 1) // 2, pair, init)
    accumulate(1, last_tile, alpha_last)
    _store_heads(o_ref, [acc_ref[hh, :HEAD_DIM] / acc_ref[hh, HEAD_DIM:HEAD_DIM + 1] for hh in heads])


def _moba_attention(qkv, slopes, bsz, seq):
    t = MOBA_BLOCK
    n_blocks = seq // t
    nb_pad = -(-n_blocks // BF16_SUBLANES) * BF16_SUBLANES
    vt_rows = HEAD_DIM + BF16_SUBLANES
    assert seq % t == 0
    cb = lambda off: off // ATTN_LANES
    return pl.pallas_call(
        functools.partial(_moba_kernel, n_blocks=n_blocks),
        grid=(bsz, N_HEADS // ATTN_HEADS, n_blocks),
        in_specs=[
            pl.BlockSpec(memory_space=pltpu.SMEM),
            pl.BlockSpec((None, t, ATTN_LANES), lambda b, hg, i: (b, i, cb(COL_MOBA_Q) + hg)),
            pl.BlockSpec((None, seq, ATTN_LANES), lambda b, hg, i: (b, 0, cb(COL_MOBA_K) + hg)),
            pl.BlockSpec((None, seq, ATTN_LANES), lambda b, hg, i: (b, 0, cb(COL_MOBA_V) + hg)),
        ],
        out_specs=pl.BlockSpec((None, t, ATTN_LANES), lambda b, hg, i: (b, i, hg)),
        out_shape=jax.ShapeDtypeStruct((bsz, seq, BRANCH_W), BF16),
        scratch_shapes=[pltpu.VMEM((nb_pad, ATTN_LANES), F32),
                        pltpu.VMEM((n_blocks, ATTN_HEADS * vt_rows, t), BF16),
                        pltpu.VMEM((ATTN_HEADS, t, t), F32),
                        pltpu.VMEM((ATTN_HEADS, nb_pad, t), F32),
                        pltpu.VMEM((ATTN_HEADS, vt_rows, t), F32),
                        pltpu.VMEM((2, ATTN_HEADS, t, t), F32),
                        pltpu.VMEM((2, ATTN_HEADS, t, t), BF16)],
        compiler_params=_params("parallel", "parallel", "arbitrary"),
        name="moba",
    )(slopes, qkv, qkv, qkv)


def _swa_kernel(slope_ref, sink_ref, q_ref, k_ref, v_ref, o_ref, vt_ref, *, q_blocks):
    w = SWA_WINDOW
    tile = pl.program_id(1)
    group = N_HEADS // SWA_KV_HEADS
    gw = group * w

    @pl.when(tile == 0)
    def _():
        _store_transposed_values(v_ref, vt_ref, w)

    key = lax.broadcasted_iota(jnp.int32, (w, gw), 0)
    qry = lax.broadcasted_iota(jnp.int32, (w, gw), 1) % w
    dist = (qry - key).astype(F32)
    ok_cur = key <= qry
    in_window = key > qry
    no_dims = jnp.zeros((HEAD_DIM, w), F32)
    units = [(qb, kv) for qb in range(q_blocks) for kv in range(SWA_KV_HEADS)]

    blk, s_cur, s_prev = {}, {}, {}
    for qb in range(q_blocks):
        blk[qb] = tile * q_blocks + qb
        prev = jnp.maximum(blk[qb] - 1, 0)
        k_cur = k_ref[pl.ds(pl.multiple_of(blk[qb] * w, w), w), :]
        k_prev = k_ref[pl.ds(pl.multiple_of(prev * w, w), w), :]
        qt = q_ref[qb * w:(qb + 1) * w, :].astype(F32).T * (HEAD_DIM ** -0.5)
        for kv in range(SWA_KV_HEADS):
            cols = []
            for j in range(group):
                h = kv * group + j
                qh = qt[h * HEAD_DIM:(h + 1) * HEAD_DIM, :]
                cols.append(jnp.concatenate([qh, no_dims] if kv == 0 else [no_dims, qh], axis=0))
            qg = jnp.concatenate(cols, axis=1).astype(BF16)
            s_cur[qb, kv] = _dot(k_cur, qg)
            s_prev[qb, kv] = _dot(k_prev, qg)

    e_cur, e_prev, denom = {}, {}, {}
    for u in units:
        qb, kv = u
        slope, sink = slope_ref[kv], sink_ref[kv]
        bias = slope * dist
        sc = jnp.where(ok_cur, s_cur[u] - bias, NEG_BIG)
        sp = jnp.where(in_window & (blk[qb] > 0), s_prev[u] - bias - slope * float(w), NEG_BIG)
        m = jnp.maximum(jnp.maximum(jnp.max(sc, axis=0, keepdims=True), jnp.max(sp, axis=0, keepdims=True)), sink)
        ec, ep = jnp.exp(sc - m), jnp.exp(sp - m)
        denom[u] = jnp.sum(ec, axis=0, keepdims=True) + jnp.sum(ep, axis=0, keepdims=True) + jnp.exp(sink - m)
        e_cur[u], e_prev[u] = ec.astype(BF16), ep.astype(BF16)

    out_t = {}
    for u in units:
        qb, kv = u
        vt_cur = _head_rows(vt_ref[blk[qb]], kv)
        vt_prev = _head_rows(vt_ref[jnp.maximum(blk[qb] - 1, 0)], kv)
        out_t[u] = (_dot(vt_cur, e_cur[u]) + _dot(vt_prev, e_prev[u])) / denom[u]
    for qb in range(q_blocks):
        heads_t = [out_t[qb, kv][:, j * w:(j + 1) * w] for kv in range(SWA_KV_HEADS) for j in range(group)]
        o_ref[qb * w:(qb + 1) * w, :] = jnp.concatenate(heads_t, axis=0).T.astype(o_ref.dtype)


def _swa_attention(qkv, slopes, sinks, bsz, seq):
    w = SWA_WINDOW
    q_blocks = 2
    tq = q_blocks * w
    qw = N_HEADS * HEAD_DIM
    assert SWA_KV_HEADS * HEAD_DIM == LANES
    per_col = lambda v: jnp.repeat(v.reshape(SWA_KV_HEADS, N_HEADS // SWA_KV_HEADS), w, axis=1)[:, None, :]
    const_spec = pl.BlockSpec((SWA_KV_HEADS, 1, (N_HEADS // SWA_KV_HEADS) * w), lambda b, i: (0, 0, 0))
    return pl.pallas_call(
        functools.partial(_swa_kernel, q_blocks=q_blocks),
        grid=(bsz, seq // tq),
        in_specs=[
            const_spec,
            const_spec,
            pl.BlockSpec((None, tq, qw), lambda b, i: (b, i, COL_SWA_Q // qw)),
            pl.BlockSpec((None, seq, LANES), lambda b, i: (b, 0, COL_SWA_K // LANES)),
            pl.BlockSpec((None, seq, LANES), lambda b, i: (b, 0, COL_SWA_V // LANES)),
        ],
        out_specs=pl.BlockSpec((None, tq, qw), lambda b, i: (b, i, 0)),
        out_shape=jax.ShapeDtypeStruct((bsz, seq, BRANCH_W), BF16),
        scratch_shapes=[pltpu.VMEM((seq // w, LANES, w), BF16)],
        compiler_params=_params("parallel", "arbitrary"),
        name="swa",
    )(per_col(slopes), per_col(sinks), qkv, qkv, qkv)


def _conv_kernel(w_ref, b_ref, c_ref, h_ref, o_ref):
    u = c_ref[...].astype(F32) * h_ref[...].astype(F32)
    pos = lax.broadcasted_iota(jnp.int32, u.shape, 0)
    y = w_ref[CONV_K - 1:CONV_K, :] * u
    for back in range(1, CONV_K):
        shifted = jnp.where(pos >= back, pltpu.roll(u, back, 0), 0.0)
        y += w_ref[CONV_K - 1 - back:CONV_K - back, :] * shifted
    o_ref[...] = (b_ref[...].astype(F32) * y).astype(o_ref.dtype)


def _gated_conv(qkv, conv_w, bsz, seq):
    ch = conv_w.shape[1]
    cb = lambda off: off // LANES
    return pl.pallas_call(
        _conv_kernel,
        grid=(bsz, ch // LANES),
        in_specs=[
            pl.BlockSpec((CONV_K, LANES), lambda b, c: (0, c)),
            pl.BlockSpec((None, seq, LANES), lambda b, c: (b, 0, cb(COL_CONV_B) + c)),
            pl.BlockSpec((None, seq, LANES), lambda b, c: (b, 0, cb(COL_CONV_C) + c)),
            pl.BlockSpec((None, seq, LANES), lambda b, c: (b, 0, cb(COL_CONV_H) + c)),
        ],
        out_specs=pl.BlockSpec((None, seq, LANES), lambda b, c: (b, 0, c)),
        out_shape=jax.ShapeDtypeStruct((bsz, seq, ch), BF16),
        compiler_params=_params("parallel", "parallel"),
        name="gated_conv",
    )(conv_w, qkv, qkv, qkv)


def _merge_kernel(x_ref, g_ref, *refs):
    y_refs = refs[:N_BRANCH]
    wb_ref, wo_ref, o_ref, xn_ref = refs[-4:]
    wg_refs = refs[N_BRANCH:-4]
    parts = len(wg_refs) // N_BRANCH
    j = pl.program_id(1)

    def mixed(xn, rows):
        out = None
        ys = [y_ref[rows, :] for y_ref in y_refs]
        for p in range(parts):
            cols = slice(p * GATE_BLOCK, (p + 1) * GATE_BLOCK)
            merged = None
            for n in range(N_BRANCH):
                gate = jax.nn.sigmoid(_dot(xn, wg_refs[n * parts + p][...]))
                term = gate * _dot(ys[n], wb_ref[n, :, cols])
                merged = term if merged is None else merged + term
            part = _dot(merged.astype(BF16), wo_ref[cols, :])
            out = part if out is None else out + part
        return out

    @pl.when(j == 0)
    def _():
        for rows in _row_chunks(x_ref.shape[0]):
            x = x_ref[rows, :]
            xn = _rms_norm(x, g_ref[...]).astype(BF16)
            xn_ref[rows, :] = xn
            o_ref[rows, :] = x + mixed(xn, rows)

    @pl.when(j > 0)
    def _():
        o_ref[...] += mixed(xn_ref[...], slice(None))


def _merge(x, g, ys, w_in, w_branch, w_out, layer):
    n, d = x.shape
    tm, tn = min(MERGE_TOKEN_TILE, n), MERGE_COL_TILE
    parts = tn // GATE_BLOCK
    assert QKV_COLS % GATE_BLOCK == 0 and tn % GATE_BLOCK == 0
    y_spec = pl.BlockSpec((tm, BRANCH_W), lambda i, j: (i, 0))

    def gate_spec(branch, part):
        first = (QKV_COLS + branch * d) // GATE_BLOCK + part
        return pl.BlockSpec((None, d, GATE_BLOCK), lambda i, j: (layer, 0, first + j * parts))

    gate_specs = [gate_spec(br, p) for br in range(N_BRANCH) for p in range(parts)]
    return pl.pallas_call(
        _merge_kernel,
        grid=(n // tm, d // tn),
        in_specs=[
            pl.BlockSpec((tm, d), lambda i, j: (i, 0)),
            pl.BlockSpec((1, d), lambda i, j: (0, 0)),
            y_spec, y_spec, y_spec, y_spec,
            *gate_specs,
            pl.BlockSpec((None, N_BRANCH, BRANCH_W, tn), lambda i, j: (layer, 0, 0, j)),
            pl.BlockSpec((None, tn, d), lambda i, j: (layer, j, 0)),
        ],
        out_specs=pl.BlockSpec((tm, d), lambda i, j: (i, 0)),
        out_shape=jax.ShapeDtypeStruct((n, d), F32),
        scratch_shapes=[pltpu.VMEM((tm, d), BF16)],
        compiler_params=_params("parallel", "arbitrary"),
        name="merge",
    )(x, g, *ys, *([w_in] * len(gate_specs)), w_branch, w_out)


def _alibi_slopes(n):
    return 2.0 ** (-8.0 * jnp.arange(1, n + 1, dtype=F32) / n)


def _mixer(h, bsz, seq, norm_g, w_in, conv_w, sinks, w_branch, w_out, layer):
    slopes = _alibi_slopes(2 * N_HEADS)
    qkv = _qkv_proj(h, norm_g, w_in, layer).reshape(bsz, seq, QKV_COLS)
    y_a = _moba_attention(qkv, slopes[N_HEADS:], bsz, seq)
    y_b = _sb_attention(qkv, bsz, seq)
    y_c = _swa_attention(qkv, slopes[:N_HEADS], sinks.astype(F32), bsz, seq)
    y_d = _gated_conv(qkv, conv_w.reshape(CONV_K, -1), bsz, seq)
    ys = [y.reshape(bsz * seq, BRANCH_W) for y in (y_a, y_b, y_c, y_d)]
    return _merge(h, norm_g, ys, w_in, w_branch, w_out, layer)


def kernel(x, ffn1_norm, ffn1_w1, ffn1_w3, ffn1_w2, mix_norm, w_in, conv_w, attn_sinks, w_branch, w_out,
           ffn2_norm, ffn2_w1, ffn2_w3, ffn2_w2, final_norm):
    bsz, seq, d = x.shape
    depth = w_in.shape[0]
    h = x.reshape(bsz * seq, d)
    row = lambda v: v.reshape(1, d)
    final_g = row(final_norm)
    ffn1 = [w.astype(BF16) for w in (ffn1_w1, ffn1_w3, ffn1_w2)]
    ffn2 = [w.astype(BF16) for w in (ffn2_w1, ffn2_w3, ffn2_w2)]
    w_in, w_branch, w_out = (w.astype(BF16) for w in (w_in, w_branch, w_out))
    for l in range(depth):
        h = _ffn(h, row(ffn1_norm[l]), *ffn1, final_g, layer=l, final_norm=False)
        h = _mixer(h, bsz, seq, row(mix_norm[l]), w_in, conv_w[l], attn_sinks[l], w_branch, w_out, l)
        h = _ffn(h, row(ffn2_norm[l]), *ffn2, final_g, layer=l, final_norm=(l == depth - 1))
    return h.reshape(bsz, seq, d)
```

```python
import functools

import jax
import jax.numpy as jnp
from jax import lax
from jax.experimental import pallas as pl
from jax.experimental.pallas import tpu as pltpu

F32 = jnp.float32
BF16 = jnp.bfloat16

EPS = 1e-6
HEAD_DIM = 64
LANES = 128
BF16_SUBLANES = 16
MXU_DEPTH = 256
ATTN_HEADS = 4
ATTN_LANES = ATTN_HEADS * HEAD_DIM
SB_HEADS = 8
SB_LANES = SB_HEADS * HEAD_DIM
MOBA_BLOCK = 256
MOBA_TOPK = 3
SWA_WINDOW = 128
CONV_K = 3
N_BRANCH = 4
BRANCH_W = 512
N_HEADS = 8
SWA_KV_HEADS = 2
SB_TILE = 256
SB_LOG_FLOOR = 110.0

COL_MOBA_Q, COL_MOBA_K, COL_MOBA_V = 0, 512, 1024
COL_SB_Q, COL_SB_K, COL_SB_V = 1536, 2048, 2560
COL_SWA_Q, COL_SWA_K, COL_SWA_V = 3072, 3584, 3712
COL_CONV_B, COL_CONV_C, COL_CONV_H = 3840, 4352, 4864
QKV_COLS = 5376

V7X_VMEM_BYTES = 64 * 1024 * 1024
VMEM_LIMIT = V7X_VMEM_BYTES - 8 * 1024 * 1024
NEG_BIG = -1e30

NORM_CHUNKS = 4
FFN_TOKEN_TILE = 1024
FF_TILE = 512
QKV_TOKEN_TILE = 1024
QKV_COL_TILE = 1792
MERGE_TOKEN_TILE = 512
MERGE_COL_TILE = 512
GATE_BLOCK = 512


def _params(*semantics):
    return pltpu.CompilerParams(dimension_semantics=semantics, vmem_limit_bytes=VMEM_LIMIT)


def _rms_norm(x, g):
    ms = jnp.mean(x * x, axis=-1, keepdims=True)
    return x * lax.rsqrt(ms + EPS) * g


def _dot(a, b):
    return jnp.dot(a, b, preferred_element_type=F32)


def _row_chunks(rows):
    chunk = rows // NORM_CHUNKS if rows % NORM_CHUNKS == 0 else rows
    return [pl.ds(start, chunk) for start in range(0, rows, chunk)]


def _split_bf16(x):
    hi = x.astype(BF16)
    lo = (x - hi.astype(F32)).astype(BF16)
    return hi, lo


def _ffn_kernel(x_ref, g_ref, w1_ref, w3_ref, w2_ref, fg_ref, o_ref, xn_ref, *, final_norm):
    f = pl.program_id(1)

    def half_swiglu(xn):
        a = _dot(xn, w1_ref[...])
        b = _dot(xn, w3_ref[...])
        hidden = (a * jax.nn.sigmoid(a) * b).astype(BF16)
        return 0.5 * _dot(hidden, w2_ref[...])

    @pl.when(f == 0)
    def _():
        for rows in _row_chunks(x_ref.shape[0]):
            x = x_ref[rows, :]
            xn = _rms_norm(x, g_ref[...]).astype(BF16)
            xn_ref[rows, :] = xn
            o_ref[rows, :] = x + half_swiglu(xn)

    @pl.when(f > 0)
    def _():
        o_ref[...] += half_swiglu(xn_ref[...])

    if final_norm:
        @pl.when(f == pl.num_programs(1) - 1)
        def _():
            o_ref[...] = _rms_norm(o_ref[...], fg_ref[...])


def _ffn(x, g, w1, w3, w2, final_g, *, layer, final_norm):
    n, d = x.shape
    d_ff = w1.shape[2]
    tm, tf = min(FFN_TOKEN_TILE, n), FF_TILE
    return pl.pallas_call(
        functools.partial(_ffn_kernel, final_norm=final_norm),
        grid=(n // tm, d_ff // tf),
        in_specs=[
            pl.BlockSpec((tm, d), lambda i, f: (i, 0)),
            pl.BlockSpec((1, d), lambda i, f: (0, 0)),
            pl.BlockSpec((None, d, tf), lambda i, f: (layer, 0, f)),
            pl.BlockSpec((None, d, tf), lambda i, f: (layer, 0, f)),
            pl.BlockSpec((None, tf, d), lambda i, f: (layer, f, 0)),
            pl.BlockSpec((1, d), lambda i, f: (0, 0)),
        ],
        out_specs=pl.BlockSpec((tm, d), lambda i, f: (i, 0)),
        out_shape=jax.ShapeDtypeStruct((n, d), F32),
        scratch_shapes=[pltpu.VMEM((tm, d), BF16)],
        compiler_params=_params("parallel", "arbitrary"),
        name="ffn",
    )(x, g, w1, w3, w2, final_g)


def _qkv_kernel(x_ref, g_ref, w_ref, o_ref, xn_ref):
    j = pl.program_id(1)

    @pl.when(j == 0)
    def _():
        for rows in _row_chunks(x_ref.shape[0]):
            xn = _rms_norm(x_ref[rows, :], g_ref[...]).astype(BF16)
            xn_ref[rows, :] = xn
            o_ref[rows, :] = _dot(xn, w_ref[...]).astype(BF16)

    @pl.when(j > 0)
    def _():
        o_ref[...] = _dot(xn_ref[...], w_ref[...]).astype(BF16)


def _qkv_proj(x, g, w_in, layer):
    n, d = x.shape
    tm, tn = min(QKV_TOKEN_TILE, n), QKV_COL_TILE
    assert QKV_COLS % tn == 0
    return pl.pallas_call(
        _qkv_kernel,
        grid=(n // tm, QKV_COLS // tn),
        in_specs=[
            pl.BlockSpec((tm, d), lambda i, j: (i, 0)),
            pl.BlockSpec((1, d), lambda i, j: (0, 0)),
            pl.BlockSpec((None, d, tn), lambda i, j: (layer, 0, j)),
        ],
        out_specs=pl.BlockSpec((tm, tn), lambda i, j: (i, j)),
        out_shape=jax.ShapeDtypeStruct((n, QKV_COLS), BF16),
        scratch_shapes=[pltpu.VMEM((tm, d), BF16)],
        compiler_params=_params("parallel", "arbitrary"),
        name="qkv_proj",
    )(x, g, w_in)


def _store_transposed_values(v_ref, vt_ref, tile):
    for n in range(vt_ref.shape[0]):
        vt_ref[n] = v_ref[n * tile:(n + 1) * tile, :].astype(F32).T.astype(BF16)


def _head_rows(vt, head_in_block):
    return vt[head_in_block * HEAD_DIM:(head_in_block + 1) * HEAD_DIM, :]


def _transposed_heads(q, scale):
    qt = q.astype(F32).T * scale
    out = []
    for hh in range(q.shape[1] // HEAD_DIM):
        lo = (hh * HEAD_DIM // MXU_DEPTH) * MXU_DEPTH
        lanes = slice(lo, min(lo + MXU_DEPTH, qt.shape[0]))
        slab = qt[lanes]
        row = lo + lax.broadcasted_iota(jnp.int32, slab.shape, 0)
        out.append((jnp.where((row // HEAD_DIM) == hh, slab, 0.0).astype(BF16), lanes))
    return out


def _head_dot(x, q_head):
    slab, lanes = q_head
    return _dot(x[:, lanes], slab)


def _store_heads(o_ref, outs_t):
    o_ref[...] = jnp.concatenate(outs_t, axis=0).T.astype(o_ref.dtype)


def _softplus(z):
    return jnp.maximum(z, 0.0) + jnp.log(1.0 + jnp.exp(-jnp.abs(z)))


def _sb_tiles(qh_t, tiles, runs, later_ones, causal):
    chains = [(ti, hh) for ti in range(len(tiles)) for hh in range(len(qh_t))]
    z = {c: _head_dot(tiles[c[0]][0], qh_t[c[1]]) for c in chains}
    runs = list(runs)
    sp, parts, later = {}, {}, {}
    for c in chains:
        ti, hh = c
        sp[c] = _softplus(z[c])
        spm = jnp.where(causal, sp[c], 0.0) if tiles[ti][2] else sp[c]
        parts[c] = _split_bf16(spm)
        later[c] = runs[hh]
        runs[hh] = runs[hh] + jnp.sum(spm, axis=0, keepdims=True)
    after = {c: _dot(later_ones, parts[c][0]) + _dot(later_ones, parts[c][1]) + later[c] for c in chains}
    weights = {}
    for c in chains:
        a = jnp.exp(z[c] - sp[c] - after[c])
        if tiles[c[0]][2]:
            a = jnp.where(causal, a, 0.0)
        weights[c] = a.astype(BF16)
    return {c: _dot(_head_rows(tiles[c[0]][1], c[1]), weights[c]) for c in chains}, runs


def _sb_kernel(q_ref, k_ref, v_ref, o_ref, vt_ref, acc_ref):
    t = SB_TILE
    i = pl.program_id(2)

    @pl.when(i == 0)
    def _():
        _store_transposed_values(v_ref, vt_ref, t)

    key = lax.broadcasted_iota(jnp.int32, (t, t), 0)
    qry = lax.broadcasted_iota(jnp.int32, (t, t), 1)
    later_ones = (qry > key).astype(BF16)
    causal = key < qry
    qh_t = _transposed_heads(q_ref[...], HEAD_DIM ** -0.5)
    heads = range(len(qh_t))
    prev = jnp.maximum(i - 1, 0)
    has_prev = jnp.where(i > 0, 1.0, 0.0)
    k_diag, vt_diag = k_ref[pl.ds(pl.multiple_of(i * t, t), t), :], vt_ref[i]
    k_prev, vt_prev = k_ref[pl.ds(pl.multiple_of(prev * t, t), t), :], vt_ref[prev]
    zeros = jnp.zeros((1, t), F32)
    pv, runs = _sb_tiles(qh_t, [(k_diag, vt_diag, True), (k_prev, vt_prev, False)], [zeros for _ in heads],
                         later_ones, causal)
    for hh in heads:
        acc_ref[hh] = pv[0, hh] + has_prev * pv[1, hh]

    def still_visible(runs):
        return jnp.min(functools.reduce(jnp.minimum, runs)) < SB_LOG_FLOOR

    def cond(carry):
        j, _, go = carry
        return jnp.logical_and(j <= i, go)

    def body(carry):
        j, runs, _ = carry
        blk = i - j
        k, vt = k_ref[pl.ds(pl.multiple_of(blk * t, t), t), :], vt_ref[blk]
        pv, new_runs = _sb_tiles(qh_t, [(k, vt, False)], runs, later_ones, causal)
        for hh in heads:
            acc_ref[hh] += pv[0, hh]
        return j + 1, tuple(new_runs), still_visible(new_runs)

    lax.while_loop(cond, body, (jnp.int32(2), tuple(runs), still_visible(runs)))
    _store_heads(o_ref, [acc_ref[hh] for hh in heads])


def _sb_attention(qkv, bsz, seq):
    t = SB_TILE
    cb = lambda off: off // SB_LANES
    return pl.pallas_call(
        _sb_kernel,
        grid=(bsz, N_HEADS // SB_HEADS, seq // t),
        in_specs=[
            pl.BlockSpec((None, t, SB_LANES), lambda b, hg, i: (b, i, cb(COL_SB_Q) + hg)),
            pl.BlockSpec((None, seq, SB_LANES), lambda b, hg, i: (b, 0, cb(COL_SB_K) + hg)),
            pl.BlockSpec((None, seq, SB_LANES), lambda b, hg, i: (b, 0, cb(COL_SB_V) + hg)),
        ],
        out_specs=pl.BlockSpec((None, t, SB_LANES), lambda b, hg, i: (b, i, hg)),
        out_shape=jax.ShapeDtypeStruct((bsz, seq, BRANCH_W), BF16),
        scratch_shapes=[pltpu.VMEM((seq // t, SB_LANES, t), BF16),
                        pltpu.VMEM((SB_HEADS, HEAD_DIM, t), F32)],
        compiler_params=_params("parallel", "parallel", "arbitrary"),
        name="stick_breaking",
    )(qkv, qkv, qkv)


def _moba_kernel(slopes_ref, q_ref, k_ref, v_ref, o_ref, kmean_ref, vt_ref, bias_ref, chosen_ref, acc_ref,
                 s_ref, p_ref, *, n_blocks):
    t = MOBA_BLOCK
    head_group = pl.program_id(1)
    i = pl.program_id(2)
    nb_pad = kmean_ref.shape[0]
    vt_rows = acc_ref.shape[1]

    heads = range(ATTN_HEADS)
    slopes = [slopes_ref[head_group * ATTN_HEADS + hh] for hh in heads]
    key = lax.broadcasted_iota(jnp.int32, (t, t), 0)
    qry = lax.broadcasted_iota(jnp.int32, (t, t), 1)

    @pl.when(i == 0)
    def _():
        for n in range(n_blocks):
            vt = v_ref[n * t:(n + 1) * t, :].astype(F32).T
            for hh in heads:
                vt_ref[n, hh * vt_rows:hh * vt_rows + HEAD_DIM] = _head_rows(vt, hh).astype(BF16)
                vt_ref[n, hh * vt_rows + HEAD_DIM:(hh + 1) * vt_rows] = jnp.ones((vt_rows - HEAD_DIM, t), BF16)
        kmean_ref[...] = jnp.zeros_like(kmean_ref)
        for n in range(n_blocks):
            kb = k_ref[n * t:(n + 1) * t, :].astype(F32)
            kmean_ref[n:n + 1, :] = jnp.mean(kb, axis=0, keepdims=True)
        dist0 = (qry - key).astype(F32)
        for hh in heads:
            bias_ref[hh] = slopes[hh] * dist0

    blk_id = lax.broadcasted_iota(jnp.int32, (nb_pad, t), 0)
    past = blk_id < i
    q_raw = q_ref[...]
    qh_t = _transposed_heads(q_raw, HEAD_DIM ** -0.5)
    qg_t = _transposed_heads(q_raw, 1.0)
    km_hi, km_lo = _split_bf16(kmean_ref[...])

    def store_logits(slot, tile, which=heads):
        k = k_ref[pl.ds(pl.multiple_of(tile * t, t), t), :]
        for hh in which:
            s_ref[slot, hh] = _head_dot(k, qh_t[hh])

    def accumulate(slot, tile, alpha, which=heads):
        vt = vt_ref[tile]
        for hh in which:
            acc_ref[hh] = alpha[hh] * acc_ref[hh] + _dot(vt[hh * vt_rows:(hh + 1) * vt_rows, :], p_ref[slot, hh])

    store_logits(0, 0)
    k_own = k_ref[pl.ds(pl.multiple_of(i * t, t), t), :]
    s_own = [_head_dot(k_own, qh_t[hh]) for hh in heads]
    gates = [_head_dot(km_hi, qg_t[hh]) + _head_dot(km_lo, qg_t[hh]) for hh in heads]
    m = []
    for hh in heads:
        acc_ref[hh] = jnp.zeros((vt_rows, t), F32)
        gate = jnp.where(past, gates[hh], -jnp.inf)
        rank = jnp.zeros((nb_pad, t), jnp.int32)
        for blk in range(n_blocks):
            gb = gate[blk:blk + 1, :]
            ahead = (gb > gate) | ((gb == gate) & (blk_id > blk))
            rank += ahead.astype(jnp.int32)
        chosen_ref[hh] = jnp.where(past & (rank < MOBA_TOPK), 1.0, 0.0)

        s = jnp.where(key <= qry, s_own[hh] - bias_ref[hh], NEG_BIG)
        m0 = jnp.max(s, axis=0, keepdims=True)
        p_ref[1, hh] = jnp.exp(s - m0).astype(BF16)
        m.append(m0)
    ones = jnp.ones((1, t), F32)

    def step(slot, n, carry):
        alpha_prev, m_old, prev_tile = carry
        offset = ((i - n) * t).astype(F32)
        alpha, m_new = [], []
        for group in (heads[:2], heads[2:]):
            accumulate(1 - slot, prev_tile, alpha_prev, group)
            store_logits(1 - slot, jnp.minimum(n + 1, i), group)
            for hh in group:
                picked = chosen_ref[hh, pl.ds(n, 1), :]
                shift = jnp.where(picked > 0.0, -slopes[hh] * offset, NEG_BIG)
                s = s_ref[slot, hh] - bias_ref[hh]
                m_new.append(jnp.maximum(m_old[hh], jnp.max(s, axis=0, keepdims=True) + shift))
                alpha.append(jnp.exp(m_old[hh] - m_new[hh]))
                p_ref[slot, hh] = jnp.exp(s - (m_new[hh] - shift)).astype(BF16)
        return tuple(alpha), tuple(m_new), n

    def pair(j, carry):
        return step(1, 2 * j + 1, step(0, 2 * j, carry))

    init = (tuple(ones for _ in heads), tuple(m), i)
    alpha_last, _, last_tile = lax.fori_loop(0, (i + 1) // 2, pair, init)
    accumulate(1, last_tile, alpha_last)
    _store_heads(o_ref, [acc_ref[hh, :HEAD_DIM] / acc_ref[hh, HEAD_DIM:HEAD_DIM + 1] for hh in heads])


def _moba_attention(qkv, slopes, bsz, seq):
    t = MOBA_BLOCK
    n_blocks = seq // t
    nb_pad = -(-n_blocks // BF16_SUBLANES) * BF16_SUBLANES
    vt_rows = HEAD_DIM + BF16_SUBLANES
    assert seq % t == 0
    cb = lambda off: off // ATTN_LANES
    return pl.pallas_call(
        functools.partial(_moba_kernel, n_blocks=n_blocks),
        grid=(bsz, N_HEADS // ATTN_HEADS, n_blocks),
        in_specs=[
            pl.BlockSpec(memory_space=pltpu.SMEM),
            pl.BlockSpec((None, t, ATTN_LANES), lambda b, hg, i: (b, i, cb(COL_MOBA_Q) + hg)),
            pl.BlockSpec((None, seq, ATTN_LANES), lambda b, hg, i: (b, 0, cb(COL_MOBA_K) + hg)),
            pl.BlockSpec((None, seq, ATTN_LANES), lambda b, hg, i: (b, 0, cb(COL_MOBA_V) + hg)),
        ],
        out_specs=pl.BlockSpec((None, t, ATTN_LANES), lambda b, hg, i: (b, i, hg)),
        out_shape=jax.ShapeDtypeStruct((bsz, seq, BRANCH_W), BF16),
        scratch_shapes=[pltpu.VMEM((nb_pad, ATTN_LANES), F32),
                        pltpu.VMEM((n_blocks, ATTN_HEADS * vt_rows, t), BF16),
                        pltpu.VMEM((ATTN_HEADS, t, t), F32),
                        pltpu.VMEM((ATTN_HEADS, nb_pad, t), F32),
                        pltpu.VMEM((ATTN_HEADS, vt_rows, t), F32),
                        pltpu.VMEM((2, ATTN_HEADS, t, t), F32),
                        pltpu.VMEM((2, ATTN_HEADS, t, t), BF16)],
        compiler_params=_params("parallel", "parallel", "arbitrary"),
        name="moba",
    )(slopes, qkv, qkv, qkv)


def _swa_kernel(slope_ref, sink_ref, q_ref, k_ref, v_ref, o_ref, vt_ref, *, q_blocks):
    w = SWA_WINDOW
    tile = pl.program_id(1)
    group = N_HEADS // SWA_KV_HEADS
    gw = group * w

    @pl.when(tile == 0)
    def _():
        _store_transposed_values(v_ref, vt_ref, w)

    key = lax.broadcasted_iota(jnp.int32, (w, gw), 0)
    qry = lax.broadcasted_iota(jnp.int32, (w, gw), 1) % w
    dist = (qry - key).astype(F32)
    ok_cur = key <= qry
    in_window = key > qry
    no_dims = jnp.zeros((HEAD_DIM, w), F32)
    units = [(qb, kv) for qb in range(q_blocks) for kv in range(SWA_KV_HEADS)]

    blk, s_cur, s_prev = {}, {}, {}
    for qb in range(q_blocks):
        blk[qb] = tile * q_blocks + qb
        prev = jnp.maximum(blk[qb] - 1, 0)
        k_cur = k_ref[pl.ds(pl.multiple_of(blk[qb] * w, w), w), :]
        k_prev = k_ref[pl.ds(pl.multiple_of(prev * w, w), w), :]
        qt = q_ref[qb * w:(qb + 1) * w, :].astype(F32).T * (HEAD_DIM ** -0.5)
        for kv in range(SWA_KV_HEADS):
            cols = []
            for j in range(group):
                h = kv * group + j
                qh = qt[h * HEAD_DIM:(h + 1) * HEAD_DIM, :]
                cols.append(jnp.concatenate([qh, no_dims] if kv == 0 else [no_dims, qh], axis=0))
            qg = jnp.concatenate(cols, axis=1).astype(BF16)
            s_cur[qb, kv] = _dot(k_cur, qg)
            s_prev[qb, kv] = _dot(k_prev, qg)

    e_cur, e_prev, denom = {}, {}, {}
    for u in units:
        qb, kv = u
        slope, sink = slope_ref[kv], sink_ref[kv]
        bias = slope * dist
        sc = jnp.where(ok_cur, s_cur[u] - bias, NEG_BIG)
        sp = jnp.where(in_window & (blk[qb] > 0), s_prev[u] - bias - slope * float(w), NEG_BIG)
        m = jnp.maximum(jnp.maximum(jnp.max(sc, axis=0, keepdims=True), jnp.max(sp, axis=0, keepdims=True)), sink)
        ec, ep = jnp.exp(sc - m), jnp.exp(sp - m)
        denom[u] = jnp.sum(ec, axis=0, keepdims=True) + jnp.sum(ep, axis=0, keepdims=True) + jnp.exp(sink - m)
        e_cur[u], e_prev[u] = ec.astype(BF16), ep.astype(BF16)

    out_t = {}
    for u in units:
        qb, kv = u
        vt_cur = _head_rows(vt_ref[blk[qb]], kv)
        vt_prev = _head_rows(vt_ref[jnp.maximum(blk[qb] - 1, 0)], kv)
        out_t[u] = (_dot(vt_cur, e_cur[u]) + _dot(vt_prev, e_prev[u])) / denom[u]
    for qb in range(q_blocks):
        heads_t = [out_t[qb, kv][:, j * w:(j + 1) * w] for kv in range(SWA_KV_HEADS) for j in range(group)]
        o_ref[qb * w:(qb + 1) * w, :] = jnp.concatenate(heads_t, axis=0).T.astype(o_ref.dtype)


def _swa_attention(qkv, slopes, sinks, bsz, seq):
    w = SWA_WINDOW
    q_blocks = 2
    tq = q_blocks * w
    qw = N_HEADS * HEAD_DIM
    assert SWA_KV_HEADS * HEAD_DIM == LANES
    per_col = lambda v: jnp.repeat(v.reshape(SWA_KV_HEADS, N_HEADS // SWA_KV_HEADS), w, axis=1)[:, None, :]
    const_spec = pl.BlockSpec((SWA_KV_HEADS, 1, (N_HEADS // SWA_KV_HEADS) * w), lambda b, i: (0, 0, 0))
    return pl.pallas_call(
        functools.partial(_swa_kernel, q_blocks=q_blocks),
        grid=(bsz, seq // tq),
        in_specs=[
            const_spec,
            const_spec,
            pl.BlockSpec((None, tq, qw), lambda b, i: (b, i, COL_SWA_Q // qw)),
            pl.BlockSpec((None, seq, LANES), lambda b, i: (b, 0, COL_SWA_K // LANES)),
            pl.BlockSpec((None, seq, LANES), lambda b, i: (b, 0, COL_SWA_V // LANES)),
        ],
        out_specs=pl.BlockSpec((None, tq, qw), lambda b, i: (b, i, 0)),
        out_shape=jax.ShapeDtypeStruct((bsz, seq, BRANCH_W), BF16),
        scratch_shapes=[pltpu.VMEM((seq // w, LANES, w), BF16)],
        compiler_params=_params("parallel", "arbitrary"),
        name="swa",
    )(per_col(slopes), per_col(sinks), qkv, qkv, qkv)


def _conv_kernel(w_ref, b_ref, c_ref, h_ref, o_ref):
    u = c_ref[...].astype(F32) * h_ref[...].astype(F32)
    pos = lax.broadcasted_iota(jnp.int32, u.shape, 0)
    y = w_ref[CONV_K - 1:CONV_K, :] * u
    for back in range(1, CONV_K):
        shifted = jnp.where(pos >= back, pltpu.roll(u, back, 0), 0.0)
        y += w_ref[CONV_K - 1 - back:CONV_K - back, :] * shifted
    o_ref[...] = (b_ref[...].astype(F32) * y).astype(o_ref.dtype)


def _gated_conv(qkv, conv_w, bsz, seq):
    ch = conv_w.shape[1]
    cb = lambda off: off // LANES
    return pl.pallas_call(
        _conv_kernel,
        grid=(bsz, ch // LANES),
        in_specs=[
            pl.BlockSpec((CONV_K, LANES), lambda b, c: (0, c)),
            pl.BlockSpec((None, seq, LANES), lambda b, c: (b, 0, cb(COL_CONV_B) + c)),
            pl.BlockSpec((None, seq, LANES), lambda b, c: (b, 0, cb(COL_CONV_C) + c)),
            pl.BlockSpec((None, seq, LANES), lambda b, c: (b, 0, cb(COL_CONV_H) + c)),
        ],
        out_specs=pl.BlockSpec((None, seq, LANES), lambda b, c: (b, 0, c)),
        out_shape=jax.ShapeDtypeStruct((bsz, seq, ch), BF16),
        compiler_params=_params("parallel", "parallel"),
        name="gated_conv",
    )(conv_w, qkv, qkv, qkv)


def _merge_kernel(x_ref, g_ref, *refs):
    y_refs = refs[:N_BRANCH]
    wb_ref, wo_ref, o_ref, xn_ref = refs[-4:]
    wg_refs = refs[N_BRANCH:-4]
    parts = len(wg_refs) // N_BRANCH
    j = pl.program_id(1)

    def mixed(xn, rows):
        out = None
        ys = [y_ref[rows, :] for y_ref in y_refs]
        for p in range(parts):
            cols = slice(p * GATE_BLOCK, (p + 1) * GATE_BLOCK)
            merged = None
            for n in range(N_BRANCH):
                gate = jax.nn.sigmoid(_dot(xn, wg_refs[n * parts + p][0]))
                term = gate * _dot(ys[n], wb_ref[n, :, cols])
                merged = term if merged is None else merged + term
            part = _dot(merged.astype(BF16), wo_ref[cols, :])
            out = part if out is None else out + part
        return out

    @pl.when(j == 0)
    def _():
        for rows in _row_chunks(x_ref.shape[0]):
            x = x_ref[rows, :]
            xn = _rms_norm(x, g_ref[...]).astype(BF16)
            xn_ref[rows, :] = xn
            o_ref[rows, :] = x + mixed(xn, rows)

    @pl.when(j > 0)
    def _():
        o_ref[...] += mixed(xn_ref[...], slice(None))


def _merge(x, g, ys, w_in, w_branch, w_out, layer):
    n, d = x.shape
    tm, tn = min(MERGE_TOKEN_TILE, n), MERGE_COL_TILE
    parts = tn // GATE_BLOCK
    assert QKV_COLS % LANES == 0 and tn % GATE_BLOCK == 0
    y_spec = pl.BlockSpec((tm, BRANCH_W), lambda i, j: (i, 0))

    def gate_spec(branch, part):
        first = QKV_COLS + branch * d + part * GATE_BLOCK
        return pl.BlockSpec((pl.Element(1), pl.Element(d), pl.Element(GATE_BLOCK)),
                            lambda i, j: (layer, 0, pl.multiple_of(first + j * tn, LANES)))

    gate_specs = [gate_spec(br, p) for br in range(N_BRANCH) for p in range(parts)]
    return pl.pallas_call(
        _merge_kernel,
        grid=(n // tm, d // tn),
        in_specs=[
            pl.BlockSpec((tm, d), lambda i, j: (i, 0)),
            pl.BlockSpec((1, d), lambda i, j: (0, 0)),
            y_spec, y_spec, y_spec, y_spec,
            *gate_specs,
            pl.BlockSpec((None, N_BRANCH, BRANCH_W, tn), lambda i, j: (layer, 0, 0, j)),
            pl.BlockSpec((None, tn, d), lambda i, j: (layer, j, 0)),
        ],
        out_specs=pl.BlockSpec((tm, d), lambda i, j: (i, 0)),
        out_shape=jax.ShapeDtypeStruct((n, d), F32),
        scratch_shapes=[pltpu.VMEM((tm, d), BF16)],
        compiler_params=_params("parallel", "arbitrary"),
        name="merge",
    )(x, g, *ys, *([w_in] * len(gate_specs)), w_branch, w_out)


def _alibi_slopes(n):
    return 2.0 ** (-8.0 * jnp.arange(1, n + 1, dtype=F32) / n)


def _mixer(h, bsz, seq, norm_g, w_in, conv_w, sinks, w_branch, w_out, layer):
    slopes = _alibi_slopes(2 * N_HEADS)
    qkv = _qkv_proj(h, norm_g, w_in, layer).reshape(bsz, seq, QKV_COLS)
    y_a = _moba_attention(qkv, slopes[N_HEADS:], bsz, seq)
    y_b = _sb_attention(qkv, bsz, seq)
    y_c = _swa_attention(qkv, slopes[:N_HEADS], sinks.astype(F32), bsz, seq)
    y_d = _gated_conv(qkv, conv_w.reshape(CONV_K, -1), bsz, seq)
    ys = [y.reshape(bsz * seq, BRANCH_W) for y in (y_a, y_b, y_c, y_d)]
    return _merge(h, norm_g, ys, w_in, w_branch, w_out, layer)


def kernel(x, ffn1_norm, ffn1_w1, ffn1_w3, ffn1_w2, mix_norm, w_in, conv_w, attn_sinks, w_branch, w_out,
           ffn2_norm, ffn2_w1, ffn2_w3, ffn2_w2, final_norm):
    bsz, seq, d = x.shape
    depth = w_in.shape[0]
    h = x.reshape(bsz * seq, d)
    row = lambda v: v.reshape(1, d)
    final_g = row(final_norm)
    ffn1 = [w.astype(BF16) for w in (ffn1_w1, ffn1_w3, ffn1_w2)]
    ffn2 = [w.astype(BF16) for w in (ffn2_w1, ffn2_w3, ffn2_w2)]
    w_in, w_branch, w_out = (w.astype(BF16) for w in (w_in, w_branch, w_out))
    for l in range(depth):
        h = _ffn(h, row(ffn1_norm[l]), *ffn1, final_g, layer=l, final_norm=False)
        h = _mixer(h, bsz, seq, row(mix_norm[l]), w_in, conv_w[l], attn_sinks[l], w_branch, w_out, l)
        h = _ffn(h, row(ffn2_norm[l]), *ffn2, final_g, layer=l, final_norm=(l == depth - 1))
    return h.reshape(bsz, seq, d)
```

```python
import functools

import jax
import jax.numpy as jnp
from jax import lax
from jax.experimental import pallas as pl
from jax.experimental.pallas import tpu as pltpu

F32 = jnp.float32
BF16 = jnp.bfloat16

EPS = 1e-6
HEAD_DIM = 64
LANES = 128
BF16_SUBLANES = 16
MXU_DEPTH = 256
ATTN_HEADS = 4
ATTN_LANES = ATTN_HEADS * HEAD_DIM
SB_HEADS = 8
SB_LANES = SB_HEADS * HEAD_DIM
MOBA_BLOCK = 256
MOBA_TOPK = 3
SWA_WINDOW = 128
CONV_K = 3
N_BRANCH = 4
BRANCH_W = 512
N_HEADS = 8
SWA_KV_HEADS = 2
SB_TILE = 256
SB_LOG_FLOOR = 110.0

COL_MOBA_Q, COL_MOBA_K, COL_MOBA_V = 0, 512, 1024
COL_SB_Q, COL_SB_K, COL_SB_V = 1536, 2048, 2560
COL_SWA_Q, COL_SWA_K, COL_SWA_V = 3072, 3584, 3712
COL_CONV_B, COL_CONV_C, COL_CONV_H = 3840, 4352, 4864
QKV_COLS = 5376

V7X_VMEM_BYTES = 64 * 1024 * 1024
VMEM_LIMIT = V7X_VMEM_BYTES - 8 * 1024 * 1024
NEG_BIG = -1e30

NORM_CHUNKS = 4
FFN_TOKEN_TILE = 1024
FF_TILE = 512
QKV_TOKEN_TILE = 1024
QKV_COL_TILE = 1792
MERGE_TOKEN_TILE = 512
MERGE_COL_TILE = 512
GATE_BLOCK = 512


def _params(*semantics):
    return pltpu.CompilerParams(dimension_semantics=semantics, vmem_limit_bytes=VMEM_LIMIT)


def _rms_norm(x, g):
    ms = jnp.mean(x * x, axis=-1, keepdims=True)
    return x * lax.rsqrt(ms + EPS) * g


def _dot(a, b):
    return jnp.dot(a, b, preferred_element_type=F32)


def _row_chunks(rows):
    chunk = rows // NORM_CHUNKS if rows % NORM_CHUNKS == 0 else rows
    return [pl.ds(start, chunk) for start in range(0, rows, chunk)]


def _split_bf16(x):
    hi = x.astype(BF16)
    lo = (x - hi.astype(F32)).astype(BF16)
    return hi, lo


def _ffn_kernel(x_ref, g_ref, w1_ref, w3_ref, w2_ref, fg_ref, o_ref, xn_ref, *, final_norm):
    f = pl.program_id(1)

    def half_swiglu(xn):
        a = _dot(xn, w1_ref[...])
        b = _dot(xn, w3_ref[...])
        hidden = (a * jax.nn.sigmoid(a) * b).astype(BF16)
        return 0.5 * _dot(hidden, w2_ref[...])

    @pl.when(f == 0)
    def _():
        x = x_ref[...]
        xn_ref[...] = _rms_norm(x, g_ref[...]).astype(BF16)
        o_ref[...] = x

    o_ref[...] += half_swiglu(xn_ref[...])

    if final_norm:
        @pl.when(f == pl.num_programs(1) - 1)
        def _():
            o_ref[...] = _rms_norm(o_ref[...], fg_ref[...])


def _ffn(x, g, w1, w3, w2, final_g, *, layer, final_norm):
    n, d = x.shape
    d_ff = w1.shape[2]
    tm, tf = min(FFN_TOKEN_TILE, n), FF_TILE
    return pl.pallas_call(
        functools.partial(_ffn_kernel, final_norm=final_norm),
        grid=(n // tm, d_ff // tf),
        in_specs=[
            pl.BlockSpec((tm, d), lambda i, f: (i, 0)),
            pl.BlockSpec((1, d), lambda i, f: (0, 0)),
            pl.BlockSpec((None, d, tf), lambda i, f: (layer, 0, f)),
            pl.BlockSpec((None, d, tf), lambda i, f: (layer, 0, f)),
            pl.BlockSpec((None, tf, d), lambda i, f: (layer, f, 0)),
            pl.BlockSpec((1, d), lambda i, f: (0, 0)),
        ],
        out_specs=pl.BlockSpec((tm, d), lambda i, f: (i, 0)),
        out_shape=jax.ShapeDtypeStruct((n, d), F32),
        scratch_shapes=[pltpu.VMEM((tm, d), BF16)],
        compiler_params=_params("parallel", "arbitrary"),
        name="ffn",
    )(x, g, w1, w3, w2, final_g)


def _qkv_kernel(x_ref, g_ref, w_ref, o_ref, xn_ref):
    j = pl.program_id(1)

    @pl.when(j == 0)
    def _():
        for rows in _row_chunks(x_ref.shape[0]):
            xn = _rms_norm(x_ref[rows, :], g_ref[...]).astype(BF16)
            xn_ref[rows, :] = xn
            o_ref[rows, :] = _dot(xn, w_ref[...]).astype(BF16)

    @pl.when(j > 0)
    def _():
        o_ref[...] = _dot(xn_ref[...], w_ref[...]).astype(BF16)


def _qkv_proj(x, g, w_in, layer):
    n, d = x.shape
    tm, tn = min(QKV_TOKEN_TILE, n), QKV_COL_TILE
    assert QKV_COLS % tn == 0
    return pl.pallas_call(
        _qkv_kernel,
        grid=(n // tm, QKV_COLS // tn),
        in_specs=[
            pl.BlockSpec((tm, d), lambda i, j: (i, 0)),
            pl.BlockSpec((1, d), lambda i, j: (0, 0)),
            pl.BlockSpec((None, d, tn), lambda i, j: (layer, 0, j)),
        ],
        out_specs=pl.BlockSpec((tm, tn), lambda i, j: (i, j)),
        out_shape=jax.ShapeDtypeStruct((n, QKV_COLS), BF16),
        scratch_shapes=[pltpu.VMEM((tm, d), BF16)],
        compiler_params=_params("parallel", "arbitrary"),
        name="qkv_proj",
    )(x, g, w_in)


def _store_transposed_values(v_ref, vt_ref, tile):
    for n in range(vt_ref.shape[0]):
        vt_ref[n] = v_ref[n * tile:(n + 1) * tile, :].astype(F32).T.astype(BF16)


def _head_rows(vt, head_in_block):
    return vt[head_in_block * HEAD_DIM:(head_in_block + 1) * HEAD_DIM, :]


def _transposed_heads(q, scale):
    qt = q.astype(F32).T * scale
    out = []
    for hh in range(q.shape[1] // HEAD_DIM):
        lo = (hh * HEAD_DIM // MXU_DEPTH) * MXU_DEPTH
        lanes = slice(lo, min(lo + MXU_DEPTH, qt.shape[0]))
        slab = qt[lanes]
        row = lo + lax.broadcasted_iota(jnp.int32, slab.shape, 0)
        out.append((jnp.where((row // HEAD_DIM) == hh, slab, 0.0).astype(BF16), lanes))
    return out


def _head_dot(x, q_head):
    slab, lanes = q_head
    return _dot(x[:, lanes], slab)


def _store_heads(o_ref, outs_t):
    o_ref[...] = jnp.concatenate(outs_t, axis=0).T.astype(o_ref.dtype)


def _softplus(z):
    return jnp.maximum(z, 0.0) + jnp.log(1.0 + jnp.exp(-jnp.abs(z)))


def _sb_tiles(qh_t, tiles, runs, later_ones, causal):
    chains = [(ti, hh) for ti in range(len(tiles)) for hh in range(len(qh_t))]
    z = {c: _head_dot(tiles[c[0]][0], qh_t[c[1]]) for c in chains}
    runs = list(runs)
    sp, parts, later = {}, {}, {}
    for c in chains:
        ti, hh = c
        sp[c] = _softplus(z[c])
        spm = jnp.where(causal, sp[c], 0.0) if tiles[ti][2] else sp[c]
        parts[c] = _split_bf16(spm)
        later[c] = runs[hh]
        runs[hh] = runs[hh] + jnp.sum(spm, axis=0, keepdims=True)
    after = {c: _dot(later_ones, parts[c][0]) + _dot(later_ones, parts[c][1]) + later[c] for c in chains}
    weights = {}
    for c in chains:
        a = jnp.exp(z[c] - sp[c] - after[c])
        if tiles[c[0]][2]:
            a = jnp.where(causal, a, 0.0)
        weights[c] = a.astype(BF16)
    return {c: _dot(_head_rows(tiles[c[0]][1], c[1]), weights[c]) for c in chains}, runs


def _sb_kernel(q_ref, k_ref, v_ref, o_ref, vt_ref, acc_ref):
    t = SB_TILE
    i = pl.program_id(2)

    @pl.when(i == 0)
    def _():
        _store_transposed_values(v_ref, vt_ref, t)

    key = lax.broadcasted_iota(jnp.int32, (t, t), 0)
    qry = lax.broadcasted_iota(jnp.int32, (t, t), 1)
    later_ones = (qry > key).astype(BF16)
    causal = key < qry
    qh_t = _transposed_heads(q_ref[...], HEAD_DIM ** -0.5)
    heads = range(len(qh_t))
    prev = jnp.maximum(i - 1, 0)
    has_prev = jnp.where(i > 0, 1.0, 0.0)
    k_diag, vt_diag = k_ref[pl.ds(pl.multiple_of(i * t, t), t), :], vt_ref[i]
    k_prev, vt_prev = k_ref[pl.ds(pl.multiple_of(prev * t, t), t), :], vt_ref[prev]
    zeros = jnp.zeros((1, t), F32)
    pv, runs = _sb_tiles(qh_t, [(k_diag, vt_diag, True), (k_prev, vt_prev, False)], [zeros for _ in heads],
                         later_ones, causal)
    for hh in heads:
        acc_ref[hh] = pv[0, hh] + has_prev * pv[1, hh]

    def still_visible(runs):
        return jnp.min(functools.reduce(jnp.minimum, runs)) < SB_LOG_FLOOR

    def cond(carry):
        j, _, go = carry
        return jnp.logical_and(j <= i, go)

    def body(carry):
        j, runs, _ = carry
        blk = i - j
        k, vt = k_ref[pl.ds(pl.multiple_of(blk * t, t), t), :], vt_ref[blk]
        pv, new_runs = _sb_tiles(qh_t, [(k, vt, False)], runs, later_ones, causal)
        for hh in heads:
            acc_ref[hh] += pv[0, hh]
        return j + 1, tuple(new_runs), still_visible(new_runs)

    lax.while_loop(cond, body, (jnp.int32(2), tuple(runs), still_visible(runs)))
    _store_heads(o_ref, [acc_ref[hh] for hh in heads])


def _sb_attention(qkv, bsz, seq):
    t = SB_TILE
    cb = lambda off: off // SB_LANES
    return pl.pallas_call(
        _sb_kernel,
        grid=(bsz, N_HEADS // SB_HEADS, seq // t),
        in_specs=[
            pl.BlockSpec((None, t, SB_LANES), lambda b, hg, i: (b, i, cb(COL_SB_Q) + hg)),
            pl.BlockSpec((None, seq, SB_LANES), lambda b, hg, i: (b, 0, cb(COL_SB_K) + hg)),
            pl.BlockSpec((None, seq, SB_LANES), lambda b, hg, i: (b, 0, cb(COL_SB_V) + hg)),
        ],
        out_specs=pl.BlockSpec((None, t, SB_LANES), lambda b, hg, i: (b, i, hg)),
        out_shape=jax.ShapeDtypeStruct((bsz, seq, BRANCH_W), BF16),
        scratch_shapes=[pltpu.VMEM((seq // t, SB_LANES, t), BF16),
                        pltpu.VMEM((SB_HEADS, HEAD_DIM, t), F32)],
        compiler_params=_params("parallel", "parallel", "arbitrary"),
        name="stick_breaking",
    )(qkv, qkv, qkv)


def _moba_kernel(slopes_ref, q_ref, k_ref, v_ref, o_ref, kmean_ref, vt_ref, bias_ref, chosen_ref, acc_ref,
                 s_ref, p_ref, *, n_blocks):
    t = MOBA_BLOCK
    head_group = pl.program_id(1)
    i = pl.program_id(2)
    nb_pad = kmean_ref.shape[0]
    vt_rows = acc_ref.shape[1]

    heads = range(ATTN_HEADS)
    slopes = [slopes_ref[head_group * ATTN_HEADS + hh] for hh in heads]
    key = lax.broadcasted_iota(jnp.int32, (t, t), 0)
    qry = lax.broadcasted_iota(jnp.int32, (t, t), 1)

    @pl.when(i == 0)
    def _():
        for n in range(n_blocks):
            vt = v_ref[n * t:(n + 1) * t, :].astype(F32).T
            for hh in heads:
                vt_ref[n, hh * vt_rows:hh * vt_rows + HEAD_DIM] = _head_rows(vt, hh).astype(BF16)
                vt_ref[n, hh * vt_rows + HEAD_DIM:(hh + 1) * vt_rows] = jnp.ones((vt_rows - HEAD_DIM, t), BF16)
        kmean_ref[...] = jnp.zeros_like(kmean_ref)
        for n in range(n_blocks):
            kb = k_ref[n * t:(n + 1) * t, :].astype(F32)
            kmean_ref[n:n + 1, :] = jnp.mean(kb, axis=0, keepdims=True)
        dist0 = (qry - key).astype(F32)
        for hh in heads:
            bias_ref[hh] = slopes[hh] * dist0

    blk_id = lax.broadcasted_iota(jnp.int32, (nb_pad, t), 0)
    past = blk_id < i
    q_raw = q_ref[...]
    qh_t = _transposed_heads(q_raw, HEAD_DIM ** -0.5)
    qg_t = _transposed_heads(q_raw, 1.0)
    km_hi, km_lo = _split_bf16(kmean_ref[...])

    def store_logits(slot, tile):
        k = k_ref[pl.ds(pl.multiple_of(tile * t, t), t), :]
        for hh in heads:
            s_ref[slot, hh] = _head_dot(k, qh_t[hh])

    def accumulate(slot, tile, alpha):
        vt = vt_ref[tile]
        for hh in heads:
            acc_ref[hh] = alpha[hh] * acc_ref[hh] + _dot(vt[hh * vt_rows:(hh + 1) * vt_rows, :], p_ref[slot, hh])

    store_logits(0, 0)
    k_own = k_ref[pl.ds(pl.multiple_of(i * t, t), t), :]
    s_own = [_head_dot(k_own, qh_t[hh]) for hh in heads]
    gates = [_head_dot(km_hi, qg_t[hh]) + _head_dot(km_lo, qg_t[hh]) for hh in heads]
    m = []
    for hh in heads:
        acc_ref[hh] = jnp.zeros((vt_rows, t), F32)
        gate = jnp.where(past, gates[hh], -jnp.inf)
        rank = jnp.zeros((nb_pad, t), jnp.int32)
        for blk in range(n_blocks):
            gb = gate[blk:blk + 1, :]
            ahead = (gb > gate) | ((gb == gate) & (blk_id > blk))
            rank += ahead.astype(jnp.int32)
        chosen_ref[hh] = jnp.where(past & (rank < MOBA_TOPK), 1.0, 0.0)

        s = jnp.where(key <= qry, s_own[hh] - bias_ref[hh], NEG_BIG)
        m0 = jnp.max(s, axis=0, keepdims=True)
        p_ref[1, hh] = jnp.exp(s - m0).astype(BF16)
        m.append(m0)
    ones = jnp.ones((1, t), F32)

    def step(slot, n, carry):
        alpha_prev, m_old, prev_tile = carry
        accumulate(1 - slot, prev_tile, alpha_prev)
        store_logits(1 - slot, jnp.minimum(n + 1, i))
        offset = ((i - n) * t).astype(F32)
        alpha, m_new = [], []
        for hh in heads:
            picked = chosen_ref[hh, pl.ds(n, 1), :]
            shift = jnp.where(picked > 0.0, -slopes[hh] * offset, NEG_BIG)
            s = s_ref[slot, hh] - bias_ref[hh]
            m_new.append(jnp.maximum(m_old[hh], jnp.max(s, axis=0, keepdims=True) + shift))
            alpha.append(jnp.exp(m_old[hh] - m_new[hh]))
            p_ref[slot, hh] = jnp.exp(s - (m_new[hh] - shift)).astype(BF16)
        return tuple(alpha), tuple(m_new), n

    def pair(j, carry):
        return step(1, 2 * j + 1, step(0, 2 * j, carry))

    init = (tuple(ones for _ in heads), tuple(m), i)
    alpha_last, _, last_tile = lax.fori_loop(0, (i + 1) // 2, pair, init)
    accumulate(1, last_tile, alpha_last)
    _store_heads(o_ref, [acc_ref[hh, :HEAD_DIM] / acc_ref[hh, HEAD_DIM:HEAD_DIM + 1] for hh in heads])


def _moba_attention(qkv, slopes, bsz, seq):
    t = MOBA_BLOCK
    n_blocks = seq // t
    nb_pad = -(-n_blocks // BF16_SUBLANES) * BF16_SUBLANES
    vt_rows = HEAD_DIM + BF16_SUBLANES
    assert seq % t == 0
    cb = lambda off: off // ATTN_LANES
    return pl.pallas_call(
        functools.partial(_moba_kernel, n_blocks=n_blocks),
        grid=(bsz, N_HEADS // ATTN_HEADS, n_blocks),
        in_specs=[
            pl.BlockSpec(memory_space=pltpu.SMEM),
            pl.BlockSpec((None, t, ATTN_LANES), lambda b, hg, i: (b, i, cb(COL_MOBA_Q) + hg)),
            pl.BlockSpec((None, seq, ATTN_LANES), lambda b, hg, i: (b, 0, cb(COL_MOBA_K) + hg)),
            pl.BlockSpec((None, seq, ATTN_LANES), lambda b, hg, i: (b, 0, cb(COL_MOBA_V) + hg)),
        ],
        out_specs=pl.BlockSpec((None, t, ATTN_LANES), lambda b, hg, i: (b, i, hg)),
        out_shape=jax.ShapeDtypeStruct((bsz, seq, BRANCH_W), BF16),
        scratch_shapes=[pltpu.VMEM((nb_pad, ATTN_LANES), F32),
                        pltpu.VMEM((n_blocks, ATTN_HEADS * vt_rows, t), BF16),
                        pltpu.VMEM((ATTN_HEADS, t, t), F32),
                        pltpu.VMEM((ATTN_HEADS, nb_pad, t), F32),
                        pltpu.VMEM((ATTN_HEADS, vt_rows, t), F32),
                        pltpu.VMEM((2, ATTN_HEADS, t, t), F32),
                        pltpu.VMEM((2, ATTN_HEADS, t, t), BF16)],
        compiler_params=_params("parallel", "parallel", "arbitrary"),
        name="moba",
    )(slopes, qkv, qkv, qkv)


def _swa_kernel(slope_ref, sink_ref, q_ref, k_ref, v_ref, o_ref, vt_ref, *, q_blocks):
    w = SWA_WINDOW
    tile = pl.program_id(1)
    group = N_HEADS // SWA_KV_HEADS
    gw = group * w

    @pl.when(tile == 0)
    def _():
        _store_transposed_values(v_ref, vt_ref, w)

    key = lax.broadcasted_iota(jnp.int32, (w, gw), 0)
    qry = lax.broadcasted_iota(jnp.int32, (w, gw), 1) % w
    dist = (qry - key).astype(F32)
    ok_cur = key <= qry
    in_window = key > qry
    no_dims = jnp.zeros((HEAD_DIM, w), F32)
    units = [(qb, kv) for qb in range(q_blocks) for kv in range(SWA_KV_HEADS)]

    blk, s_cur, s_prev = {}, {}, {}
    for qb in range(q_blocks):
        blk[qb] = tile * q_blocks + qb
        prev = jnp.maximum(blk[qb] - 1, 0)
        k_cur = k_ref[pl.ds(pl.multiple_of(blk[qb] * w, w), w), :]
        k_prev = k_ref[pl.ds(pl.multiple_of(prev * w, w), w), :]
        qt = q_ref[qb * w:(qb + 1) * w, :].astype(F32).T * (HEAD_DIM ** -0.5)
        for kv in range(SWA_KV_HEADS):
            cols = []
            for j in range(group):
                h = kv * group + j
                qh = qt[h * HEAD_DIM:(h + 1) * HEAD_DIM, :]
                cols.append(jnp.concatenate([qh, no_dims] if kv == 0 else [no_dims, qh], axis=0))
            qg = jnp.concatenate(cols, axis=1).astype(BF16)
            s_cur[qb, kv] = _dot(k_cur, qg)
            s_prev[qb, kv] = _dot(k_prev, qg)

    e_cur, e_prev, denom = {}, {}, {}
    for u in units:
        qb, kv = u
        slope, sink = slope_ref[kv], sink_ref[kv]
        bias = slope * dist
        sc = jnp.where(ok_cur, s_cur[u] - bias, NEG_BIG)
        sp = jnp.where(in_window & (blk[qb] > 0), s_prev[u] - bias - slope * float(w), NEG_BIG)
        m = jnp.maximum(jnp.maximum(jnp.max(sc, axis=0, keepdims=True), jnp.max(sp, axis=0, keepdims=True)), sink)
        ec, ep = jnp.exp(sc - m), jnp.exp(sp - m)
        denom[u] = jnp.sum(ec, axis=0, keepdims=True) + jnp.sum(ep, axis=0, keepdims=True) + jnp.exp(sink - m)
        e_cur[u], e_prev[u] = ec.astype(BF16), ep.astype(BF16)

    out_t = {}
    for u in units:
        qb, kv = u
        vt_cur = _head_rows(vt_ref[blk[qb]], kv)
        vt_prev = _head_rows(vt_ref[jnp.maximum(blk[qb] - 1, 0)], kv)
        out_t[u] = (_dot(vt_cur, e_cur[u]) + _dot(vt_prev, e_prev[u])) / denom[u]
    for qb in range(q_blocks):
        heads_t = [out_t[qb, kv][:, j * w:(j + 1) * w] for kv in range(SWA_KV_HEADS) for j in range(group)]
        o_ref[qb * w:(qb + 1) * w, :] = jnp.concatenate(heads_t, axis=0).T.astype(o_ref.dtype)


def _swa_attention(qkv, slopes, sinks, bsz, seq):
    w = SWA_WINDOW
    q_blocks = 2
    tq = q_blocks * w
    qw = N_HEADS * HEAD_DIM
    assert SWA_KV_HEADS * HEAD_DIM == LANES
    per_col = lambda v: jnp.repeat(v.reshape(SWA_KV_HEADS, N_HEADS // SWA_KV_HEADS), w, axis=1)[:, None, :]
    const_spec = pl.BlockSpec((SWA_KV_HEADS, 1, (N_HEADS // SWA_KV_HEADS) * w), lambda b, i: (0, 0, 0))
    return pl.pallas_call(
        functools.partial(_swa_kernel, q_blocks=q_blocks),
        grid=(bsz, seq // tq),
        in_specs=[
            const_spec,
            const_spec,
            pl.BlockSpec((None, tq, qw), lambda b, i: (b, i, COL_SWA_Q // qw)),
            pl.BlockSpec((None, seq, LANES), lambda b, i: (b, 0, COL_SWA_K // LANES)),
            pl.BlockSpec((None, seq, LANES), lambda b, i: (b, 0, COL_SWA_V // LANES)),
        ],
        out_specs=pl.BlockSpec((None, tq, qw), lambda b, i: (b, i, 0)),
        out_shape=jax.ShapeDtypeStruct((bsz, seq, BRANCH_W), BF16),
        scratch_shapes=[pltpu.VMEM((seq // w, LANES, w), BF16)],
        compiler_params=_params("parallel", "arbitrary"),
        name="swa",
    )(per_col(slopes), per_col(sinks), qkv, qkv, qkv)


def _conv_kernel(w_ref, b_ref, c_ref, h_ref, o_ref):
    u = c_ref[...].astype(F32) * h_ref[...].astype(F32)
    pos = lax.broadcasted_iota(jnp.int32, u.shape, 0)
    y = w_ref[CONV_K - 1:CONV_K, :] * u
    for back in range(1, CONV_K):
        shifted = jnp.where(pos >= back, pltpu.roll(u, back, 0), 0.0)
        y += w_ref[CONV_K - 1 - back:CONV_K - back, :] * shifted
    o_ref[...] = (b_ref[...].astype(F32) * y).astype(o_ref.dtype)


def _gated_conv(qkv, conv_w, bsz, seq):
    ch = conv_w.shape[1]
    cb = lambda off: off // LANES
    return pl.pallas_call(
        _conv_kernel,
        grid=(bsz, ch // LANES),
        in_specs=[
            pl.BlockSpec((CONV_K, LANES), lambda b, c: (0, c)),
            pl.BlockSpec((None, seq, LANES), lambda b, c: (b, 0, cb(COL_CONV_B) + c)),
            pl.BlockSpec((None, seq, LANES), lambda b, c: (b, 0, cb(COL_CONV_C) + c)),
            pl.BlockSpec((None, seq, LANES), lambda b, c: (b, 0, cb(COL_CONV_H) + c)),
        ],
        out_specs=pl.BlockSpec((None, seq, LANES), lambda b, c: (b, 0, c)),
        out_shape=jax.ShapeDtypeStruct((bsz, seq, ch), BF16),
        compiler_params=_params("parallel", "parallel"),
        name="gated_conv",
    )(conv_w, qkv, qkv, qkv)


def _merge_kernel(x_ref, g_ref, *refs):
    y_refs = refs[:N_BRANCH]
    wb_ref, wo_ref, o_ref, xn_ref = refs[-4:]
    wg_refs = refs[N_BRANCH:-4]
    parts = len(wg_refs) // N_BRANCH
    j = pl.program_id(1)

    def mixed(xn, rows):
        out = None
        ys = [y_ref[rows, :] for y_ref in y_refs]
        for p in range(parts):
            cols = slice(p * GATE_BLOCK, (p + 1) * GATE_BLOCK)
            merged = None
            for n in range(N_BRANCH):
                gate = jax.nn.sigmoid(_dot(xn, wg_refs[n * parts + p][0]))
                term = gate * _dot(ys[n], wb_ref[n, :, cols])
                merged = term if merged is None else merged + term
            part = _dot(merged.astype(BF16), wo_ref[cols, :])
            out = part if out is None else out + part
        return out

    @pl.when(j == 0)
    def _():
        x = x_ref[...]
        xn_ref[...] = _rms_norm(x, g_ref[...]).astype(BF16)
        o_ref[...] = x

    o_ref[...] += mixed(xn_ref[...], slice(None))


def _merge(x, g, ys, w_in, w_branch, w_out, layer):
    n, d = x.shape
    tm, tn = min(MERGE_TOKEN_TILE, n), MERGE_COL_TILE
    parts = tn // GATE_BLOCK
    assert QKV_COLS % LANES == 0 and tn % GATE_BLOCK == 0
    y_spec = pl.BlockSpec((tm, BRANCH_W), lambda i, j: (i, 0))

    def gate_spec(branch, part):
        first = QKV_COLS + branch * d + part * GATE_BLOCK
        return pl.BlockSpec((pl.Element(1), pl.Element(d), pl.Element(GATE_BLOCK)),
                            lambda i, j: (layer, 0, pl.multiple_of(first + j * tn, LANES)))

    gate_specs = [gate_spec(br, p) for br in range(N_BRANCH) for p in range(parts)]
    return pl.pallas_call(
        _merge_kernel,
        grid=(n // tm, d // tn),
        in_specs=[
            pl.BlockSpec((tm, d), lambda i, j: (i, 0)),
            pl.BlockSpec((1, d), lambda i, j: (0, 0)),
            y_spec, y_spec, y_spec, y_spec,
            *gate_specs,
            pl.BlockSpec((None, N_BRANCH, BRANCH_W, tn), lambda i, j: (layer, 0, 0, j)),
            pl.BlockSpec((None, tn, d), lambda i, j: (layer, j, 0)),
        ],
        out_specs=pl.BlockSpec((tm, d), lambda i, j: (i, 0)),
        out_shape=jax.ShapeDtypeStruct((n, d), F32),
        scratch_shapes=[pltpu.VMEM((tm, d), BF16)],
        compiler_params=_params("parallel", "arbitrary"),
        name="merge",
    )(x, g, *ys, *([w_in] * len(gate_specs)), w_branch, w_out)


def _alibi_slopes(n):
    return 2.0 ** (-8.0 * jnp.arange(1, n + 1, dtype=F32) / n)


def _mixer(h, bsz, seq, norm_g, w_in, conv_w, sinks, w_branch, w_out, layer):
    slopes = _alibi_slopes(2 * N_HEADS)
    qkv = _qkv_proj(h, norm_g, w_in, layer).reshape(bsz, seq, QKV_COLS)
    y_a = _moba_attention(qkv, slopes[N_HEADS:], bsz, seq)
    y_b = _sb_attention(qkv, bsz, seq)
    y_c = _swa_attention(qkv, slopes[:N_HEADS], sinks.astype(F32), bsz, seq)
    y_d = _gated_conv(qkv, conv_w.reshape(CONV_K, -1), bsz, seq)
    ys = [y.reshape(bsz * seq, BRANCH_W) for y in (y_a, y_b, y_c, y_d)]
    return _merge(h, norm_g, ys, w_in, w_branch, w_out, layer)


def kernel(x, ffn1_norm, ffn1_w1, ffn1_w3, ffn1_w2, mix_norm, w_in, conv_w, attn_sinks, w_branch, w_out,
           ffn2_norm, ffn2_w1, ffn2_w3, ffn2_w2, final_norm):
    bsz, seq, d = x.shape
    depth = w_in.shape[0]
    h = x.reshape(bsz * seq, d)
    row = lambda v: v.reshape(1, d)
    final_g = row(final_norm)
    ffn1 = [w.astype(BF16) for w in (ffn1_w1, ffn1_w3, ffn1_w2)]
    ffn2 = [w.astype(BF16) for w in (ffn2_w1, ffn2_w3, ffn2_w2)]
    w_in, w_branch, w_out = (w.astype(BF16) for w in (w_in, w_branch, w_out))
    for l in range(depth):
        h = _ffn(h, row(ffn1_norm[l]), *ffn1, final_g, layer=l, final_norm=False)
        h = _mixer(h, bsz, seq, row(mix_norm[l]), w_in, conv_w[l], attn_sinks[l], w_branch, w_out, l)
        h = _ffn(h, row(ffn2_norm[l]), *ffn2, final_g, layer=l, final_norm=(l == depth - 1))
    return h.reshape(bsz, seq, d)
```

```python
import functools

import jax
import jax.numpy as jnp
from jax import lax
from jax.experimental import pallas as pl
from jax.experimental.pallas import tpu as pltpu

F32 = jnp.float32
BF16 = jnp.bfloat16

EPS = 1e-6
HEAD_DIM = 64
LANES = 128
BF16_SUBLANES = 16
MXU_DEPTH = 256
ATTN_HEADS = 4
ATTN_LANES = ATTN_HEADS * HEAD_DIM
SB_HEADS = 8
SB_LANES = SB_HEADS * HEAD_DIM
MOBA_BLOCK = 256
MOBA_TOPK = 3
LOG2_E = 1.4426950408889634
MOBA_Q_SCALE = LOG2_E * HEAD_DIM ** -0.5
SWA_WINDOW = 128
CONV_K = 3
N_BRANCH = 4
BRANCH_W = 512
N_HEADS = 8
SWA_KV_HEADS = 2
SB_TILE = 256
SB_LOG_FLOOR = 110.0

COL_MOBA_Q, COL_MOBA_K, COL_MOBA_V = 0, 512, 1024
COL_SB_Q, COL_SB_K, COL_SB_V = 1536, 2048, 2560
COL_SWA_Q, COL_SWA_K, COL_SWA_V = 3072, 3584, 3712
COL_CONV_B, COL_CONV_C, COL_CONV_H = 3840, 4352, 4864
QKV_COLS = 5376

V7X_VMEM_BYTES = 64 * 1024 * 1024
VMEM_LIMIT = V7X_VMEM_BYTES - 8 * 1024 * 1024
NEG_BIG = -1e30

NORM_CHUNKS = 4
FFN_TOKEN_TILE = 1024
FF_TILE = 512
QKV_TOKEN_TILE = 1024
QKV_COL_TILE = 1792
MERGE_TOKEN_TILE = 512
MERGE_COL_TILE = 512
GATE_BLOCK = 512


def _params(*semantics):
    return pltpu.CompilerParams(dimension_semantics=semantics, vmem_limit_bytes=VMEM_LIMIT)


def _rms_norm(x, g):
    ms = jnp.mean(x * x, axis=-1, keepdims=True)
    return x * lax.rsqrt(ms + EPS) * g


def _dot(a, b):
    return jnp.dot(a, b, preferred_element_type=F32)


def _row_chunks(rows):
    chunk = rows // NORM_CHUNKS if rows % NORM_CHUNKS == 0 else rows
    return [pl.ds(start, chunk) for start in range(0, rows, chunk)]


def _split_bf16(x):
    hi = x.astype(BF16)
    lo = (x - hi.astype(F32)).astype(BF16)
    return hi, lo


def _ffn_kernel(x_ref, g_ref, w1_ref, w3_ref, w2_ref, fg_ref, o_ref, xn_ref, *, final_norm):
    f = pl.program_id(1)

    def half_swiglu(xn):
        a = _dot(xn, w1_ref[...])
        b = _dot(xn, w3_ref[...])
        hidden = (a * jax.nn.sigmoid(a) * b).astype(BF16)
        return 0.5 * _dot(hidden, w2_ref[...])

    @pl.when(f == 0)
    def _():
        x = x_ref[...]
        xn_ref[...] = _rms_norm(x, g_ref[...]).astype(BF16)
        o_ref[...] = x

    o_ref[...] += half_swiglu(xn_ref[...])

    if final_norm:
        @pl.when(f == pl.num_programs(1) - 1)
        def _():
            o_ref[...] = _rms_norm(o_ref[...], fg_ref[...])


def _ffn(x, g, w1, w3, w2, final_g, *, layer, final_norm):
    n, d = x.shape
    d_ff = w1.shape[2]
    tm, tf = min(FFN_TOKEN_TILE, n), FF_TILE
    return pl.pallas_call(
        functools.partial(_ffn_kernel, final_norm=final_norm),
        grid=(n // tm, d_ff // tf),
        in_specs=[
            pl.BlockSpec((tm, d), lambda i, f: (i, 0)),
            pl.BlockSpec((1, d), lambda i, f: (0, 0)),
            pl.BlockSpec((None, d, tf), lambda i, f: (layer, 0, f)),
            pl.BlockSpec((None, d, tf), lambda i, f: (layer, 0, f)),
            pl.BlockSpec((None, tf, d), lambda i, f: (layer, f, 0)),
            pl.BlockSpec((1, d), lambda i, f: (0, 0)),
        ],
        out_specs=pl.BlockSpec((tm, d), lambda i, f: (i, 0)),
        out_shape=jax.ShapeDtypeStruct((n, d), F32),
        scratch_shapes=[pltpu.VMEM((tm, d), BF16)],
        compiler_params=_params("parallel", "arbitrary"),
        name="ffn",
    )(x, g, w1, w3, w2, final_g)


def _qkv_kernel(x_ref, g_ref, w_ref, cs_ref, o_ref, xn_ref):
    j = pl.program_id(1)

    @pl.when(j == 0)
    def _():
        for rows in _row_chunks(x_ref.shape[0]):
            xn = _rms_norm(x_ref[rows, :], g_ref[...]).astype(BF16)
            xn_ref[rows, :] = xn
            o_ref[rows, :] = (_dot(xn, w_ref[...]) * cs_ref[...]).astype(BF16)

    @pl.when(j > 0)
    def _():
        o_ref[...] = (_dot(xn_ref[...], w_ref[...]) * cs_ref[...]).astype(BF16)


def _qkv_proj(x, g, w_in, layer):
    n, d = x.shape
    tm, tn = min(QKV_TOKEN_TILE, n), QKV_COL_TILE
    assert QKV_COLS % tn == 0
    col = jnp.arange(QKV_COLS)
    is_moba_q = (col >= COL_MOBA_Q) & (col < COL_MOBA_K)
    col_scale = jnp.where(is_moba_q, MOBA_Q_SCALE, 1.0).astype(F32).reshape(1, QKV_COLS)
    return pl.pallas_call(
        _qkv_kernel,
        grid=(n // tm, QKV_COLS // tn),
        in_specs=[
            pl.BlockSpec((tm, d), lambda i, j: (i, 0)),
            pl.BlockSpec((1, d), lambda i, j: (0, 0)),
            pl.BlockSpec((None, d, tn), lambda i, j: (layer, 0, j)),
            pl.BlockSpec((1, tn), lambda i, j: (0, j)),
        ],
        out_specs=pl.BlockSpec((tm, tn), lambda i, j: (i, j)),
        out_shape=jax.ShapeDtypeStruct((n, QKV_COLS), BF16),
        scratch_shapes=[pltpu.VMEM((tm, d), BF16)],
        compiler_params=_params("parallel", "arbitrary"),
        name="qkv_proj",
    )(x, g, w_in, col_scale)


def _store_transposed_values(v_ref, vt_ref, tile):
    for n in range(vt_ref.shape[0]):
        vt_ref[n] = v_ref[n * tile:(n + 1) * tile, :].astype(F32).T.astype(BF16)


def _head_rows(vt, head_in_block):
    return vt[head_in_block * HEAD_DIM:(head_in_block + 1) * HEAD_DIM, :]


def _transposed_heads(q, scale):
    qt = q.astype(F32).T * scale
    out = []
    for hh in range(q.shape[1] // HEAD_DIM):
        lo = (hh * HEAD_DIM // MXU_DEPTH) * MXU_DEPTH
        lanes = slice(lo, min(lo + MXU_DEPTH, qt.shape[0]))
        slab = qt[lanes]
        row = lo + lax.broadcasted_iota(jnp.int32, slab.shape, 0)
        out.append((jnp.where((row // HEAD_DIM) == hh, slab, 0.0).astype(BF16), lanes))
    return out


def _head_dot(x, q_head):
    slab, lanes = q_head
    return _dot(x[:, lanes], slab)


def _store_heads(o_ref, outs_t):
    o_ref[...] = jnp.concatenate(outs_t, axis=0).T.astype(o_ref.dtype)


def _softplus(z):
    return jnp.maximum(z, 0.0) + jnp.log(1.0 + jnp.exp(-jnp.abs(z)))


def _sb_tiles(qh_t, tiles, runs, later_ones, causal):
    chains = [(ti, hh) for ti in range(len(tiles)) for hh in range(len(qh_t))]
    z = {c: _head_dot(tiles[c[0]][0], qh_t[c[1]]) for c in chains}
    runs = list(runs)
    sp, parts, later = {}, {}, {}
    for c in chains:
        ti, hh = c
        sp[c] = _softplus(z[c])
        spm = jnp.where(causal, sp[c], 0.0) if tiles[ti][2] else sp[c]
        parts[c] = _split_bf16(spm)
        later[c] = runs[hh]
        runs[hh] = runs[hh] + jnp.sum(spm, axis=0, keepdims=True)
    after = {c: _dot(later_ones, parts[c][0]) + _dot(later_ones, parts[c][1]) + later[c] for c in chains}
    weights = {}
    for c in chains:
        a = jnp.exp(z[c] - sp[c] - after[c])
        if tiles[c[0]][2]:
            a = jnp.where(causal, a, 0.0)
        weights[c] = a.astype(BF16)
    return {c: _dot(_head_rows(tiles[c[0]][1], c[1]), weights[c]) for c in chains}, runs


def _sb_kernel(q_ref, k_ref, v_ref, o_ref, vt_ref, acc_ref):
    t = SB_TILE
    i = pl.program_id(2)

    @pl.when(i == 0)
    def _():
        _store_transposed_values(v_ref, vt_ref, t)

    key = lax.broadcasted_iota(jnp.int32, (t, t), 0)
    qry = lax.broadcasted_iota(jnp.int32, (t, t), 1)
    later_ones = (qry > key).astype(BF16)
    causal = key < qry
    qh_t = _transposed_heads(q_ref[...], HEAD_DIM ** -0.5)
    heads = range(len(qh_t))
    prev = jnp.maximum(i - 1, 0)
    has_prev = jnp.where(i > 0, 1.0, 0.0)
    k_diag, vt_diag = k_ref[pl.ds(pl.multiple_of(i * t, t), t), :], vt_ref[i]
    k_prev, vt_prev = k_ref[pl.ds(pl.multiple_of(prev * t, t), t), :], vt_ref[prev]
    zeros = jnp.zeros((1, t), F32)
    pv, runs = _sb_tiles(qh_t, [(k_diag, vt_diag, True), (k_prev, vt_prev, False)], [zeros for _ in heads],
                         later_ones, causal)
    for hh in heads:
        acc_ref[hh] = pv[0, hh] + has_prev * pv[1, hh]

    def still_visible(runs):
        return jnp.min(functools.reduce(jnp.minimum, runs)) < SB_LOG_FLOOR

    def cond(carry):
        j, _, go = carry
        return jnp.logical_and(j <= i, go)

    def body(carry):
        j, runs, _ = carry
        blk = i - j
        k, vt = k_ref[pl.ds(pl.multiple_of(blk * t, t), t), :], vt_ref[blk]
        pv, new_runs = _sb_tiles(qh_t, [(k, vt, False)], runs, later_ones, causal)
        for hh in heads:
            acc_ref[hh] += pv[0, hh]
        return j + 1, tuple(new_runs), still_visible(new_runs)

    lax.while_loop(cond, body, (jnp.int32(2), tuple(runs), still_visible(runs)))
    _store_heads(o_ref, [acc_ref[hh] for hh in heads])


def _sb_attention(qkv, bsz, seq):
    t = SB_TILE
    cb = lambda off: off // SB_LANES
    return pl.pallas_call(
        _sb_kernel,
        grid=(bsz, N_HEADS // SB_HEADS, seq // t),
        in_specs=[
            pl.BlockSpec((None, t, SB_LANES), lambda b, hg, i: (b, i, cb(COL_SB_Q) + hg)),
            pl.BlockSpec((None, seq, SB_LANES), lambda b, hg, i: (b, 0, cb(COL_SB_K) + hg)),
            pl.BlockSpec((None, seq, SB_LANES), lambda b, hg, i: (b, 0, cb(COL_SB_V) + hg)),
        ],
        out_specs=pl.BlockSpec((None, t, SB_LANES), lambda b, hg, i: (b, i, hg)),
        out_shape=jax.ShapeDtypeStruct((bsz, seq, BRANCH_W), BF16),
        scratch_shapes=[pltpu.VMEM((seq // t, SB_LANES, t), BF16),
                        pltpu.VMEM((SB_HEADS, HEAD_DIM, t), F32)],
        compiler_params=_params("parallel", "parallel", "arbitrary"),
        name="stick_breaking",
    )(qkv, qkv, qkv)


def _moba_kernel(slopes_ref, q_ref, k_ref, v_ref, o_ref, kmean_ref, vt_ref, chosen_ref, acc_ref, s_ref, p_ref,
                 *, n_blocks):
    t = MOBA_BLOCK
    head_group = pl.program_id(1)
    i = pl.program_id(2)
    nb_pad = kmean_ref.shape[0]
    vt_rows = acc_ref.shape[1]

    heads = range(ATTN_HEADS)
    slopes = [slopes_ref[head_group * ATTN_HEADS + hh] for hh in heads]
    key = lax.broadcasted_iota(jnp.int32, (t, t), 0)
    qry = lax.broadcasted_iota(jnp.int32, (t, t), 1)

    key_pos = lax.broadcasted_iota(jnp.int32, (1, t), 1).astype(F32)
    top = float(t - 1)

    @pl.when(i == 0)
    def _():
        for n in range(n_blocks):
            vt = v_ref[n * t:(n + 1) * t, :].astype(F32).T
            for hh in heads:
                key_w = jnp.exp2(slopes[hh] * key_pos)
                vt_ref[n, hh * vt_rows:hh * vt_rows + HEAD_DIM] = (_head_rows(vt, hh) * key_w).astype(BF16)
                vt_ref[n, hh * vt_rows + HEAD_DIM:(hh + 1) * vt_rows] = jnp.broadcast_to(
                    key_w, (vt_rows - HEAD_DIM, t)).astype(BF16)
        kmean_ref[...] = jnp.zeros_like(kmean_ref)
        for n in range(n_blocks):
            kb = k_ref[n * t:(n + 1) * t, :].astype(F32)
            kmean_ref[n:n + 1, :] = jnp.mean(kb, axis=0, keepdims=True)

    blk_id = lax.broadcasted_iota(jnp.int32, (nb_pad, t), 0)
    past = blk_id < i
    q_raw = q_ref[...]
    qh_t = _transposed_heads(q_raw, 1.0)
    km_hi, km_lo = _split_bf16(kmean_ref[...])

    def store_logits(slot, tile):
        k = k_ref[pl.ds(pl.multiple_of(tile * t, t), t), :]
        for hh in heads:
            s_ref[slot, hh] = _head_dot(k, qh_t[hh])

    def accumulate(slot, tile, alpha):
        vt = vt_ref[tile]
        for hh in heads:
            acc_ref[hh] = alpha[hh] * acc_ref[hh] + _dot(vt[hh * vt_rows:(hh + 1) * vt_rows, :], p_ref[slot, hh])

    store_logits(0, 0)
    k_own = k_ref[pl.ds(pl.multiple_of(i * t, t), t), :]
    s_own = [_head_dot(k_own, qh_t[hh]) for hh in heads]
    gates = [_head_dot(km_hi, qh_t[hh]) + _head_dot(km_lo, qh_t[hh]) for hh in heads]
    m = []
    for hh in heads:
        acc_ref[hh] = jnp.zeros((vt_rows, t), F32)
        gate = jnp.where(past, gates[hh], -jnp.inf)
        rank = jnp.zeros((nb_pad, t), jnp.int32)
        for blk in range(n_blocks):
            gb = gate[blk:blk + 1, :]
            ahead = (gb > gate) | ((gb == gate) & (blk_id > blk))
            rank += ahead.astype(jnp.int32)
        chosen_ref[hh] = jnp.where(past & (rank < MOBA_TOPK), 1.0, 0.0)

        s = jnp.where(key <= qry, s_own[hh], NEG_BIG)
        m0 = jnp.max(s, axis=0, keepdims=True) + slopes[hh] * top
        p_ref[1, hh] = jnp.exp2(s - m0).astype(BF16)
        m.append(m0)
    ones = jnp.ones((1, t), F32)

    def step(slot, n, carry):
        alpha_prev, m_old, prev_tile = carry
        accumulate(1 - slot, prev_tile, alpha_prev)
        store_logits(1 - slot, jnp.minimum(n + 1, i))
        offset = ((i - n) * t).astype(F32)
        alpha, m_new = [], []
        for hh in heads:
            picked = chosen_ref[hh, pl.ds(n, 1), :]
            shift = jnp.where(picked > 0.0, -slopes[hh] * offset, NEG_BIG)
            s = s_ref[slot, hh]
            bound = jnp.max(s, axis=0, keepdims=True) + (shift + slopes[hh] * top)
            m_new.append(jnp.maximum(m_old[hh], bound))
            alpha.append(jnp.exp2(m_old[hh] - m_new[hh]))
            p_ref[slot, hh] = jnp.exp2(s - (m_new[hh] - shift)).astype(BF16)
        return tuple(alpha), tuple(m_new), n

    def pair(j, carry):
        return step(1, 2 * j + 1, step(0, 2 * j, carry))

    init = (tuple(ones for _ in heads), tuple(m), i)
    alpha_last, _, last_tile = lax.fori_loop(0, (i + 1) // 2, pair, init)
    accumulate(1, last_tile, alpha_last)
    _store_heads(o_ref, [acc_ref[hh, :HEAD_DIM] / acc_ref[hh, HEAD_DIM:HEAD_DIM + 1] for hh in heads])


def _moba_attention(qkv, slopes, bsz, seq):
    t = MOBA_BLOCK
    n_blocks = seq // t
    nb_pad = -(-n_blocks // BF16_SUBLANES) * BF16_SUBLANES
    vt_rows = HEAD_DIM + BF16_SUBLANES
    assert seq % t == 0
    cb = lambda off: off // ATTN_LANES
    return pl.pallas_call(
        functools.partial(_moba_kernel, n_blocks=n_blocks),
        grid=(bsz, N_HEADS // ATTN_HEADS, n_blocks),
        in_specs=[
            pl.BlockSpec(memory_space=pltpu.SMEM),
            pl.BlockSpec((None, t, ATTN_LANES), lambda b, hg, i: (b, i, cb(COL_MOBA_Q) + hg)),
            pl.BlockSpec((None, seq, ATTN_LANES), lambda b, hg, i: (b, 0, cb(COL_MOBA_K) + hg)),
            pl.BlockSpec((None, seq, ATTN_LANES), lambda b, hg, i: (b, 0, cb(COL_MOBA_V) + hg)),
        ],
        out_specs=pl.BlockSpec((None, t, ATTN_LANES), lambda b, hg, i: (b, i, hg)),
        out_shape=jax.ShapeDtypeStruct((bsz, seq, BRANCH_W), BF16),
        scratch_shapes=[pltpu.VMEM((nb_pad, ATTN_LANES), F32),
                        pltpu.VMEM((n_blocks, ATTN_HEADS * vt_rows, t), BF16),
                        pltpu.VMEM((ATTN_HEADS, nb_pad, t), F32),
                        pltpu.VMEM((ATTN_HEADS, vt_rows, t), F32),
                        pltpu.VMEM((2, ATTN_HEADS, t, t), F32),
                        pltpu.VMEM((2, ATTN_HEADS, t, t), BF16)],
        compiler_params=_params("parallel", "parallel", "arbitrary"),
        name="moba",
    )(slopes, qkv, qkv, qkv)


def _swa_kernel(slope_ref, sink_ref, q_ref, k_ref, v_ref, o_ref, vt_ref, *, q_blocks):
    w = SWA_WINDOW
    tile = pl.program_id(1)
    group = N_HEADS // SWA_KV_HEADS
    gw = group * w

    @pl.when(tile == 0)
    def _():
        _store_transposed_values(v_ref, vt_ref, w)

    key = lax.broadcasted_iota(jnp.int32, (w, gw), 0)
    qry = lax.broadcasted_iota(jnp.int32, (w, gw), 1) % w
    dist = (qry - key).astype(F32)
    ok_cur = key <= qry
    in_window = key > qry
    no_dims = jnp.zeros((HEAD_DIM, w), F32)
    units = [(qb, kv) for qb in range(q_blocks) for kv in range(SWA_KV_HEADS)]

    blk, s_cur, s_prev = {}, {}, {}
    for qb in range(q_blocks):
        blk[qb] = tile * q_blocks + qb
        prev = jnp.maximum(blk[qb] - 1, 0)
        k_cur = k_ref[pl.ds(pl.multiple_of(blk[qb] * w, w), w), :]
        k_prev = k_ref[pl.ds(pl.multiple_of(prev * w, w), w), :]
        qt = q_ref[qb * w:(qb + 1) * w, :].astype(F32).T * (HEAD_DIM ** -0.5)
        for kv in range(SWA_KV_HEADS):
            cols = []
            for j in range(group):
                h = kv * group + j
                qh = qt[h * HEAD_DIM:(h + 1) * HEAD_DIM, :]
                cols.append(jnp.concatenate([qh, no_dims] if kv == 0 else [no_dims, qh], axis=0))
            qg = jnp.concatenate(cols, axis=1).astype(BF16)
            s_cur[qb, kv] = _dot(k_cur, qg)
            s_prev[qb, kv] = _dot(k_prev, qg)

    e_cur, e_prev, denom = {}, {}, {}
    for u in units:
        qb, kv = u
        slope, sink = slope_ref[kv], sink_ref[kv]
        bias = slope * dist
        sc = jnp.where(ok_cur, s_cur[u] - bias, NEG_BIG)
        sp = jnp.where(in_window & (blk[qb] > 0), s_prev[u] - bias - slope * float(w), NEG_BIG)
        m = jnp.maximum(jnp.maximum(jnp.max(sc, axis=0, keepdims=True), jnp.max(sp, axis=0, keepdims=True)), sink)
        ec, ep = jnp.exp(sc - m), jnp.exp(sp - m)
        denom[u] = jnp.sum(ec, axis=0, keepdims=True) + jnp.sum(ep, axis=0, keepdims=True) + jnp.exp(sink - m)
        e_cur[u], e_prev[u] = ec.astype(BF16), ep.astype(BF16)

    out_t = {}
    for u in units:
        qb, kv = u
        vt_cur = _head_rows(vt_ref[blk[qb]], kv)
        vt_prev = _head_rows(vt_ref[jnp.maximum(blk[qb] - 1, 0)], kv)
        out_t[u] = (_dot(vt_cur, e_cur[u]) + _dot(vt_prev, e_prev[u])) / denom[u]
    for qb in range(q_blocks):
        heads_t = [out_t[qb, kv][:, j * w:(j + 1) * w] for kv in range(SWA_KV_HEADS) for j in range(group)]
        o_ref[qb * w:(qb + 1) * w, :] = jnp.concatenate(heads_t, axis=0).T.astype(o_ref.dtype)


def _swa_attention(qkv, slopes, sinks, bsz, seq):
    w = SWA_WINDOW
    q_blocks = 2
    tq = q_blocks * w
    qw = N_HEADS * HEAD_DIM
    assert SWA_KV_HEADS * HEAD_DIM == LANES
    per_col = lambda v: jnp.repeat(v.reshape(SWA_KV_HEADS, N_HEADS // SWA_KV_HEADS), w, axis=1)[:, None, :]
    const_spec = pl.BlockSpec((SWA_KV_HEADS, 1, (N_HEADS // SWA_KV_HEADS) * w), lambda b, i: (0, 0, 0))
    return pl.pallas_call(
        functools.partial(_swa_kernel, q_blocks=q_blocks),
        grid=(bsz, seq // tq),
        in_specs=[
            const_spec,
            const_spec,
            pl.BlockSpec((None, tq, qw), lambda b, i: (b, i, COL_SWA_Q // qw)),
            pl.BlockSpec((None, seq, LANES), lambda b, i: (b, 0, COL_SWA_K // LANES)),
            pl.BlockSpec((None, seq, LANES), lambda b, i: (b, 0, COL_SWA_V // LANES)),
        ],
        out_specs=pl.BlockSpec((None, tq, qw), lambda b, i: (b, i, 0)),
        out_shape=jax.ShapeDtypeStruct((bsz, seq, BRANCH_W), BF16),
        scratch_shapes=[pltpu.VMEM((seq // w, LANES, w), BF16)],
        compiler_params=_params("parallel", "arbitrary"),
        name="swa",
    )(per_col(slopes), per_col(sinks), qkv, qkv, qkv)


def _conv_kernel(w_ref, b_ref, c_ref, h_ref, o_ref):
    u = c_ref[...].astype(F32) * h_ref[...].astype(F32)
    pos = lax.broadcasted_iota(jnp.int32, u.shape, 0)
    y = w_ref[CONV_K - 1:CONV_K, :] * u
    for back in range(1, CONV_K):
        shifted = jnp.where(pos >= back, pltpu.roll(u, back, 0), 0.0)
        y += w_ref[CONV_K - 1 - back:CONV_K - back, :] * shifted
    o_ref[...] = (b_ref[...].astype(F32) * y).astype(o_ref.dtype)


def _gated_conv(qkv, conv_w, bsz, seq):
    ch = conv_w.shape[1]
    cb = lambda off: off // LANES
    return pl.pallas_call(
        _conv_kernel,
        grid=(bsz, ch // LANES),
        in_specs=[
            pl.BlockSpec((CONV_K, LANES), lambda b, c: (0, c)),
            pl.BlockSpec((None, seq, LANES), lambda b, c: (b, 0, cb(COL_CONV_B) + c)),
            pl.BlockSpec((None, seq, LANES), lambda b, c: (b, 0, cb(COL_CONV_C) + c)),
            pl.BlockSpec((None, seq, LANES), lambda b, c: (b, 0, cb(COL_CONV_H) + c)),
        ],
        out_specs=pl.BlockSpec((None, seq, LANES), lambda b, c: (b, 0, c)),
        out_shape=jax.ShapeDtypeStruct((bsz, seq, ch), BF16),
        compiler_params=_params("parallel", "parallel"),
        name="gated_conv",
    )(conv_w, qkv, qkv, qkv)


def _merge_kernel(x_ref, g_ref, *refs):
    y_refs = refs[:N_BRANCH]
    wb_ref, wo_ref, o_ref, xn_ref = refs[-4:]
    wg_refs = refs[N_BRANCH:-4]
    parts = len(wg_refs) // N_BRANCH
    j = pl.program_id(1)

    def mixed(xn, rows):
        out = None
        ys = [y_ref[rows, :] for y_ref in y_refs]
        for p in range(parts):
            cols = slice(p * GATE_BLOCK, (p + 1) * GATE_BLOCK)
            merged = None
            for n in range(N_BRANCH):
                gate = jax.nn.sigmoid(_dot(xn, wg_refs[n * parts + p][0]))
                term = gate * _dot(ys[n], wb_ref[n, :, cols])
                merged = term if merged is None else merged + term
            part = _dot(merged.astype(BF16), wo_ref[cols, :])
            out = part if out is None else out + part
        return out

    @pl.when(j == 0)
    def _():
        x = x_ref[...]
        xn_ref[...] = _rms_norm(x, g_ref[...]).astype(BF16)
        o_ref[...] = x

    o_ref[...] += mixed(xn_ref[...], slice(None))


def _merge(x, g, ys, w_in, w_branch, w_out, layer):
    n, d = x.shape
    tm, tn = min(MERGE_TOKEN_TILE, n), MERGE_COL_TILE
    parts = tn // GATE_BLOCK
    assert QKV_COLS % LANES == 0 and tn % GATE_BLOCK == 0
    y_spec = pl.BlockSpec((tm, BRANCH_W), lambda i, j: (i, 0))

    def gate_spec(branch, part):
        first = QKV_COLS + branch * d + part * GATE_BLOCK
        return pl.BlockSpec((pl.Element(1), pl.Element(d), pl.Element(GATE_BLOCK)),
                            lambda i, j: (layer, 0, pl.multiple_of(first + j * tn, LANES)))

    gate_specs = [gate_spec(br, p) for br in range(N_BRANCH) for p in range(parts)]
    return pl.pallas_call(
        _merge_kernel,
        grid=(n // tm, d // tn),
        in_specs=[
            pl.BlockSpec((tm, d), lambda i, j: (i, 0)),
            pl.BlockSpec((1, d), lambda i, j: (0, 0)),
            y_spec, y_spec, y_spec, y_spec,
            *gate_specs,
            pl.BlockSpec((None, N_BRANCH, BRANCH_W, tn), lambda i, j: (layer, 0, 0, j)),
            pl.BlockSpec((None, tn, d), lambda i, j: (layer, j, 0)),
        ],
        out_specs=pl.BlockSpec((tm, d), lambda i, j: (i, 0)),
        out_shape=jax.ShapeDtypeStruct((n, d), F32),
        scratch_shapes=[pltpu.VMEM((tm, d), BF16)],
        compiler_params=_params("parallel", "arbitrary"),
        name="merge",
    )(x, g, *ys, *([w_in] * len(gate_specs)), w_branch, w_out)


def _alibi_slopes(n):
    return 2.0 ** (-8.0 * jnp.arange(1, n + 1, dtype=F32) / n)


def _mixer(h, bsz, seq, norm_g, w_in, conv_w, sinks, w_branch, w_out, layer):
    slopes = _alibi_slopes(2 * N_HEADS)
    qkv = _qkv_proj(h, norm_g, w_in, layer).reshape(bsz, seq, QKV_COLS)
    y_a = _moba_attention(qkv, slopes[N_HEADS:] * LOG2_E, bsz, seq)
    y_b = _sb_attention(qkv, bsz, seq)
    y_c = _swa_attention(qkv, slopes[:N_HEADS], sinks.astype(F32), bsz, seq)
    y_d = _gated_conv(qkv, conv_w.reshape(CONV_K, -1), bsz, seq)
    ys = [y.reshape(bsz * seq, BRANCH_W) for y in (y_a, y_b, y_c, y_d)]
    return _merge(h, norm_g, ys, w_in, w_branch, w_out, layer)


def kernel(x, ffn1_norm, ffn1_w1, ffn1_w3, ffn1_w2, mix_norm, w_in, conv_w, attn_sinks, w_branch, w_out,
           ffn2_norm, ffn2_w1, ffn2_w3, ffn2_w2, final_norm):
    bsz, seq, d = x.shape
    depth = w_in.shape[0]
    h = x.reshape(bsz * seq, d)
    row = lambda v: v.reshape(1, d)
    final_g = row(final_norm)
    ffn1 = [w.astype(BF16) for w in (ffn1_w1, ffn1_w3, ffn1_w2)]
    ffn2 = [w.astype(BF16) for w in (ffn2_w1, ffn2_w3, ffn2_w2)]
    w_in, w_branch, w_out = (w.astype(BF16) for w in (w_in, w_branch, w_out))
    for l in range(depth):
        h = _ffn(h, row(ffn1_norm[l]), *ffn1, final_g, layer=l, final_norm=False)
        h = _mixer(h, bsz, seq, row(mix_norm[l]), w_in, conv_w[l], attn_sinks[l], w_branch, w_out, l)
        h = _ffn(h, row(ffn2_norm[l]), *ffn2, final_g, layer=l, final_norm=(l == depth - 1))
    return h.reshape(bsz, seq, d)
```

```python
import functools

import jax
import jax.numpy as jnp
from jax import lax
from jax.experimental import pallas as pl
from jax.experimental.pallas import tpu as pltpu

F32 = jnp.float32
BF16 = jnp.bfloat16

EPS = 1e-6
HEAD_DIM = 64
LANES = 128
BF16_SUBLANES = 16
MXU_DEPTH = 256
ATTN_HEADS = 4
ATTN_LANES = ATTN_HEADS * HEAD_DIM
SB_HEADS = 8
SB_LANES = SB_HEADS * HEAD_DIM
MOBA_BLOCK = 256
MOBA_TOPK = 3
LOG2_E = 1.4426950408889634
BASE2_Q_SCALE = LOG2_E * HEAD_DIM ** -0.5
SWA_WINDOW = 128
CONV_K = 3
N_BRANCH = 4
BRANCH_W = 512
N_HEADS = 8
SWA_KV_HEADS = 2
SB_TILE = 256
SB_LOG2_FLOOR = 110.0 * LOG2_E

COL_MOBA_Q, COL_MOBA_K, COL_MOBA_V = 0, 512, 1024
COL_SB_Q, COL_SB_K, COL_SB_V = 1536, 2048, 2560
COL_SWA_Q, COL_SWA_K, COL_SWA_V = 3072, 3584, 3712
COL_CONV_B, COL_CONV_C, COL_CONV_H = 3840, 4352, 4864
QKV_COLS = 5376

V7X_VMEM_BYTES = 64 * 1024 * 1024
VMEM_LIMIT = V7X_VMEM_BYTES - 8 * 1024 * 1024
NEG_BIG = -1e30

NORM_CHUNKS = 4
FFN_TOKEN_TILE = 1024
FF_TILE = 512
QKV_TOKEN_TILE = 1024
QKV_COL_TILE = 1792
MERGE_TOKEN_TILE = 512
MERGE_COL_TILE = 512
GATE_BLOCK = 512


def _params(*semantics):
    return pltpu.CompilerParams(dimension_semantics=semantics, vmem_limit_bytes=VMEM_LIMIT)


def _rms_norm(x, g):
    ms = jnp.mean(x * x, axis=-1, keepdims=True)
    return x * lax.rsqrt(ms + EPS) * g


def _dot(a, b):
    return jnp.dot(a, b, preferred_element_type=F32)


def _row_chunks(rows):
    chunk = rows // NORM_CHUNKS if rows % NORM_CHUNKS == 0 else rows
    return [pl.ds(start, chunk) for start in range(0, rows, chunk)]


def _split_bf16(x):
    hi = x.astype(BF16)
    lo = (x - hi.astype(F32)).astype(BF16)
    return hi, lo


def _ffn_kernel(x_ref, g_ref, w1_ref, w3_ref, w2_ref, fg_ref, o_ref, xn_ref, *, final_norm):
    f = pl.program_id(1)

    def half_swiglu(xn):
        a = _dot(xn, w1_ref[...])
        b = _dot(xn, w3_ref[...])
        hidden = (a * jax.nn.sigmoid(a) * b).astype(BF16)
        return 0.5 * _dot(hidden, w2_ref[...])

    @pl.when(f == 0)
    def _():
        x = x_ref[...]
        xn_ref[...] = _rms_norm(x, g_ref[...]).astype(BF16)
        o_ref[...] = x

    o_ref[...] += half_swiglu(xn_ref[...])

    if final_norm:
        @pl.when(f == pl.num_programs(1) - 1)
        def _():
            o_ref[...] = _rms_norm(o_ref[...], fg_ref[...])


def _ffn(x, g, w1, w3, w2, final_g, *, layer, final_norm):
    n, d = x.shape
    d_ff = w1.shape[2]
    tm, tf = min(FFN_TOKEN_TILE, n), FF_TILE
    return pl.pallas_call(
        functools.partial(_ffn_kernel, final_norm=final_norm),
        grid=(n // tm, d_ff // tf),
        in_specs=[
            pl.BlockSpec((tm, d), lambda i, f: (i, 0)),
            pl.BlockSpec((1, d), lambda i, f: (0, 0)),
            pl.BlockSpec((None, d, tf), lambda i, f: (layer, 0, f)),
            pl.BlockSpec((None, d, tf), lambda i, f: (layer, 0, f)),
            pl.BlockSpec((None, tf, d), lambda i, f: (layer, f, 0)),
            pl.BlockSpec((1, d), lambda i, f: (0, 0)),
        ],
        out_specs=pl.BlockSpec((tm, d), lambda i, f: (i, 0)),
        out_shape=jax.ShapeDtypeStruct((n, d), F32),
        scratch_shapes=[pltpu.VMEM((tm, d), BF16)],
        compiler_params=_params("parallel", "arbitrary"),
        name="ffn",
    )(x, g, w1, w3, w2, final_g)


def _qkv_kernel(x_ref, g_ref, w_ref, cs_ref, o_ref, xn_ref):
    j = pl.program_id(1)

    @pl.when(j == 0)
    def _():
        for rows in _row_chunks(x_ref.shape[0]):
            xn = _rms_norm(x_ref[rows, :], g_ref[...]).astype(BF16)
            xn_ref[rows, :] = xn
            o_ref[rows, :] = (_dot(xn, w_ref[...]) * cs_ref[...]).astype(BF16)

    @pl.when(j > 0)
    def _():
        o_ref[...] = (_dot(xn_ref[...], w_ref[...]) * cs_ref[...]).astype(BF16)


def _qkv_proj(x, g, w_in, layer):
    n, d = x.shape
    tm, tn = min(QKV_TOKEN_TILE, n), QKV_COL_TILE
    assert QKV_COLS % tn == 0
    col = jnp.arange(QKV_COLS)
    is_moba_q = (col >= COL_MOBA_Q) & (col < COL_MOBA_K)
    is_sb_q = (col >= COL_SB_Q) & (col < COL_SB_K)
    col_scale = jnp.where(is_moba_q | is_sb_q, BASE2_Q_SCALE, 1.0).astype(F32).reshape(1, QKV_COLS)
    return pl.pallas_call(
        _qkv_kernel,
        grid=(n // tm, QKV_COLS // tn),
        in_specs=[
            pl.BlockSpec((tm, d), lambda i, j: (i, 0)),
            pl.BlockSpec((1, d), lambda i, j: (0, 0)),
            pl.BlockSpec((None, d, tn), lambda i, j: (layer, 0, j)),
            pl.BlockSpec((1, tn), lambda i, j: (0, j)),
        ],
        out_specs=pl.BlockSpec((tm, tn), lambda i, j: (i, j)),
        out_shape=jax.ShapeDtypeStruct((n, QKV_COLS), BF16),
        scratch_shapes=[pltpu.VMEM((tm, d), BF16)],
        compiler_params=_params("parallel", "arbitrary"),
        name="qkv_proj",
    )(x, g, w_in, col_scale)


def _store_transposed_values(v_ref, vt_ref, tile):
    for n in range(vt_ref.shape[0]):
        vt_ref[n] = v_ref[n * tile:(n + 1) * tile, :].astype(F32).T.astype(BF16)


def _head_rows(vt, head_in_block):
    return vt[head_in_block * HEAD_DIM:(head_in_block + 1) * HEAD_DIM, :]


def _transposed_heads(q, scale):
    qt = q.astype(F32).T * scale
    out = []
    for hh in range(q.shape[1] // HEAD_DIM):
        lo = (hh * HEAD_DIM // MXU_DEPTH) * MXU_DEPTH
        lanes = slice(lo, min(lo + MXU_DEPTH, qt.shape[0]))
        slab = qt[lanes]
        row = lo + lax.broadcasted_iota(jnp.int32, slab.shape, 0)
        out.append((jnp.where((row // HEAD_DIM) == hh, slab, 0.0).astype(BF16), lanes))
    return out


def _head_dot(x, q_head):
    slab, lanes = q_head
    return _dot(x[:, lanes], slab)


def _store_heads(o_ref, outs_t):
    o_ref[...] = jnp.concatenate(outs_t, axis=0).T.astype(o_ref.dtype)


def _softplus_base2(z):
    return jnp.maximum(z, 0.0) + jnp.log(1.0 + jnp.exp2(-jnp.abs(z))) * LOG2_E


def _sb_tiles(qh_t, tiles, runs, later_ones, causal):
    chains = [(ti, hh) for ti in range(len(tiles)) for hh in range(len(qh_t))]
    z = {c: _head_dot(tiles[c[0]][0], qh_t[c[1]]) for c in chains}
    sp, first, parts = {}, {}, {}
    for c in chains:
        sp[c] = _softplus_base2(z[c])
        spm = jnp.where(causal, sp[c], 0.0) if tiles[c[0]][2] else sp[c]
        first[c] = spm[0:1, :]
        parts[c] = _split_bf16(spm)
    local = {c: _dot(later_ones, parts[c][0]) + _dot(later_ones, parts[c][1]) for c in chains}
    runs = list(runs)
    weights = {}
    for c in chains:
        hh = c[1]
        a = jnp.exp2(z[c] - sp[c] - (local[c] + runs[hh]))
        runs[hh] = runs[hh] + (local[c][0:1, :] + first[c])
        if tiles[c[0]][2]:
            a = jnp.where(causal, a, 0.0)
        weights[c] = a.astype(BF16)
    return {c: _dot(_head_rows(tiles[c[0]][1], c[1]), weights[c]) for c in chains}, runs


def _sb_kernel(q_ref, k_ref, v_ref, o_ref, vt_ref, acc_ref):
    t = SB_TILE
    i = pl.program_id(2)

    @pl.when(i == 0)
    def _():
        _store_transposed_values(v_ref, vt_ref, t)

    key = lax.broadcasted_iota(jnp.int32, (t, t), 0)
    qry = lax.broadcasted_iota(jnp.int32, (t, t), 1)
    later_ones = (qry > key).astype(BF16)
    causal = key < qry
    qh_t = _transposed_heads(q_ref[...], 1.0)
    heads = range(len(qh_t))
    prev = jnp.maximum(i - 1, 0)
    has_prev = jnp.where(i > 0, 1.0, 0.0)
    k_diag, vt_diag = k_ref[pl.ds(pl.multiple_of(i * t, t), t), :], vt_ref[i]
    k_prev, vt_prev = k_ref[pl.ds(pl.multiple_of(prev * t, t), t), :], vt_ref[prev]
    zeros = jnp.zeros((1, t), F32)
    pv, runs = _sb_tiles(qh_t, [(k_diag, vt_diag, True), (k_prev, vt_prev, False)], [zeros for _ in heads],
                         later_ones, causal)
    for hh in heads:
        acc_ref[hh] = pv[0, hh] + has_prev * pv[1, hh]

    def still_visible(runs):
        return jnp.min(functools.reduce(jnp.minimum, runs)) < SB_LOG2_FLOOR

    def cond(carry):
        j, _, go = carry
        return jnp.logical_and(j <= i, go)

    def body(carry):
        j, runs, _ = carry
        blk = i - j
        k, vt = k_ref[pl.ds(pl.multiple_of(blk * t, t), t), :], vt_ref[blk]
        pv, new_runs = _sb_tiles(qh_t, [(k, vt, False)], runs, later_ones, causal)
        for hh in heads:
            acc_ref[hh] += pv[0, hh]
        return j + 1, tuple(new_runs), still_visible(new_runs)

    lax.while_loop(cond, body, (jnp.int32(2), tuple(runs), still_visible(runs)))
    _store_heads(o_ref, [acc_ref[hh] for hh in heads])


def _sb_attention(qkv, bsz, seq):
    t = SB_TILE
    cb = lambda off: off // SB_LANES
    return pl.pallas_call(
        _sb_kernel,
        grid=(bsz, N_HEADS // SB_HEADS, seq // t),
        in_specs=[
            pl.BlockSpec((None, t, SB_LANES), lambda b, hg, i: (b, i, cb(COL_SB_Q) + hg)),
            pl.BlockSpec((None, seq, SB_LANES), lambda b, hg, i: (b, 0, cb(COL_SB_K) + hg)),
            pl.BlockSpec((None, seq, SB_LANES), lambda b, hg, i: (b, 0, cb(COL_SB_V) + hg)),
        ],
        out_specs=pl.BlockSpec((None, t, SB_LANES), lambda b, hg, i: (b, i, hg)),
        out_shape=jax.ShapeDtypeStruct((bsz, seq, BRANCH_W), BF16),
        scratch_shapes=[pltpu.VMEM((seq // t, SB_LANES, t), BF16),
                        pltpu.VMEM((SB_HEADS, HEAD_DIM, t), F32)],
        compiler_params=_params("parallel", "parallel", "arbitrary"),
        name="stick_breaking",
    )(qkv, qkv, qkv)


def _moba_kernel(slopes_ref, q_ref, k_ref, v_ref, o_ref, kmean_ref, vt_ref, chosen_ref, acc_ref, s_ref, p_ref,
                 *, n_blocks):
    t = MOBA_BLOCK
    head_group = pl.program_id(1)
    i = pl.program_id(2)
    nb_pad = kmean_ref.shape[0]
    vt_rows = acc_ref.shape[1]

    heads = range(ATTN_HEADS)
    slopes = [slopes_ref[head_group * ATTN_HEADS + hh] for hh in heads]
    key = lax.broadcasted_iota(jnp.int32, (t, t), 0)
    qry = lax.broadcasted_iota(jnp.int32, (t, t), 1)

    key_pos = lax.broadcasted_iota(jnp.int32, (1, t), 1).astype(F32)
    top = float(t - 1)

    @pl.when(i == 0)
    def _():
        for n in range(n_blocks):
            vt = v_ref[n * t:(n + 1) * t, :].astype(F32).T
            for hh in heads:
                key_w = jnp.exp2(slopes[hh] * key_pos)
                vt_ref[n, hh * vt_rows:hh * vt_rows + HEAD_DIM] = (_head_rows(vt, hh) * key_w).astype(BF16)
                vt_ref[n, hh * vt_rows + HEAD_DIM:(hh + 1) * vt_rows] = jnp.broadcast_to(
                    key_w, (vt_rows - HEAD_DIM, t)).astype(BF16)
        kmean_ref[...] = jnp.zeros_like(kmean_ref)
        for n in range(n_blocks):
            kb = k_ref[n * t:(n + 1) * t, :].astype(F32)
            kmean_ref[n:n + 1, :] = jnp.mean(kb, axis=0, keepdims=True)

    blk_id = lax.broadcasted_iota(jnp.int32, (nb_pad, t), 0)
    past = blk_id < i
    q_raw = q_ref[...]
    qh_t = _transposed_heads(q_raw, 1.0)
    km_hi, km_lo = _split_bf16(kmean_ref[...])

    def store_logits(slot, tile):
        k = k_ref[pl.ds(pl.multiple_of(tile * t, t), t), :]
        for hh in heads:
            s_ref[slot, hh] = _head_dot(k, qh_t[hh])

    def accumulate(slot, tile, alpha):
        vt = vt_ref[tile]
        for hh in heads:
            acc_ref[hh] = alpha[hh] * acc_ref[hh] + _dot(vt[hh * vt_rows:(hh + 1) * vt_rows, :], p_ref[slot, hh])

    store_logits(0, 0)
    k_own = k_ref[pl.ds(pl.multiple_of(i * t, t), t), :]
    s_own = [_head_dot(k_own, qh_t[hh]) for hh in heads]
    gates = [_head_dot(km_hi, qh_t[hh]) + _head_dot(km_lo, qh_t[hh]) for hh in heads]
    m = []
    for hh in heads:
        acc_ref[hh] = jnp.zeros((vt_rows, t), F32)
        gate = jnp.where(past, gates[hh], -jnp.inf)
        rank = jnp.zeros((nb_pad, t), jnp.int32)
        for blk in range(n_blocks):
            gb = gate[blk:blk + 1, :]
            ahead = (gb > gate) | ((gb == gate) & (blk_id > blk))
            rank += ahead.astype(jnp.int32)
        chosen_ref[hh] = jnp.where(past & (rank < MOBA_TOPK), 1.0, 0.0)

        s = jnp.where(key <= qry, s_own[hh], NEG_BIG)
        m0 = jnp.max(s, axis=0, keepdims=True) + slopes[hh] * top
        p_ref[1, hh] = jnp.exp2(s - m0).astype(BF16)
        m.append(m0)
    ones = jnp.ones((1, t), F32)

    def step(slot, n, carry):
        alpha_prev, m_old, prev_tile = carry
        accumulate(1 - slot, prev_tile, alpha_prev)
        store_logits(1 - slot, jnp.minimum(n + 1, i))
        offset = ((i - n) * t).astype(F32)
        alpha, m_new = [], []
        for hh in heads:
            picked = chosen_ref[hh, pl.ds(n, 1), :]
            shift = jnp.where(picked > 0.0, -slopes[hh] * offset, NEG_BIG)
            s = s_ref[slot, hh]
            bound = jnp.max(s, axis=0, keepdims=True) + (shift + slopes[hh] * top)
            m_new.append(jnp.maximum(m_old[hh], bound))
            alpha.append(jnp.exp2(m_old[hh] - m_new[hh]))
            p_ref[slot, hh] = jnp.exp2(s - (m_new[hh] - shift)).astype(BF16)
        return tuple(alpha), tuple(m_new), n

    def pair(j, carry):
        return step(1, 2 * j + 1, step(0, 2 * j, carry))

    init = (tuple(ones for _ in heads), tuple(m), i)
    alpha_last, _, last_tile = lax.fori_loop(0, (i + 1) // 2, pair, init)
    accumulate(1, last_tile, alpha_last)
    _store_heads(o_ref, [acc_ref[hh, :HEAD_DIM] / acc_ref[hh, HEAD_DIM:HEAD_DIM + 1] for hh in heads])


def _moba_attention(qkv, slopes, bsz, seq):
    t = MOBA_BLOCK
    n_blocks = seq // t
    nb_pad = -(-n_blocks // BF16_SUBLANES) * BF16_SUBLANES
    vt_rows = HEAD_DIM + BF16_SUBLANES
    assert seq % t == 0
    cb = lambda off: off // ATTN_LANES
    return pl.pallas_call(
        functools.partial(_moba_kernel, n_blocks=n_blocks),
        grid=(bsz, N_HEADS // ATTN_HEADS, n_blocks),
        in_specs=[
            pl.BlockSpec(memory_space=pltpu.SMEM),
            pl.BlockSpec((None, t, ATTN_LANES), lambda b, hg, i: (b, i, cb(COL_MOBA_Q) + hg)),
            pl.BlockSpec((None, seq, ATTN_LANES), lambda b, hg, i: (b, 0, cb(COL_MOBA_K) + hg)),
            pl.BlockSpec((None, seq, ATTN_LANES), lambda b, hg, i: (b, 0, cb(COL_MOBA_V) + hg)),
        ],
        out_specs=pl.BlockSpec((None, t, ATTN_LANES), lambda b, hg, i: (b, i, hg)),
        out_shape=jax.ShapeDtypeStruct((bsz, seq, BRANCH_W), BF16),
        scratch_shapes=[pltpu.VMEM((nb_pad, ATTN_LANES), F32),
                        pltpu.VMEM((n_blocks, ATTN_HEADS * vt_rows, t), BF16),
                        pltpu.VMEM((ATTN_HEADS, nb_pad, t), F32),
                        pltpu.VMEM((ATTN_HEADS, vt_rows, t), F32),
                        pltpu.VMEM((2, ATTN_HEADS, t, t), F32),
                        pltpu.VMEM((2, ATTN_HEADS, t, t), BF16)],
        compiler_params=_params("parallel", "parallel", "arbitrary"),
        name="moba",
    )(slopes, qkv, qkv, qkv)


def _swa_kernel(slope_ref, sink_ref, q_ref, k_ref, v_ref, o_ref, vt_ref, *, q_blocks):
    w = SWA_WINDOW
    tile = pl.program_id(1)
    group = N_HEADS // SWA_KV_HEADS
    gw = group * w

    @pl.when(tile == 0)
    def _():
        _store_transposed_values(v_ref, vt_ref, w)

    key = lax.broadcasted_iota(jnp.int32, (w, gw), 0)
    qry = lax.broadcasted_iota(jnp.int32, (w, gw), 1) % w
    dist = (qry - key).astype(F32)
    ok_cur = key <= qry
    in_window = key > qry
    no_dims = jnp.zeros((HEAD_DIM, w), F32)
    units = [(qb, kv) for qb in range(q_blocks) for kv in range(SWA_KV_HEADS)]

    blk, s_cur, s_prev = {}, {}, {}
    for qb in range(q_blocks):
        blk[qb] = tile * q_blocks + qb
        prev = jnp.maximum(blk[qb] - 1, 0)
        k_cur = k_ref[pl.ds(pl.multiple_of(blk[qb] * w, w), w), :]
        k_prev = k_ref[pl.ds(pl.multiple_of(prev * w, w), w), :]
        qt = q_ref[qb * w:(qb + 1) * w, :].astype(F32).T * (HEAD_DIM ** -0.5)
        for kv in range(SWA_KV_HEADS):
            cols = []
            for j in range(group):
                h = kv * group + j
                qh = qt[h * HEAD_DIM:(h + 1) * HEAD_DIM, :]
                cols.append(jnp.concatenate([qh, no_dims] if kv == 0 else [no_dims, qh], axis=0))
            qg = jnp.concatenate(cols, axis=1).astype(BF16)
            s_cur[qb, kv] = _dot(k_cur, qg)
            s_prev[qb, kv] = _dot(k_prev, qg)

    e_cur, e_prev, denom = {}, {}, {}
    for u in units:
        qb, kv = u
        slope, sink = slope_ref[kv], sink_ref[kv]
        bias = slope * dist
        sc = jnp.where(ok_cur, s_cur[u] - bias, NEG_BIG)
        sp = jnp.where(in_window & (blk[qb] > 0), s_prev[u] - bias - slope * float(w), NEG_BIG)
        m = jnp.maximum(jnp.maximum(jnp.max(sc, axis=0, keepdims=True), jnp.max(sp, axis=0, keepdims=True)), sink)
        ec, ep = jnp.exp(sc - m), jnp.exp(sp - m)
        denom[u] = jnp.sum(ec, axis=0, keepdims=True) + jnp.sum(ep, axis=0, keepdims=True) + jnp.exp(sink - m)
        e_cur[u], e_prev[u] = ec.astype(BF16), ep.astype(BF16)

    out_t = {}
    for u in units:
        qb, kv = u
        vt_cur = _head_rows(vt_ref[blk[qb]], kv)
        vt_prev = _head_rows(vt_ref[jnp.maximum(blk[qb] - 1, 0)], kv)
        out_t[u] = (_dot(vt_cur, e_cur[u]) + _dot(vt_prev, e_prev[u])) / denom[u]
    for qb in range(q_blocks):
        heads_t = [out_t[qb, kv][:, j * w:(j + 1) * w] for kv in range(SWA_KV_HEADS) for j in range(group)]
        o_ref[qb * w:(qb + 1) * w, :] = jnp.concatenate(heads_t, axis=0).T.astype(o_ref.dtype)


def _swa_attention(qkv, slopes, sinks, bsz, seq):
    w = SWA_WINDOW
    q_blocks = 2
    tq = q_blocks * w
    qw = N_HEADS * HEAD_DIM
    assert SWA_KV_HEADS * HEAD_DIM == LANES
    per_col = lambda v: jnp.repeat(v.reshape(SWA_KV_HEADS, N_HEADS // SWA_KV_HEADS), w, axis=1)[:, None, :]
    const_spec = pl.BlockSpec((SWA_KV_HEADS, 1, (N_HEADS // SWA_KV_HEADS) * w), lambda b, i: (0, 0, 0))
    return pl.pallas_call(
        functools.partial(_swa_kernel, q_blocks=q_blocks),
        grid=(bsz, seq // tq),
        in_specs=[
            const_spec,
            const_spec,
            pl.BlockSpec((None, tq, qw), lambda b, i: (b, i, COL_SWA_Q // qw)),
            pl.BlockSpec((None, seq, LANES), lambda b, i: (b, 0, COL_SWA_K // LANES)),
            pl.BlockSpec((None, seq, LANES), lambda b, i: (b, 0, COL_SWA_V // LANES)),
        ],
        out_specs=pl.BlockSpec((None, tq, qw), lambda b, i: (b, i, 0)),
        out_shape=jax.ShapeDtypeStruct((bsz, seq, BRANCH_W), BF16),
        scratch_shapes=[pltpu.VMEM((seq // w, LANES, w), BF16)],
        compiler_params=_params("parallel", "arbitrary"),
        name="swa",
    )(per_col(slopes), per_col(sinks), qkv, qkv, qkv)


def _conv_kernel(w_ref, b_ref, c_ref, h_ref, o_ref):
    u = c_ref[...].astype(F32) * h_ref[...].astype(F32)
    pos = lax.broadcasted_iota(jnp.int32, u.shape, 0)
    y = w_ref[CONV_K - 1:CONV_K, :] * u
    for back in range(1, CONV_K):
        shifted = jnp.where(pos >= back, pltpu.roll(u, back, 0), 0.0)
        y += w_ref[CONV_K - 1 - back:CONV_K - back, :] * shifted
    o_ref[...] = (b_ref[...].astype(F32) * y).astype(o_ref.dtype)


def _gated_conv(qkv, conv_w, bsz, seq):
    ch = conv_w.shape[1]
    cb = lambda off: off // LANES
    return pl.pallas_call(
        _conv_kernel,
        grid=(bsz, ch // LANES),
        in_specs=[
            pl.BlockSpec((CONV_K, LANES), lambda b, c: (0, c)),
            pl.BlockSpec((None, seq, LANES), lambda b, c: (b, 0, cb(COL_CONV_B) + c)),
            pl.BlockSpec((None, seq, LANES), lambda b, c: (b, 0, cb(COL_CONV_C) + c)),
            pl.BlockSpec((None, seq, LANES), lambda b, c: (b, 0, cb(COL_CONV_H) + c)),
        ],
        out_specs=pl.BlockSpec((None, seq, LANES), lambda b, c: (b, 0, c)),
        out_shape=jax.ShapeDtypeStruct((bsz, seq, ch), BF16),
        compiler_params=_params("parallel", "parallel"),
        name="gated_conv",
    )(conv_w, qkv, qkv, qkv)


def _merge_kernel(x_ref, g_ref, *refs):
    y_refs = refs[:N_BRANCH]
    wb_ref, wo_ref, o_ref, xn_ref = refs[-4:]
    wg_refs = refs[N_BRANCH:-4]
    parts = len(wg_refs) // N_BRANCH
    j = pl.program_id(1)

    def mixed(xn, rows):
        out = None
        ys = [y_ref[rows, :] for y_ref in y_refs]
        for p in range(parts):
            cols = slice(p * GATE_BLOCK, (p + 1) * GATE_BLOCK)
            merged = None
            for n in range(N_BRANCH):
                gate = jax.nn.sigmoid(_dot(xn, wg_refs[n * parts + p][0]))
                term = gate * _dot(ys[n], wb_ref[n, :, cols])
                merged = term if merged is None else merged + term
            part = _dot(merged.astype(BF16), wo_ref[cols, :])
            out = part if out is None else out + part
        return out

    @pl.when(j == 0)
    def _():
        x = x_ref[...]
        xn_ref[...] = _rms_norm(x, g_ref[...]).astype(BF16)
        o_ref[...] = x

    o_ref[...] += mixed(xn_ref[...], slice(None))


def _merge(x, g, ys, w_in, w_branch, w_out, layer):
    n, d = x.shape
    tm, tn = min(MERGE_TOKEN_TILE, n), MERGE_COL_TILE
    parts = tn // GATE_BLOCK
    assert QKV_COLS % LANES == 0 and tn % GATE_BLOCK == 0
    y_spec = pl.BlockSpec((tm, BRANCH_W), lambda i, j: (i, 0))

    def gate_spec(branch, part):
        first = QKV_COLS + branch * d + part * GATE_BLOCK
        return pl.BlockSpec((pl.Element(1), pl.Element(d), pl.Element(GATE_BLOCK)),
                            lambda i, j: (layer, 0, pl.multiple_of(first + j * tn, LANES)))

    gate_specs = [gate_spec(br, p) for br in range(N_BRANCH) for p in range(parts)]
    return pl.pallas_call(
        _merge_kernel,
        grid=(n // tm, d // tn),
        in_specs=[
            pl.BlockSpec((tm, d), lambda i, j: (i, 0)),
            pl.BlockSpec((1, d), lambda i, j: (0, 0)),
            y_spec, y_spec, y_spec, y_spec,
            *gate_specs,
            pl.BlockSpec((None, N_BRANCH, BRANCH_W, tn), lambda i, j: (layer, 0, 0, j)),
            pl.BlockSpec((None, tn, d), lambda i, j: (layer, j, 0)),
        ],
        out_specs=pl.BlockSpec((tm, d), lambda i, j: (i, 0)),
        out_shape=jax.ShapeDtypeStruct((n, d), F32),
        scratch_shapes=[pltpu.VMEM((tm, d), BF16)],
        compiler_params=_params("parallel", "arbitrary"),
        name="merge",
    )(x, g, *ys, *([w_in] * len(gate_specs)), w_branch, w_out)


def _alibi_slopes(n):
    return 2.0 ** (-8.0 * jnp.arange(1, n + 1, dtype=F32) / n)


def _mixer(h, bsz, seq, norm_g, w_in, conv_w, sinks, w_branch, w_out, layer):
    slopes = _alibi_slopes(2 * N_HEADS)
    qkv = _qkv_proj(h, norm_g, w_in, layer).reshape(bsz, seq, QKV_COLS)
    y_a = _moba_attention(qkv, slopes[N_HEADS:] * LOG2_E, bsz, seq)
    y_b = _sb_attention(qkv, bsz, seq)
    y_c = _swa_attention(qkv, slopes[:N_HEADS], sinks.astype(F32), bsz, seq)
    y_d = _gated_conv(qkv, conv_w.reshape(CONV_K, -1), bsz, seq)
    ys = [y.reshape(bsz * seq, BRANCH_W) for y in (y_a, y_b, y_c, y_d)]
    return _merge(h, norm_g, ys, w_in, w_branch, w_out, layer)


def kernel(x, ffn1_norm, ffn1_w1, ffn1_w3, ffn1_w2, mix_norm, w_in, conv_w, attn_sinks, w_branch, w_out,
           ffn2_norm, ffn2_w1, ffn2_w3, ffn2_w2, final_norm):
    bsz, seq, d = x.shape
    depth = w_in.shape[0]
    h = x.reshape(bsz * seq, d)
    row = lambda v: v.reshape(1, d)
    final_g = row(final_norm)
    ffn1 = [w.astype(BF16) for w in (ffn1_w1, ffn1_w3, ffn1_w2)]
    ffn2 = [w.astype(BF16) for w in (ffn2_w1, ffn2_w3, ffn2_w2)]
    w_in, w_branch, w_out = (w.astype(BF16) for w in (w_in, w_branch, w_out))
    for l in range(depth):
        h = _ffn(h, row(ffn1_norm[l]), *ffn1, final_g, layer=l, final_norm=False)
        h = _mixer(h, bsz, seq, row(mix_norm[l]), w_in, conv_w[l], attn_sinks[l], w_branch, w_out, l)
        h = _ffn(h, row(ffn2_norm[l]), *ffn2, final_g, layer=l, final_norm=(l == depth - 1))
    return h.reshape(bsz, seq, d)
```

```python
import functools

import jax
import jax.numpy as jnp
from jax import lax
from jax.experimental import pallas as pl
from jax.experimental.pallas import tpu as pltpu

F32 = jnp.float32
BF16 = jnp.bfloat16

EPS = 1e-6
HEAD_DIM = 64
LANES = 128
BF16_SUBLANES = 16
MXU_DEPTH = 256
ATTN_HEADS = 8
ATTN_LANES = ATTN_HEADS * HEAD_DIM
SB_HEADS = 8
SB_LANES = SB_HEADS * HEAD_DIM
MOBA_BLOCK = 256
MOBA_TOPK = 3
LOG2_E = 1.4426950408889634
BASE2_Q_SCALE = LOG2_E * HEAD_DIM ** -0.5
SWA_WINDOW = 128
CONV_K = 3
N_BRANCH = 4
BRANCH_W = 512
N_HEADS = 8
SWA_KV_HEADS = 2
SB_TILE = 256
SB_LOG2_FLOOR = 110.0 * LOG2_E

COL_MOBA_Q, COL_MOBA_K, COL_MOBA_V = 0, 512, 1024
COL_SB_Q, COL_SB_K, COL_SB_V = 1536, 2048, 2560
COL_SWA_Q, COL_SWA_K, COL_SWA_V = 3072, 3584, 3712
COL_CONV_B, COL_CONV_C, COL_CONV_H = 3840, 4352, 4864
QKV_COLS = 5376

V7X_VMEM_BYTES = 64 * 1024 * 1024
VMEM_LIMIT = V7X_VMEM_BYTES - 8 * 1024 * 1024
NEG_BIG = -1e30

NORM_CHUNKS = 4
FFN_TOKEN_TILE = 1024
FF_TILE = 512
QKV_TOKEN_TILE = 1024
QKV_COL_TILE = 1792
MERGE_TOKEN_TILE = 512
MERGE_COL_TILE = 512
GATE_BLOCK = 512


def _params(*semantics):
    return pltpu.CompilerParams(dimension_semantics=semantics, vmem_limit_bytes=VMEM_LIMIT)


def _rms_norm(x, g):
    ms = jnp.mean(x * x, axis=-1, keepdims=True)
    return x * lax.rsqrt(ms + EPS) * g


def _dot(a, b):
    return jnp.dot(a, b, preferred_element_type=F32)


def _row_chunks(rows):
    chunk = rows // NORM_CHUNKS if rows % NORM_CHUNKS == 0 else rows
    return [pl.ds(start, chunk) for start in range(0, rows, chunk)]


def _split_bf16(x):
    hi = x.astype(BF16)
    lo = (x - hi.astype(F32)).astype(BF16)
    return hi, lo


def _ffn_kernel(x_ref, g_ref, w1_ref, w3_ref, w2_ref, fg_ref, o_ref, xn_ref, *, final_norm):
    f = pl.program_id(1)

    def half_swiglu(xn):
        a = _dot(xn, w1_ref[...])
        b = _dot(xn, w3_ref[...])
        hidden = (a * jax.nn.sigmoid(a) * b).astype(BF16)
        return 0.5 * _dot(hidden, w2_ref[...])

    @pl.when(f == 0)
    def _():
        x = x_ref[...]
        xn_ref[...] = _rms_norm(x, g_ref[...]).astype(BF16)
        o_ref[...] = x

    o_ref[...] += half_swiglu(xn_ref[...])

    if final_norm:
        @pl.when(f == pl.num_programs(1) - 1)
        def _():
            o_ref[...] = _rms_norm(o_ref[...], fg_ref[...])


def _ffn(x, g, w1, w3, w2, final_g, *, layer, final_norm):
    n, d = x.shape
    d_ff = w1.shape[2]
    tm, tf = min(FFN_TOKEN_TILE, n), FF_TILE
    return pl.pallas_call(
        functools.partial(_ffn_kernel, final_norm=final_norm),
        grid=(n // tm, d_ff // tf),
        in_specs=[
            pl.BlockSpec((tm, d), lambda i, f: (i, 0)),
            pl.BlockSpec((1, d), lambda i, f: (0, 0)),
            pl.BlockSpec((None, d, tf), lambda i, f: (layer, 0, f)),
            pl.BlockSpec((None, d, tf), lambda i, f: (layer, 0, f)),
            pl.BlockSpec((None, tf, d), lambda i, f: (layer, f, 0)),
            pl.BlockSpec((1, d), lambda i, f: (0, 0)),
        ],
        out_specs=pl.BlockSpec((tm, d), lambda i, f: (i, 0)),
        out_shape=jax.ShapeDtypeStruct((n, d), F32),
        scratch_shapes=[pltpu.VMEM((tm, d), BF16)],
        compiler_params=_params("parallel", "arbitrary"),
        name="ffn",
    )(x, g, w1, w3, w2, final_g)


def _qkv_kernel(x_ref, g_ref, w_ref, cs_ref, o_ref, xn_ref):
    j = pl.program_id(1)

    @pl.when(j == 0)
    def _():
        for rows in _row_chunks(x_ref.shape[0]):
            xn = _rms_norm(x_ref[rows, :], g_ref[...]).astype(BF16)
            xn_ref[rows, :] = xn
            o_ref[rows, :] = (_dot(xn, w_ref[...]) * cs_ref[...]).astype(BF16)

    @pl.when(j > 0)
    def _():
        o_ref[...] = (_dot(xn_ref[...], w_ref[...]) * cs_ref[...]).astype(BF16)


def _qkv_proj(x, g, w_in, layer):
    n, d = x.shape
    tm, tn = min(QKV_TOKEN_TILE, n), QKV_COL_TILE
    assert QKV_COLS % tn == 0
    col = jnp.arange(QKV_COLS)
    is_moba_q = (col >= COL_MOBA_Q) & (col < COL_MOBA_K)
    is_sb_q = (col >= COL_SB_Q) & (col < COL_SB_K)
    col_scale = jnp.where(is_moba_q | is_sb_q, BASE2_Q_SCALE, 1.0).astype(F32).reshape(1, QKV_COLS)
    return pl.pallas_call(
        _qkv_kernel,
        grid=(n // tm, QKV_COLS // tn),
        in_specs=[
            pl.BlockSpec((tm, d), lambda i, j: (i, 0)),
            pl.BlockSpec((1, d), lambda i, j: (0, 0)),
            pl.BlockSpec((None, d, tn), lambda i, j: (layer, 0, j)),
            pl.BlockSpec((1, tn), lambda i, j: (0, j)),
        ],
        out_specs=pl.BlockSpec((tm, tn), lambda i, j: (i, j)),
        out_shape=jax.ShapeDtypeStruct((n, QKV_COLS), BF16),
        scratch_shapes=[pltpu.VMEM((tm, d), BF16)],
        compiler_params=_params("parallel", "arbitrary"),
        name="qkv_proj",
    )(x, g, w_in, col_scale)


def _store_transposed_values(v_ref, vt_ref, tile):
    for n in range(vt_ref.shape[0]):
        vt_ref[n] = v_ref[n * tile:(n + 1) * tile, :].astype(F32).T.astype(BF16)


def _head_rows(vt, head_in_block):
    return vt[head_in_block * HEAD_DIM:(head_in_block + 1) * HEAD_DIM, :]


def _transposed_heads(q, scale):
    qt = q.astype(F32).T * scale
    out = []
    for hh in range(q.shape[1] // HEAD_DIM):
        lo = (hh * HEAD_DIM // MXU_DEPTH) * MXU_DEPTH
        lanes = slice(lo, min(lo + MXU_DEPTH, qt.shape[0]))
        slab = qt[lanes]
        row = lo + lax.broadcasted_iota(jnp.int32, slab.shape, 0)
        out.append((jnp.where((row // HEAD_DIM) == hh, slab, 0.0).astype(BF16), lanes))
    return out


def _head_dot(x, q_head):
    slab, lanes = q_head
    return _dot(x[:, lanes], slab)


def _store_heads(o_ref, outs_t):
    o_ref[...] = jnp.concatenate(outs_t, axis=0).T.astype(o_ref.dtype)


def _softplus_base2(z):
    return jnp.maximum(z, 0.0) + jnp.log(1.0 + jnp.exp2(-jnp.abs(z))) * LOG2_E


def _sb_tiles(qh_t, tiles, runs, later_ones, causal):
    chains = [(ti, hh) for ti in range(len(tiles)) for hh in range(len(qh_t))]
    z = {c: _head_dot(tiles[c[0]][0], qh_t[c[1]]) for c in chains}
    sp, first, parts = {}, {}, {}
    for c in chains:
        sp[c] = _softplus_base2(z[c])
        spm = jnp.where(causal, sp[c], 0.0) if tiles[c[0]][2] else sp[c]
        first[c] = spm[0:1, :]
        parts[c] = _split_bf16(spm)
    local = {c: _dot(later_ones, parts[c][0]) + _dot(later_ones, parts[c][1]) for c in chains}
    runs = list(runs)
    weights = {}
    for c in chains:
        hh = c[1]
        a = jnp.exp2(z[c] - sp[c] - (local[c] + runs[hh]))
        runs[hh] = runs[hh] + (local[c][0:1, :] + first[c])
        if tiles[c[0]][2]:
            a = jnp.where(causal, a, 0.0)
        weights[c] = a.astype(BF16)
    return {c: _dot(_head_rows(tiles[c[0]][1], c[1]), weights[c]) for c in chains}, runs


def _sb_kernel(q_ref, k_ref, v_ref, o_ref, vt_ref, acc_ref):
    t = SB_TILE
    i = pl.program_id(2)

    @pl.when(i == 0)
    def _():
        _store_transposed_values(v_ref, vt_ref, t)

    key = lax.broadcasted_iota(jnp.int32, (t, t), 0)
    qry = lax.broadcasted_iota(jnp.int32, (t, t), 1)
    later_ones = (qry > key).astype(BF16)
    causal = key < qry
    qh_t = _transposed_heads(q_ref[...], 1.0)
    heads = range(len(qh_t))
    prev = jnp.maximum(i - 1, 0)
    has_prev = jnp.where(i > 0, 1.0, 0.0)
    k_diag, vt_diag = k_ref[pl.ds(pl.multiple_of(i * t, t), t), :], vt_ref[i]
    k_prev, vt_prev = k_ref[pl.ds(pl.multiple_of(prev * t, t), t), :], vt_ref[prev]
    zeros = jnp.zeros((1, t), F32)
    pv, runs = _sb_tiles(qh_t, [(k_diag, vt_diag, True), (k_prev, vt_prev, False)], [zeros for _ in heads],
                         later_ones, causal)
    for hh in heads:
        acc_ref[hh] = pv[0, hh] + has_prev * pv[1, hh]

    def still_visible(runs):
        return jnp.min(functools.reduce(jnp.minimum, runs)) < SB_LOG2_FLOOR

    def cond(carry):
        j, _, go = carry
        return jnp.logical_and(j <= i, go)

    def body(carry):
        j, runs, _ = carry
        blk = i - j
        k, vt = k_ref[pl.ds(pl.multiple_of(blk * t, t), t), :], vt_ref[blk]
        pv, new_runs = _sb_tiles(qh_t, [(k, vt, False)], runs, later_ones, causal)
        for hh in heads:
            acc_ref[hh] += pv[0, hh]
        return j + 1, tuple(new_runs), still_visible(new_runs)

    lax.while_loop(cond, body, (jnp.int32(2), tuple(runs), still_visible(runs)))
    _store_heads(o_ref, [acc_ref[hh] for hh in heads])


def _sb_attention(qkv, bsz, seq):
    t = SB_TILE
    cb = lambda off: off // SB_LANES
    return pl.pallas_call(
        _sb_kernel,
        grid=(bsz, N_HEADS // SB_HEADS, seq // t),
        in_specs=[
            pl.BlockSpec((None, t, SB_LANES), lambda b, hg, i: (b, i, cb(COL_SB_Q) + hg)),
            pl.BlockSpec((None, seq, SB_LANES), lambda b, hg, i: (b, 0, cb(COL_SB_K) + hg)),
            pl.BlockSpec((None, seq, SB_LANES), lambda b, hg, i: (b, 0, cb(COL_SB_V) + hg)),
        ],
        out_specs=pl.BlockSpec((None, t, SB_LANES), lambda b, hg, i: (b, i, hg)),
        out_shape=jax.ShapeDtypeStruct((bsz, seq, BRANCH_W), BF16),
        scratch_shapes=[pltpu.VMEM((seq // t, SB_LANES, t), BF16),
                        pltpu.VMEM((SB_HEADS, HEAD_DIM, t), F32)],
        compiler_params=_params("parallel", "parallel", "arbitrary"),
        name="stick_breaking",
    )(qkv, qkv, qkv)


def _moba_kernel(slopes_ref, q_ref, k_ref, v_ref, o_ref, kmean_ref, vt_ref, chosen_ref, acc_ref, s_ref, p_ref,
                 *, n_blocks):
    t = MOBA_BLOCK
    head_group = pl.program_id(1)
    i = pl.program_id(2)
    nb_pad = kmean_ref.shape[0]
    vt_rows = acc_ref.shape[1]

    heads = range(ATTN_HEADS)
    slopes = [slopes_ref[head_group * ATTN_HEADS + hh] for hh in heads]
    key = lax.broadcasted_iota(jnp.int32, (t, t), 0)
    qry = lax.broadcasted_iota(jnp.int32, (t, t), 1)

    key_pos = lax.broadcasted_iota(jnp.int32, (1, t), 1).astype(F32)
    top = float(t - 1)

    @pl.when(i == 0)
    def _():
        for n in range(n_blocks):
            vt = v_ref[n * t:(n + 1) * t, :].astype(F32).T
            for hh in heads:
                key_w = jnp.exp2(slopes[hh] * key_pos)
                vt_ref[n, hh * vt_rows:hh * vt_rows + HEAD_DIM] = (_head_rows(vt, hh) * key_w).astype(BF16)
                vt_ref[n, hh * vt_rows + HEAD_DIM:(hh + 1) * vt_rows] = jnp.broadcast_to(
                    key_w, (vt_rows - HEAD_DIM, t)).astype(BF16)
        kmean_ref[...] = jnp.zeros_like(kmean_ref)
        for n in range(n_blocks):
            kb = k_ref[n * t:(n + 1) * t, :].astype(F32)
            kmean_ref[n:n + 1, :] = jnp.mean(kb, axis=0, keepdims=True)

    blk_id = lax.broadcasted_iota(jnp.int32, (nb_pad, t), 0)
    past = blk_id < i
    q_raw = q_ref[...]
    qh_t = _transposed_heads(q_raw, 1.0)
    km_hi, km_lo = _split_bf16(kmean_ref[...])

    def store_logits(slot, tile):
        k = k_ref[pl.ds(pl.multiple_of(tile * t, t), t), :]
        for hh in heads:
            s_ref[slot, hh] = _head_dot(k, qh_t[hh])

    def accumulate(slot, tile, alpha):
        vt = vt_ref[tile]
        for hh in heads:
            acc_ref[hh] = alpha[hh] * acc_ref[hh] + _dot(vt[hh * vt_rows:(hh + 1) * vt_rows, :], p_ref[slot, hh])

    store_logits(0, 0)
    k_own = k_ref[pl.ds(pl.multiple_of(i * t, t), t), :]
    s_own = [_head_dot(k_own, qh_t[hh]) for hh in heads]
    gates = [_head_dot(km_hi, qh_t[hh]) + _head_dot(km_lo, qh_t[hh]) for hh in heads]
    m = []
    for hh in heads:
        acc_ref[hh] = jnp.zeros((vt_rows, t), F32)
        gate = jnp.where(past, gates[hh], -jnp.inf)
        rank = jnp.zeros((nb_pad, t), jnp.int32)
        for blk in range(n_blocks):
            gb = gate[blk:blk + 1, :]
            ahead = (gb > gate) | ((gb == gate) & (blk_id > blk))
            rank += ahead.astype(jnp.int32)
        chosen_ref[hh] = jnp.where(past & (rank < MOBA_TOPK), 1.0, 0.0)

        s = jnp.where(key <= qry, s_own[hh], NEG_BIG)
        m0 = jnp.max(s, axis=0, keepdims=True) + slopes[hh] * top
        p_ref[1, hh] = jnp.exp2(s - m0).astype(BF16)
        m.append(m0)
    ones = jnp.ones((1, t), F32)

    def step(slot, n, carry):
        alpha_prev, m_old, prev_tile = carry
        accumulate(1 - slot, prev_tile, alpha_prev)
        store_logits(1 - slot, jnp.minimum(n + 1, i))
        offset = ((i - n) * t).astype(F32)
        alpha, m_new = [], []
        for hh in heads:
            picked = chosen_ref[hh, pl.ds(n, 1), :]
            shift = jnp.where(picked > 0.0, -slopes[hh] * offset, NEG_BIG)
            s = s_ref[slot, hh]
            bound = jnp.max(s, axis=0, keepdims=True) + (shift + slopes[hh] * top)
            m_new.append(jnp.maximum(m_old[hh], bound))
            alpha.append(jnp.exp2(m_old[hh] - m_new[hh]))
            p_ref[slot, hh] = jnp.exp2(s - (m_new[hh] - shift)).astype(BF16)
        return tuple(alpha), tuple(m_new), n

    def pair(j, carry):
        return step(1, 2 * j + 1, step(0, 2 * j, carry))

    init = (tuple(ones for _ in heads), tuple(m), i)
    alpha_last, _, last_tile = lax.fori_loop(0, (i + 1) // 2, pair, init)
    accumulate(1, last_tile, alpha_last)
    _store_heads(o_ref, [acc_ref[hh, :HEAD_DIM] / acc_ref[hh, HEAD_DIM:HEAD_DIM + 1] for hh in heads])


def _moba_attention(qkv, slopes, bsz, seq):
    t = MOBA_BLOCK
    n_blocks = seq // t
    nb_pad = -(-n_blocks // BF16_SUBLANES) * BF16_SUBLANES
    vt_rows = HEAD_DIM + BF16_SUBLANES
    assert seq % t == 0
    cb = lambda off: off // ATTN_LANES
    return pl.pallas_call(
        functools.partial(_moba_kernel, n_blocks=n_blocks),
        grid=(bsz, N_HEADS // ATTN_HEADS, n_blocks),
        in_specs=[
            pl.BlockSpec(memory_space=pltpu.SMEM),
            pl.BlockSpec((None, t, ATTN_LANES), lambda b, hg, i: (b, i, cb(COL_MOBA_Q) + hg)),
            pl.BlockSpec((None, seq, ATTN_LANES), lambda b, hg, i: (b, 0, cb(COL_MOBA_K) + hg)),
            pl.BlockSpec((None, seq, ATTN_LANES), lambda b, hg, i: (b, 0, cb(COL_MOBA_V) + hg)),
        ],
        out_specs=pl.BlockSpec((None, t, ATTN_LANES), lambda b, hg, i: (b, i, hg)),
        out_shape=jax.ShapeDtypeStruct((bsz, seq, BRANCH_W), BF16),
        scratch_shapes=[pltpu.VMEM((nb_pad, ATTN_LANES), F32),
                        pltpu.VMEM((n_blocks, ATTN_HEADS * vt_rows, t), BF16),
                        pltpu.VMEM((ATTN_HEADS, nb_pad, t), F32),
                        pltpu.VMEM((ATTN_HEADS, vt_rows, t), F32),
                        pltpu.VMEM((2, ATTN_HEADS, t, t), F32),
                        pltpu.VMEM((2, ATTN_HEADS, t, t), BF16)],
        compiler_params=_params("parallel", "parallel", "arbitrary"),
        name="moba",
    )(slopes, qkv, qkv, qkv)


def _swa_kernel(slope_ref, sink_ref, q_ref, k_ref, v_ref, o_ref, vt_ref, *, q_blocks):
    w = SWA_WINDOW
    tile = pl.program_id(1)
    group = N_HEADS // SWA_KV_HEADS
    gw = group * w

    @pl.when(tile == 0)
    def _():
        _store_transposed_values(v_ref, vt_ref, w)

    key = lax.broadcasted_iota(jnp.int32, (w, gw), 0)
    qry = lax.broadcasted_iota(jnp.int32, (w, gw), 1) % w
    dist = (qry - key).astype(F32)
    ok_cur = key <= qry
    in_window = key > qry
    no_dims = jnp.zeros((HEAD_DIM, w), F32)
    units = [(qb, kv) for qb in range(q_blocks) for kv in range(SWA_KV_HEADS)]

    blk, s_cur, s_prev = {}, {}, {}
    for qb in range(q_blocks):
        blk[qb] = tile * q_blocks + qb
        prev = jnp.maximum(blk[qb] - 1, 0)
        k_cur = k_ref[pl.ds(pl.multiple_of(blk[qb] * w, w), w), :]
        k_prev = k_ref[pl.ds(pl.multiple_of(prev * w, w), w), :]
        qt = q_ref[qb * w:(qb + 1) * w, :].astype(F32).T * (HEAD_DIM ** -0.5)
        for kv in range(SWA_KV_HEADS):
            cols = []
            for j in range(group):
                h = kv * group + j
                qh = qt[h * HEAD_DIM:(h + 1) * HEAD_DIM, :]
                cols.append(jnp.concatenate([qh, no_dims] if kv == 0 else [no_dims, qh], axis=0))
            qg = jnp.concatenate(cols, axis=1).astype(BF16)
            s_cur[qb, kv] = _dot(k_cur, qg)
            s_prev[qb, kv] = _dot(k_prev, qg)

    e_cur, e_prev, denom = {}, {}, {}
    for u in units:
        qb, kv = u
        slope, sink = slope_ref[kv], sink_ref[kv]
        bias = slope * dist
        sc = jnp.where(ok_cur, s_cur[u] - bias, NEG_BIG)
        sp = jnp.where(in_window & (blk[qb] > 0), s_prev[u] - bias - slope * float(w), NEG_BIG)
        m = jnp.maximum(jnp.maximum(jnp.max(sc, axis=0, keepdims=True), jnp.max(sp, axis=0, keepdims=True)), sink)
        ec, ep = jnp.exp(sc - m), jnp.exp(sp - m)
        denom[u] = jnp.sum(ec, axis=0, keepdims=True) + jnp.sum(ep, axis=0, keepdims=True) + jnp.exp(sink - m)
        e_cur[u], e_prev[u] = ec.astype(BF16), ep.astype(BF16)

    out_t = {}
    for u in units:
        qb, kv = u
        vt_cur = _head_rows(vt_ref[blk[qb]], kv)
        vt_prev = _head_rows(vt_ref[jnp.maximum(blk[qb] - 1, 0)], kv)
        out_t[u] = (_dot(vt_cur, e_cur[u]) + _dot(vt_prev, e_prev[u])) / denom[u]
    for qb in range(q_blocks):
        heads_t = [out_t[qb, kv][:, j * w:(j + 1) * w] for kv in range(SWA_KV_HEADS) for j in range(group)]
        o_ref[qb * w:(qb + 1) * w, :] = jnp.concatenate(heads_t, axis=0).T.astype(o_ref.dtype)


def _swa_attention(qkv, slopes, sinks, bsz, seq):
    w = SWA_WINDOW
    q_blocks = 2
    tq = q_blocks * w
    qw = N_HEADS * HEAD_DIM
    assert SWA_KV_HEADS * HEAD_DIM == LANES
    per_col = lambda v: jnp.repeat(v.reshape(SWA_KV_HEADS, N_HEADS // SWA_KV_HEADS), w, axis=1)[:, None, :]
    const_spec = pl.BlockSpec((SWA_KV_HEADS, 1, (N_HEADS // SWA_KV_HEADS) * w), lambda b, i: (0, 0, 0))
    return pl.pallas_call(
        functools.partial(_swa_kernel, q_blocks=q_blocks),
        grid=(bsz, seq // tq),
        in_specs=[
            const_spec,
            const_spec,
            pl.BlockSpec((None, tq, qw), lambda b, i: (b, i, COL_SWA_Q // qw)),
            pl.BlockSpec((None, seq, LANES), lambda b, i: (b, 0, COL_SWA_K // LANES)),
            pl.BlockSpec((None, seq, LANES), lambda b, i: (b, 0, COL_SWA_V // LANES)),
        ],
        out_specs=pl.BlockSpec((None, tq, qw), lambda b, i: (b, i, 0)),
        out_shape=jax.ShapeDtypeStruct((bsz, seq, BRANCH_W), BF16),
        scratch_shapes=[pltpu.VMEM((seq // w, LANES, w), BF16)],
        compiler_params=_params("parallel", "arbitrary"),
        name="swa",
    )(per_col(slopes), per_col(sinks), qkv, qkv, qkv)


def _conv_kernel(w_ref, b_ref, c_ref, h_ref, o_ref):
    u = c_ref[...].astype(F32) * h_ref[...].astype(F32)
    pos = lax.broadcasted_iota(jnp.int32, u.shape, 0)
    y = w_ref[CONV_K - 1:CONV_K, :] * u
    for back in range(1, CONV_K):
        shifted = jnp.where(pos >= back, pltpu.roll(u, back, 0), 0.0)
        y += w_ref[CONV_K - 1 - back:CONV_K - back, :] * shifted
    o_ref[...] = (b_ref[...].astype(F32) * y).astype(o_ref.dtype)


def _gated_conv(qkv, conv_w, bsz, seq):
    ch = conv_w.shape[1]
    cb = lambda off: off // LANES
    return pl.pallas_call(
        _conv_kernel,
        grid=(bsz, ch // LANES),
        in_specs=[
            pl.BlockSpec((CONV_K, LANES), lambda b, c: (0, c)),
            pl.BlockSpec((None, seq, LANES), lambda b, c: (b, 0, cb(COL_CONV_B) + c)),
            pl.BlockSpec((None, seq, LANES), lambda b, c: (b, 0, cb(COL_CONV_C) + c)),
            pl.BlockSpec((None, seq, LANES), lambda b, c: (b, 0, cb(COL_CONV_H) + c)),
        ],
        out_specs=pl.BlockSpec((None, seq, LANES), lambda b, c: (b, 0, c)),
        out_shape=jax.ShapeDtypeStruct((bsz, seq, ch), BF16),
        compiler_params=_params("parallel", "parallel"),
        name="gated_conv",
    )(conv_w, qkv, qkv, qkv)


def _merge_kernel(x_ref, g_ref, *refs):
    y_refs = refs[:N_BRANCH]
    wb_ref, wo_ref, o_ref, xn_ref = refs[-4:]
    wg_refs = refs[N_BRANCH:-4]
    parts = len(wg_refs) // N_BRANCH
    j = pl.program_id(1)

    def mixed(xn, rows):
        out = None
        ys = [y_ref[rows, :] for y_ref in y_refs]
        for p in range(parts):
            cols = slice(p * GATE_BLOCK, (p + 1) * GATE_BLOCK)
            merged = None
            for n in range(N_BRANCH):
                gate = jax.nn.sigmoid(_dot(xn, wg_refs[n * parts + p][0]))
                term = gate * _dot(ys[n], wb_ref[n, :, cols])
                merged = term if merged is None else merged + term
            part = _dot(merged.astype(BF16), wo_ref[cols, :])
            out = part if out is None else out + part
        return out

    @pl.when(j == 0)
    def _():
        x = x_ref[...]
        xn_ref[...] = _rms_norm(x, g_ref[...]).astype(BF16)
        o_ref[...] = x

    o_ref[...] += mixed(xn_ref[...], slice(None))


def _merge(x, g, ys, w_in, w_branch, w_out, layer):
    n, d = x.shape
    tm, tn = min(MERGE_TOKEN_TILE, n), MERGE_COL_TILE
    parts = tn // GATE_BLOCK
    assert QKV_COLS % LANES == 0 and tn % GATE_BLOCK == 0
    y_spec = pl.BlockSpec((tm, BRANCH_W), lambda i, j: (i, 0))

    def gate_spec(branch, part):
        first = QKV_COLS + branch * d + part * GATE_BLOCK
        return pl.BlockSpec((pl.Element(1), pl.Element(d), pl.Element(GATE_BLOCK)),
                            lambda i, j: (layer, 0, pl.multiple_of(first + j * tn, LANES)))

    gate_specs = [gate_spec(br, p) for br in range(N_BRANCH) for p in range(parts)]
    return pl.pallas_call(
        _merge_kernel,
        grid=(n // tm, d // tn),
        in_specs=[
            pl.BlockSpec((tm, d), lambda i, j: (i, 0)),
            pl.BlockSpec((1, d), lambda i, j: (0, 0)),
            y_spec, y_spec, y_spec, y_spec,
            *gate_specs,
            pl.BlockSpec((None, N_BRANCH, BRANCH_W, tn), lambda i, j: (layer, 0, 0, j)),
            pl.BlockSpec((None, tn, d), lambda i, j: (layer, j, 0)),
        ],
        out_specs=pl.BlockSpec((tm, d), lambda i, j: (i, 0)),
        out_shape=jax.ShapeDtypeStruct((n, d), F32),
        scratch_shapes=[pltpu.VMEM((tm, d), BF16)],
        compiler_params=_params("parallel", "arbitrary"),
        name="merge",
    )(x, g, *ys, *([w_in] * len(gate_specs)), w_branch, w_out)


def _alibi_slopes(n):
    return 2.0 ** (-8.0 * jnp.arange(1, n + 1, dtype=F32) / n)


def _mixer(h, bsz, seq, norm_g, w_in, conv_w, sinks, w_branch, w_out, layer):
    slopes = _alibi_slopes(2 * N_HEADS)
    qkv = _qkv_proj(h, norm_g, w_in, layer).reshape(bsz, seq, QKV_COLS)
    y_a = _moba_attention(qkv, slopes[N_HEADS:] * LOG2_E, bsz, seq)
    y_b = _sb_attention(qkv, bsz, seq)
    y_c = _swa_attention(qkv, slopes[:N_HEADS], sinks.astype(F32), bsz, seq)
    y_d = _gated_conv(qkv, conv_w.reshape(CONV_K, -1), bsz, seq)
    ys = [y.reshape(bsz * seq, BRANCH_W) for y in (y_a, y_b, y_c, y_d)]
    return _merge(h, norm_g, ys, w_in, w_branch, w_out, layer)


def kernel(x, ffn1_norm, ffn1_w1, ffn1_w3, ffn1_w2, mix_norm, w_in, conv_w, attn_sinks, w_branch, w_out,
           ffn2_norm, ffn2_w1, ffn2_w3, ffn2_w2, final_norm):
    bsz, seq, d = x.shape
    depth = w_in.shape[0]
    h = x.reshape(bsz * seq, d)
    row = lambda v: v.reshape(1, d)
    final_g = row(final_norm)
    ffn1 = [w.astype(BF16) for w in (ffn1_w1, ffn1_w3, ffn1_w2)]
    ffn2 = [w.astype(BF16) for w in (ffn2_w1, ffn2_w3, ffn2_w2)]
    w_in, w_branch, w_out = (w.astype(BF16) for w in (w_in, w_branch, w_out))
    for l in range(depth):
        h = _ffn(h, row(ffn1_norm[l]), *ffn1, final_g, layer=l, final_norm=False)
        h = _mixer(h, bsz, seq, row(mix_norm[l]), w_in, conv_w[l], attn_sinks[l], w_branch, w_out, l)
        h = _ffn(h, row(ffn2_norm[l]), *ffn2, final_g, layer=l, final_norm=(l == depth - 1))
    return h.reshape(bsz, seq, d)
```

```python
import functools

import jax
import jax.numpy as jnp
from jax import lax
from jax.experimental import pallas as pl
from jax.experimental.pallas import tpu as pltpu

F32 = jnp.float32
BF16 = jnp.bfloat16

EPS = 1e-6
HEAD_DIM = 64
LANES = 128
BF16_SUBLANES = 16
MXU_DEPTH = 256
ATTN_HEADS = 8
ATTN_LANES = ATTN_HEADS * HEAD_DIM
SB_HEADS = 8
SB_LANES = SB_HEADS * HEAD_DIM
MOBA_BLOCK = 256
MOBA_TOPK = 3
LOG2_E = 1.4426950408889634
BASE2_Q_SCALE = LOG2_E * HEAD_DIM ** -0.5
SWA_WINDOW = 128
CONV_K = 3
N_BRANCH = 4
BRANCH_W = 512
N_HEADS = 8
SWA_KV_HEADS = 2
SB_TILE = 256
SB_LOG2_FLOOR = 110.0 * LOG2_E

COL_MOBA_Q, COL_MOBA_K, COL_MOBA_V = 0, 512, 1024
COL_SB_Q, COL_SB_K, COL_SB_V = 1536, 2048, 2560
COL_SWA_Q, COL_SWA_K, COL_SWA_V = 3072, 3584, 3712
COL_CONV_B, COL_CONV_C, COL_CONV_H = 3840, 4352, 4864
QKV_COLS = 5376

V7X_VMEM_BYTES = 64 * 1024 * 1024
VMEM_LIMIT = V7X_VMEM_BYTES - 8 * 1024 * 1024
NEG_BIG = -1e30

NORM_CHUNKS = 4
FFN_TOKEN_TILE = 1024
FF_TILE = 512
QKV_TOKEN_TILE = 1024
QKV_COL_TILE = 1792
MERGE_TOKEN_TILE = 512
MERGE_COL_TILE = 512
GATE_BLOCK = 512


def _params(*semantics):
    return pltpu.CompilerParams(dimension_semantics=semantics, vmem_limit_bytes=VMEM_LIMIT)


def _rms_norm(x, g):
    ms = jnp.mean(x * x, axis=-1, keepdims=True)
    return x * lax.rsqrt(ms + EPS) * g


def _dot(a, b):
    return jnp.dot(a, b, preferred_element_type=F32)


def _row_chunks(rows):
    chunk = rows // NORM_CHUNKS if rows % NORM_CHUNKS == 0 else rows
    return [pl.ds(start, chunk) for start in range(0, rows, chunk)]


def _split_bf16(x):
    hi = x.astype(BF16)
    lo = (x - hi.astype(F32)).astype(BF16)
    return hi, lo


def _ffn_kernel(x_ref, g_ref, w1_ref, w3_ref, w2_ref, fg_ref, o_ref, xn_ref, *, final_norm):
    f = pl.program_id(1)

    def half_swiglu(xn):
        a = _dot(xn, w1_ref[...])
        b = _dot(xn, w3_ref[...])
        hidden = (a * jax.nn.sigmoid(a) * b).astype(BF16)
        return 0.5 * _dot(hidden, w2_ref[...])

    @pl.when(f == 0)
    def _():
        x = x_ref[...]
        xn_ref[...] = _rms_norm(x, g_ref[...]).astype(BF16)
        o_ref[...] = x

    o_ref[...] += half_swiglu(xn_ref[...])

    if final_norm:
        @pl.when(f == pl.num_programs(1) - 1)
        def _():
            o_ref[...] = _rms_norm(o_ref[...], fg_ref[...])


def _ffn(x, g, w1, w3, w2, final_g, *, layer, final_norm):
    n, d = x.shape
    d_ff = w1.shape[2]
    tm, tf = min(FFN_TOKEN_TILE, n), FF_TILE
    return pl.pallas_call(
        functools.partial(_ffn_kernel, final_norm=final_norm),
        grid=(n // tm, d_ff // tf),
        in_specs=[
            pl.BlockSpec((tm, d), lambda i, f: (i, 0)),
            pl.BlockSpec((1, d), lambda i, f: (0, 0)),
            pl.BlockSpec((None, d, tf), lambda i, f: (layer, 0, f)),
            pl.BlockSpec((None, d, tf), lambda i, f: (layer, 0, f)),
            pl.BlockSpec((None, tf, d), lambda i, f: (layer, f, 0)),
            pl.BlockSpec((1, d), lambda i, f: (0, 0)),
        ],
        out_specs=pl.BlockSpec((tm, d), lambda i, f: (i, 0)),
        out_shape=jax.ShapeDtypeStruct((n, d), F32),
        scratch_shapes=[pltpu.VMEM((tm, d), BF16)],
        compiler_params=_params("parallel", "arbitrary"),
        name="ffn",
    )(x, g, w1, w3, w2, final_g)


def _qkv_kernel(x_ref, g_ref, w_ref, cs_ref, o_ref, xn_ref):
    j = pl.program_id(1)

    @pl.when(j == 0)
    def _():
        for rows in _row_chunks(x_ref.shape[0]):
            xn = _rms_norm(x_ref[rows, :], g_ref[...]).astype(BF16)
            xn_ref[rows, :] = xn
            o_ref[rows, :] = (_dot(xn, w_ref[...]) * cs_ref[...]).astype(BF16)

    @pl.when(j > 0)
    def _():
        o_ref[...] = (_dot(xn_ref[...], w_ref[...]) * cs_ref[...]).astype(BF16)


def _qkv_proj(x, g, w_in, layer):
    n, d = x.shape
    tm, tn = min(QKV_TOKEN_TILE, n), QKV_COL_TILE
    assert QKV_COLS % tn == 0
    col = jnp.arange(QKV_COLS)
    is_moba_q = (col >= COL_MOBA_Q) & (col < COL_MOBA_K)
    is_sb_q = (col >= COL_SB_Q) & (col < COL_SB_K)
    is_swa_q = (col >= COL_SWA_Q) & (col < COL_SWA_K)
    col_scale = jnp.where(is_moba_q | is_sb_q | is_swa_q, BASE2_Q_SCALE, 1.0).astype(F32).reshape(1, QKV_COLS)
    return pl.pallas_call(
        _qkv_kernel,
        grid=(n // tm, QKV_COLS // tn),
        in_specs=[
            pl.BlockSpec((tm, d), lambda i, j: (i, 0)),
            pl.BlockSpec((1, d), lambda i, j: (0, 0)),
            pl.BlockSpec((None, d, tn), lambda i, j: (layer, 0, j)),
            pl.BlockSpec((1, tn), lambda i, j: (0, j)),
        ],
        out_specs=pl.BlockSpec((tm, tn), lambda i, j: (i, j)),
        out_shape=jax.ShapeDtypeStruct((n, QKV_COLS), BF16),
        scratch_shapes=[pltpu.VMEM((tm, d), BF16)],
        compiler_params=_params("parallel", "arbitrary"),
        name="qkv_proj",
    )(x, g, w_in, col_scale)


def _store_transposed_values(v_ref, vt_ref, tile):
    for n in range(vt_ref.shape[0]):
        vt_ref[n] = v_ref[n * tile:(n + 1) * tile, :].astype(F32).T.astype(BF16)


def _head_rows(vt, head_in_block):
    return vt[head_in_block * HEAD_DIM:(head_in_block + 1) * HEAD_DIM, :]


def _transposed_heads(q, scale):
    qt = q.astype(F32).T * scale
    out = []
    for hh in range(q.shape[1] // HEAD_DIM):
        lo = (hh * HEAD_DIM // MXU_DEPTH) * MXU_DEPTH
        lanes = slice(lo, min(lo + MXU_DEPTH, qt.shape[0]))
        slab = qt[lanes]
        row = lo + lax.broadcasted_iota(jnp.int32, slab.shape, 0)
        out.append((jnp.where((row // HEAD_DIM) == hh, slab, 0.0).astype(BF16), lanes))
    return out


def _head_dot(x, q_head):
    slab, lanes = q_head
    return _dot(x[:, lanes], slab)


def _store_heads(o_ref, outs_t):
    o_ref[...] = jnp.concatenate(outs_t, axis=0).T.astype(o_ref.dtype)


def _softplus_base2(z):
    return jnp.maximum(z, 0.0) + jnp.log(1.0 + jnp.exp2(-jnp.abs(z))) * LOG2_E


def _sb_tiles(qh_t, tiles, runs, later_ones, causal):
    chains = [(ti, hh) for ti in range(len(tiles)) for hh in range(len(qh_t))]
    z = {c: _head_dot(tiles[c[0]][0], qh_t[c[1]]) for c in chains}
    sp, first, parts = {}, {}, {}
    for c in chains:
        sp[c] = _softplus_base2(z[c])
        spm = jnp.where(causal, sp[c], 0.0) if tiles[c[0]][2] else sp[c]
        first[c] = spm[0:1, :]
        parts[c] = _split_bf16(spm)
    local = {c: _dot(later_ones, parts[c][0]) + _dot(later_ones, parts[c][1]) for c in chains}
    runs = list(runs)
    weights = {}
    for c in chains:
        hh = c[1]
        a = jnp.exp2(z[c] - sp[c] - (local[c] + runs[hh]))
        runs[hh] = runs[hh] + (local[c][0:1, :] + first[c])
        if tiles[c[0]][2]:
            a = jnp.where(causal, a, 0.0)
        weights[c] = a.astype(BF16)
    return {c: _dot(_head_rows(tiles[c[0]][1], c[1]), weights[c]) for c in chains}, runs


def _sb_kernel(q_ref, k_ref, v_ref, o_ref, vt_ref, acc_ref):
    t = SB_TILE
    i = pl.program_id(2)

    @pl.when(i == 0)
    def _():
        _store_transposed_values(v_ref, vt_ref, t)

    key = lax.broadcasted_iota(jnp.int32, (t, t), 0)
    qry = lax.broadcasted_iota(jnp.int32, (t, t), 1)
    later_ones = (qry > key).astype(BF16)
    causal = key < qry
    qh_t = _transposed_heads(q_ref[...], 1.0)
    heads = range(len(qh_t))
    prev = jnp.maximum(i - 1, 0)
    has_prev = jnp.where(i > 0, 1.0, 0.0)
    k_diag, vt_diag = k_ref[pl.ds(pl.multiple_of(i * t, t), t), :], vt_ref[i]
    k_prev, vt_prev = k_ref[pl.ds(pl.multiple_of(prev * t, t), t), :], vt_ref[prev]
    zeros = jnp.zeros((1, t), F32)
    pv, runs = _sb_tiles(qh_t, [(k_diag, vt_diag, True), (k_prev, vt_prev, False)], [zeros for _ in heads],
                         later_ones, causal)
    for hh in heads:
        acc_ref[hh] = pv[0, hh] + has_prev * pv[1, hh]

    def still_visible(runs):
        return jnp.min(functools.reduce(jnp.minimum, runs)) < SB_LOG2_FLOOR

    def cond(carry):
        j, _, go = carry
        return jnp.logical_and(j <= i, go)

    def body(carry):
        j, runs, _ = carry
        blk = i - j
        k, vt = k_ref[pl.ds(pl.multiple_of(blk * t, t), t), :], vt_ref[blk]
        pv, new_runs = _sb_tiles(qh_t, [(k, vt, False)], runs, later_ones, causal)
        for hh in heads:
            acc_ref[hh] += pv[0, hh]
        return j + 1, tuple(new_runs), still_visible(new_runs)

    lax.while_loop(cond, body, (jnp.int32(2), tuple(runs), still_visible(runs)))
    _store_heads(o_ref, [acc_ref[hh] for hh in heads])


def _sb_attention(qkv, bsz, seq):
    t = SB_TILE
    cb = lambda off: off // SB_LANES
    return pl.pallas_call(
        _sb_kernel,
        grid=(bsz, N_HEADS // SB_HEADS, seq // t),
        in_specs=[
            pl.BlockSpec((None, t, SB_LANES), lambda b, hg, i: (b, i, cb(COL_SB_Q) + hg)),
            pl.BlockSpec((None, seq, SB_LANES), lambda b, hg, i: (b, 0, cb(COL_SB_K) + hg)),
            pl.BlockSpec((None, seq, SB_LANES), lambda b, hg, i: (b, 0, cb(COL_SB_V) + hg)),
        ],
        out_specs=pl.BlockSpec((None, t, SB_LANES), lambda b, hg, i: (b, i, hg)),
        out_shape=jax.ShapeDtypeStruct((bsz, seq, BRANCH_W), BF16),
        scratch_shapes=[pltpu.VMEM((seq // t, SB_LANES, t), BF16),
                        pltpu.VMEM((SB_HEADS, HEAD_DIM, t), F32)],
        compiler_params=_params("parallel", "parallel", "arbitrary"),
        name="stick_breaking",
    )(qkv, qkv, qkv)


def _moba_kernel(slopes_ref, q_ref, k_ref, v_ref, o_ref, kmean_ref, vt_ref, chosen_ref, acc_ref, s_ref, p_ref,
                 *, n_blocks):
    t = MOBA_BLOCK
    head_group = pl.program_id(1)
    i = pl.program_id(2)
    nb_pad = kmean_ref.shape[0]
    vt_rows = acc_ref.shape[1]

    heads = range(ATTN_HEADS)
    slopes = [slopes_ref[head_group * ATTN_HEADS + hh] for hh in heads]
    key = lax.broadcasted_iota(jnp.int32, (t, t), 0)
    qry = lax.broadcasted_iota(jnp.int32, (t, t), 1)

    key_pos = lax.broadcasted_iota(jnp.int32, (1, t), 1).astype(F32)
    top = float(t - 1)

    @pl.when(i == 0)
    def _():
        for n in range(n_blocks):
            vt = v_ref[n * t:(n + 1) * t, :].astype(F32).T
            for hh in heads:
                key_w = jnp.exp2(slopes[hh] * key_pos)
                vt_ref[n, hh * vt_rows:hh * vt_rows + HEAD_DIM] = (_head_rows(vt, hh) * key_w).astype(BF16)
                vt_ref[n, hh * vt_rows + HEAD_DIM:(hh + 1) * vt_rows] = jnp.broadcast_to(
                    key_w, (vt_rows - HEAD_DIM, t)).astype(BF16)
        kmean_ref[...] = jnp.zeros_like(kmean_ref)
        for n in range(n_blocks):
            kb = k_ref[n * t:(n + 1) * t, :].astype(F32)
            kmean_ref[n:n + 1, :] = jnp.mean(kb, axis=0, keepdims=True)

    blk_id = lax.broadcasted_iota(jnp.int32, (nb_pad, t), 0)
    past = blk_id < i
    q_raw = q_ref[...]
    qh_t = _transposed_heads(q_raw, 1.0)
    km_hi, km_lo = _split_bf16(kmean_ref[...])

    def store_logits(slot, tile):
        k = k_ref[pl.ds(pl.multiple_of(tile * t, t), t), :]
        for hh in heads:
            s_ref[slot, hh] = _head_dot(k, qh_t[hh])

    def accumulate(slot, tile, alpha):
        vt = vt_ref[tile]
        for hh in heads:
            acc_ref[hh] = alpha[hh] * acc_ref[hh] + _dot(vt[hh * vt_rows:(hh + 1) * vt_rows, :], p_ref[slot, hh])

    store_logits(0, 0)
    k_own = k_ref[pl.ds(pl.multiple_of(i * t, t), t), :]
    s_own = [_head_dot(k_own, qh_t[hh]) for hh in heads]
    gates = [_head_dot(km_hi, qh_t[hh]) + _head_dot(km_lo, qh_t[hh]) for hh in heads]
    m = []
    for hh in heads:
        acc_ref[hh] = jnp.zeros((vt_rows, t), F32)
        gate = jnp.where(past, gates[hh], -jnp.inf)
        rank = jnp.zeros((nb_pad, t), jnp.int32)
        for blk in range(n_blocks):
            gb = gate[blk:blk + 1, :]
            ahead = (gb > gate) | ((gb == gate) & (blk_id > blk))
            rank += ahead.astype(jnp.int32)
        chosen_ref[hh] = jnp.where(past & (rank < MOBA_TOPK), 1.0, 0.0)

        s = jnp.where(key <= qry, s_own[hh], NEG_BIG)
        m0 = jnp.max(s, axis=0, keepdims=True) + slopes[hh] * top
        p_ref[1, hh] = jnp.exp2(s - m0).astype(BF16)
        m.append(m0)
    ones = jnp.ones((1, t), F32)

    def step(slot, n, carry):
        alpha_prev, m_old, prev_tile = carry
        accumulate(1 - slot, prev_tile, alpha_prev)
        store_logits(1 - slot, jnp.minimum(n + 1, i))
        offset = ((i - n) * t).astype(F32)
        alpha, m_new = [], []
        for hh in heads:
            picked = chosen_ref[hh, pl.ds(n, 1), :]
            shift = jnp.where(picked > 0.0, -slopes[hh] * offset, NEG_BIG)
            s = s_ref[slot, hh]
            bound = jnp.max(s, axis=0, keepdims=True) + (shift + slopes[hh] * top)
            m_new.append(jnp.maximum(m_old[hh], bound))
            alpha.append(jnp.exp2(m_old[hh] - m_new[hh]))
            p_ref[slot, hh] = jnp.exp2(s - (m_new[hh] - shift)).astype(BF16)
        return tuple(alpha), tuple(m_new), n

    def pair(j, carry):
        return step(1, 2 * j + 1, step(0, 2 * j, carry))

    init = (tuple(ones for _ in heads), tuple(m), i)
    alpha_last, _, last_tile = lax.fori_loop(0, (i + 1) // 2, pair, init)
    accumulate(1, last_tile, alpha_last)
    _store_heads(o_ref, [acc_ref[hh, :HEAD_DIM] / acc_ref[hh, HEAD_DIM:HEAD_DIM + 1] for hh in heads])


def _moba_attention(qkv, slopes, bsz, seq):
    t = MOBA_BLOCK
    n_blocks = seq // t
    nb_pad = -(-n_blocks // BF16_SUBLANES) * BF16_SUBLANES
    vt_rows = HEAD_DIM + BF16_SUBLANES
    assert seq % t == 0
    cb = lambda off: off // ATTN_LANES
    return pl.pallas_call(
        functools.partial(_moba_kernel, n_blocks=n_blocks),
        grid=(bsz, N_HEADS // ATTN_HEADS, n_blocks),
        in_specs=[
            pl.BlockSpec(memory_space=pltpu.SMEM),
            pl.BlockSpec((None, t, ATTN_LANES), lambda b, hg, i: (b, i, cb(COL_MOBA_Q) + hg)),
            pl.BlockSpec((None, seq, ATTN_LANES), lambda b, hg, i: (b, 0, cb(COL_MOBA_K) + hg)),
            pl.BlockSpec((None, seq, ATTN_LANES), lambda b, hg, i: (b, 0, cb(COL_MOBA_V) + hg)),
        ],
        out_specs=pl.BlockSpec((None, t, ATTN_LANES), lambda b, hg, i: (b, i, hg)),
        out_shape=jax.ShapeDtypeStruct((bsz, seq, BRANCH_W), BF16),
        scratch_shapes=[pltpu.VMEM((nb_pad, ATTN_LANES), F32),
                        pltpu.VMEM((n_blocks, ATTN_HEADS * vt_rows, t), BF16),
                        pltpu.VMEM((ATTN_HEADS, nb_pad, t), F32),
                        pltpu.VMEM((ATTN_HEADS, vt_rows, t), F32),
                        pltpu.VMEM((2, ATTN_HEADS, t, t), F32),
                        pltpu.VMEM((2, ATTN_HEADS, t, t), BF16)],
        compiler_params=_params("parallel", "parallel", "arbitrary"),
        name="moba",
    )(slopes, qkv, qkv, qkv)


def _swa_kernel(slope_ref, sink_ref, q_ref, k_ref, v_ref, o_ref, vt_ref, *, q_blocks):
    w = SWA_WINDOW
    tile = pl.program_id(1)
    group = N_HEADS // SWA_KV_HEADS
    gw = group * w

    @pl.when(tile == 0)
    def _():
        _store_transposed_values(v_ref, vt_ref, w)

    key = lax.broadcasted_iota(jnp.int32, (w, gw), 0)
    qry = lax.broadcasted_iota(jnp.int32, (w, gw), 1) % w
    dist = (qry - key).astype(F32)
    ok_cur = key <= qry
    in_window = key > qry
    no_dims = jnp.zeros((HEAD_DIM, w), F32)
    units = [(qb, kv) for qb in range(q_blocks) for kv in range(SWA_KV_HEADS)]

    blk, s_cur, s_prev = {}, {}, {}
    for qb in range(q_blocks):
        blk[qb] = tile * q_blocks + qb
        prev = jnp.maximum(blk[qb] - 1, 0)
        k_cur = k_ref[pl.ds(pl.multiple_of(blk[qb] * w, w), w), :]
        k_prev = k_ref[pl.ds(pl.multiple_of(prev * w, w), w), :]
        qt = q_ref[qb * w:(qb + 1) * w, :].astype(F32).T
        for kv in range(SWA_KV_HEADS):
            cols = []
            for j in range(group):
                h = kv * group + j
                qh = qt[h * HEAD_DIM:(h + 1) * HEAD_DIM, :]
                cols.append(jnp.concatenate([qh, no_dims] if kv == 0 else [no_dims, qh], axis=0))
            qg = jnp.concatenate(cols, axis=1).astype(BF16)
            s_cur[qb, kv] = _dot(k_cur, qg)
            s_prev[qb, kv] = _dot(k_prev, qg)

    e_cur, e_prev, denom = {}, {}, {}
    for u in units:
        qb, kv = u
        slope, sink = slope_ref[kv], sink_ref[kv]
        bias = slope * dist
        sc = jnp.where(ok_cur, s_cur[u] - bias, NEG_BIG)
        sp = jnp.where(in_window & (blk[qb] > 0), s_prev[u] - bias - slope * float(w), NEG_BIG)
        m = jnp.maximum(jnp.maximum(jnp.max(sc, axis=0, keepdims=True), jnp.max(sp, axis=0, keepdims=True)), sink)
        ec, ep = jnp.exp2(sc - m), jnp.exp2(sp - m)
        denom[u] = jnp.sum(ec, axis=0, keepdims=True) + jnp.sum(ep, axis=0, keepdims=True) + jnp.exp2(sink - m)
        e_cur[u], e_prev[u] = ec.astype(BF16), ep.astype(BF16)

    out_t = {}
    for u in units:
        qb, kv = u
        vt_cur = _head_rows(vt_ref[blk[qb]], kv)
        vt_prev = _head_rows(vt_ref[jnp.maximum(blk[qb] - 1, 0)], kv)
        out_t[u] = (_dot(vt_cur, e_cur[u]) + _dot(vt_prev, e_prev[u])) / denom[u]
    for qb in range(q_blocks):
        heads_t = [out_t[qb, kv][:, j * w:(j + 1) * w] for kv in range(SWA_KV_HEADS) for j in range(group)]
        o_ref[qb * w:(qb + 1) * w, :] = jnp.concatenate(heads_t, axis=0).T.astype(o_ref.dtype)


def _swa_attention(qkv, slopes, sinks, bsz, seq):
    w = SWA_WINDOW
    q_blocks = 4
    tq = q_blocks * w
    qw = N_HEADS * HEAD_DIM
    assert SWA_KV_HEADS * HEAD_DIM == LANES
    per_col = lambda v: jnp.repeat(v.reshape(SWA_KV_HEADS, N_HEADS // SWA_KV_HEADS), w, axis=1)[:, None, :]
    const_spec = pl.BlockSpec((SWA_KV_HEADS, 1, (N_HEADS // SWA_KV_HEADS) * w), lambda b, i: (0, 0, 0))
    return pl.pallas_call(
        functools.partial(_swa_kernel, q_blocks=q_blocks),
        grid=(bsz, seq // tq),
        in_specs=[
            const_spec,
            const_spec,
            pl.BlockSpec((None, tq, qw), lambda b, i: (b, i, COL_SWA_Q // qw)),
            pl.BlockSpec((None, seq, LANES), lambda b, i: (b, 0, COL_SWA_K // LANES)),
            pl.BlockSpec((None, seq, LANES), lambda b, i: (b, 0, COL_SWA_V // LANES)),
        ],
        out_specs=pl.BlockSpec((None, tq, qw), lambda b, i: (b, i, 0)),
        out_shape=jax.ShapeDtypeStruct((bsz, seq, BRANCH_W), BF16),
        scratch_shapes=[pltpu.VMEM((seq // w, LANES, w), BF16)],
        compiler_params=_params("parallel", "arbitrary"),
        name="swa",
    )(per_col(slopes * LOG2_E), per_col(sinks * LOG2_E), qkv, qkv, qkv)


def _conv_kernel(w_ref, b_ref, c_ref, h_ref, o_ref):
    u = c_ref[...].astype(F32) * h_ref[...].astype(F32)
    pos = lax.broadcasted_iota(jnp.int32, u.shape, 0)
    y = w_ref[CONV_K - 1:CONV_K, :] * u
    for back in range(1, CONV_K):
        shifted = jnp.where(pos >= back, pltpu.roll(u, back, 0), 0.0)
        y += w_ref[CONV_K - 1 - back:CONV_K - back, :] * shifted
    o_ref[...] = (b_ref[...].astype(F32) * y).astype(o_ref.dtype)


def _gated_conv(qkv, conv_w, bsz, seq):
    ch = conv_w.shape[1]
    cb = lambda off: off // LANES
    return pl.pallas_call(
        _conv_kernel,
        grid=(bsz, ch // LANES),
        in_specs=[
            pl.BlockSpec((CONV_K, LANES), lambda b, c: (0, c)),
            pl.BlockSpec((None, seq, LANES), lambda b, c: (b, 0, cb(COL_CONV_B) + c)),
            pl.BlockSpec((None, seq, LANES), lambda b, c: (b, 0, cb(COL_CONV_C) + c)),
            pl.BlockSpec((None, seq, LANES), lambda b, c: (b, 0, cb(COL_CONV_H) + c)),
        ],
        out_specs=pl.BlockSpec((None, seq, LANES), lambda b, c: (b, 0, c)),
        out_shape=jax.ShapeDtypeStruct((bsz, seq, ch), BF16),
        compiler_params=_params("parallel", "parallel"),
        name="gated_conv",
    )(conv_w, qkv, qkv, qkv)


def _merge_kernel(x_ref, g_ref, *refs):
    y_refs = refs[:N_BRANCH]
    wb_ref, wo_ref, o_ref, xn_ref = refs[-4:]
    wg_refs = refs[N_BRANCH:-4]
    parts = len(wg_refs) // N_BRANCH
    j = pl.program_id(1)

    def mixed(xn, rows):
        out = None
        ys = [y_ref[rows, :] for y_ref in y_refs]
        for p in range(parts):
            cols = slice(p * GATE_BLOCK, (p + 1) * GATE_BLOCK)
            merged = None
            for n in range(N_BRANCH):
                gate = jax.nn.sigmoid(_dot(xn, wg_refs[n * parts + p][0]))
                term = gate * _dot(ys[n], wb_ref[n, :, cols])
                merged = term if merged is None else merged + term
            part = _dot(merged.astype(BF16), wo_ref[cols, :])
            out = part if out is None else out + part
        return out

    @pl.when(j == 0)
    def _():
        x = x_ref[...]
        xn_ref[...] = _rms_norm(x, g_ref[...]).astype(BF16)
        o_ref[...] = x

    o_ref[...] += mixed(xn_ref[...], slice(None))


def _merge(x, g, ys, w_in, w_branch, w_out, layer):
    n, d = x.shape
    tm, tn = min(MERGE_TOKEN_TILE, n), MERGE_COL_TILE
    parts = tn // GATE_BLOCK
    assert QKV_COLS % LANES == 0 and tn % GATE_BLOCK == 0
    y_spec = pl.BlockSpec((tm, BRANCH_W), lambda i, j: (i, 0))

    def gate_spec(branch, part):
        first = QKV_COLS + branch * d + part * GATE_BLOCK
        return pl.BlockSpec((pl.Element(1), pl.Element(d), pl.Element(GATE_BLOCK)),
                            lambda i, j: (layer, 0, pl.multiple_of(first + j * tn, LANES)))

    gate_specs = [gate_spec(br, p) for br in range(N_BRANCH) for p in range(parts)]
    return pl.pallas_call(
        _merge_kernel,
        grid=(n // tm, d // tn),
        in_specs=[
            pl.BlockSpec((tm, d), lambda i, j: (i, 0)),
            pl.BlockSpec((1, d), lambda i, j: (0, 0)),
            y_spec, y_spec, y_spec, y_spec,
            *gate_specs,
            pl.BlockSpec((None, N_BRANCH, BRANCH_W, tn), lambda i, j: (layer, 0, 0, j)),
            pl.BlockSpec((None, tn, d), lambda i, j: (layer, j, 0)),
        ],
        out_specs=pl.BlockSpec((tm, d), lambda i, j: (i, 0)),
        out_shape=jax.ShapeDtypeStruct((n, d), F32),
        scratch_shapes=[pltpu.VMEM((tm, d), BF16)],
        compiler_params=_params("parallel", "arbitrary"),
        name="merge",
    )(x, g, *ys, *([w_in] * len(gate_specs)), w_branch, w_out)


def _alibi_slopes(n):
    return 2.0 ** (-8.0 * jnp.arange(1, n + 1, dtype=F32) / n)


def _mixer(h, bsz, seq, norm_g, w_in, conv_w, sinks, w_branch, w_out, layer):
    slopes = _alibi_slopes(2 * N_HEADS)
    qkv = _qkv_proj(h, norm_g, w_in, layer).reshape(bsz, seq, QKV_COLS)
    y_a = _moba_attention(qkv, slopes[N_HEADS:] * LOG2_E, bsz, seq)
    y_b = _sb_attention(qkv, bsz, seq)
    y_c = _swa_attention(qkv, slopes[:N_HEADS], sinks.astype(F32), bsz, seq)
    y_d = _gated_conv(qkv, conv_w.reshape(CONV_K, -1), bsz, seq)
    ys = [y.reshape(bsz * seq, BRANCH_W) for y in (y_a, y_b, y_c, y_d)]
    return _merge(h, norm_g, ys, w_in, w_branch, w_out, layer)


def kernel(x, ffn1_norm, ffn1_w1, ffn1_w3, ffn1_w2, mix_norm, w_in, conv_w, attn_sinks, w_branch, w_out,
           ffn2_norm, ffn2_w1, ffn2_w3, ffn2_w2, final_norm):
    bsz, seq, d = x.shape
    depth = w_in.shape[0]
    h = x.reshape(bsz * seq, d)
    row = lambda v: v.reshape(1, d)
    final_g = row(final_norm)
    ffn1 = [w.astype(BF16) for w in (ffn1_w1, ffn1_w3, ffn1_w2)]
    ffn2 = [w.astype(BF16) for w in (ffn2_w1, ffn2_w3, ffn2_w2)]
    w_in, w_branch, w_out = (w.astype(BF16) for w in (w_in, w_branch, w_out))
    for l in range(depth):
        h = _ffn(h, row(ffn1_norm[l]), *ffn1, final_g, layer=l, final_norm=False)
        h = _mixer(h, bsz, seq, row(mix_norm[l]), w_in, conv_w[l], attn_sinks[l], w_branch, w_out, l)
        h = _ffn(h, row(ffn2_norm[l]), *ffn2, final_g, layer=l, final_norm=(l == depth - 1))
    return h.reshape(bsz, seq, d)
```

```python
import functools

import jax
import jax.numpy as jnp
from jax import lax
from jax.experimental import pallas as pl
from jax.experimental.pallas import tpu as pltpu

F32 = jnp.float32
BF16 = jnp.bfloat16

EPS = 1e-6
HEAD_DIM = 64
LANES = 128
BF16_SUBLANES = 16
MXU_DEPTH = 256
ATTN_HEADS = 8
ATTN_LANES = ATTN_HEADS * HEAD_DIM
SB_HEADS = 8
SB_LANES = SB_HEADS * HEAD_DIM
MOBA_BLOCK = 256
MOBA_TOPK = 3
LOG2_E = 1.4426950408889634
BASE2_Q_SCALE = LOG2_E * HEAD_DIM ** -0.5
SWA_WINDOW = 128
CONV_K = 3
N_BRANCH = 4
BRANCH_W = 512
N_HEADS = 8
SWA_KV_HEADS = 2
SB_TILE = 256
SB_LOG2_FLOOR = 110.0 * LOG2_E

COL_MOBA_Q, COL_MOBA_K, COL_MOBA_V = 0, 512, 1024
COL_SB_Q, COL_SB_K, COL_SB_V = 1536, 2048, 2560
COL_SWA_Q, COL_SWA_K, COL_SWA_V = 3072, 3584, 3712
COL_CONV_B, COL_CONV_C, COL_CONV_H = 3840, 4352, 4864
QKV_COLS = 5376

V7X_VMEM_BYTES = 64 * 1024 * 1024
VMEM_LIMIT = V7X_VMEM_BYTES - 8 * 1024 * 1024
NEG_BIG = -1e30

NORM_CHUNKS = 4
FFN_TOKEN_TILE = 1024
FF_TILE = 512
QKV_TOKEN_TILE = 1024
QKV_COL_TILE = 1792
MERGE_TOKEN_TILE = 512
MERGE_COL_TILE = 512
GATE_BLOCK = 512


def _params(*semantics):
    return pltpu.CompilerParams(dimension_semantics=semantics, vmem_limit_bytes=VMEM_LIMIT)


def _rms_norm(x, g):
    ms = jnp.mean(x * x, axis=-1, keepdims=True)
    return x * lax.rsqrt(ms + EPS) * g


def _dot(a, b):
    return jnp.dot(a, b, preferred_element_type=F32)


def _row_chunks(rows):
    chunk = rows // NORM_CHUNKS if rows % NORM_CHUNKS == 0 else rows
    return [pl.ds(start, chunk) for start in range(0, rows, chunk)]


def _split_bf16(x):
    hi = x.astype(BF16)
    lo = (x - hi.astype(F32)).astype(BF16)
    return hi, lo


def _ffn_kernel(x_ref, g_ref, w1_ref, w3_ref, w2_ref, fg_ref, o_ref, xn_ref, *, final_norm):
    f = pl.program_id(1)

    def half_swiglu(xn):
        a = _dot(xn, w1_ref[...])
        b = _dot(xn, w3_ref[...])
        hidden = (a * jax.nn.sigmoid(a) * b).astype(BF16)
        return 0.5 * _dot(hidden, w2_ref[...])

    @pl.when(f == 0)
    def _():
        x = x_ref[...]
        xn_ref[...] = _rms_norm(x, g_ref[...]).astype(BF16)
        o_ref[...] = x

    o_ref[...] += half_swiglu(xn_ref[...])

    if final_norm:
        @pl.when(f == pl.num_programs(1) - 1)
        def _():
            o_ref[...] = _rms_norm(o_ref[...], fg_ref[...])


def _ffn(x, g, w1, w3, w2, final_g, *, layer, final_norm):
    n, d = x.shape
    d_ff = w1.shape[2]
    tm, tf = min(FFN_TOKEN_TILE, n), FF_TILE
    return pl.pallas_call(
        functools.partial(_ffn_kernel, final_norm=final_norm),
        grid=(n // tm, d_ff // tf),
        in_specs=[
            pl.BlockSpec((tm, d), lambda i, f: (i, 0)),
            pl.BlockSpec((1, d), lambda i, f: (0, 0)),
            pl.BlockSpec((None, d, tf), lambda i, f: (layer, 0, f)),
            pl.BlockSpec((None, d, tf), lambda i, f: (layer, 0, f)),
            pl.BlockSpec((None, tf, d), lambda i, f: (layer, f, 0)),
            pl.BlockSpec((1, d), lambda i, f: (0, 0)),
        ],
        out_specs=pl.BlockSpec((tm, d), lambda i, f: (i, 0)),
        out_shape=jax.ShapeDtypeStruct((n, d), F32),
        scratch_shapes=[pltpu.VMEM((tm, d), BF16)],
        compiler_params=_params("parallel", "arbitrary"),
        name="ffn",
    )(x, g, w1, w3, w2, final_g)


def _qkv_kernel(x_ref, g_ref, w_ref, cs_ref, o_ref, xn_ref):
    j = pl.program_id(1)

    @pl.when(j == 0)
    def _():
        for rows in _row_chunks(x_ref.shape[0]):
            xn = _rms_norm(x_ref[rows, :], g_ref[...]).astype(BF16)
            xn_ref[rows, :] = xn
            o_ref[rows, :] = (_dot(xn, w_ref[...]) * cs_ref[...]).astype(BF16)

    @pl.when(j > 0)
    def _():
        o_ref[...] = (_dot(xn_ref[...], w_ref[...]) * cs_ref[...]).astype(BF16)


def _qkv_proj(x, g, w_in, layer):
    n, d = x.shape
    tm, tn = min(QKV_TOKEN_TILE, n), QKV_COL_TILE
    assert QKV_COLS % tn == 0
    col = jnp.arange(QKV_COLS)
    is_moba_q = (col >= COL_MOBA_Q) & (col < COL_MOBA_K)
    is_sb_q = (col >= COL_SB_Q) & (col < COL_SB_K)
    is_swa_q = (col >= COL_SWA_Q) & (col < COL_SWA_K)
    col_scale = jnp.where(is_moba_q | is_sb_q | is_swa_q, BASE2_Q_SCALE, 1.0).astype(F32).reshape(1, QKV_COLS)
    return pl.pallas_call(
        _qkv_kernel,
        grid=(n // tm, QKV_COLS // tn),
        in_specs=[
            pl.BlockSpec((tm, d), lambda i, j: (i, 0)),
            pl.BlockSpec((1, d), lambda i, j: (0, 0)),
            pl.BlockSpec((None, d, tn), lambda i, j: (layer, 0, j)),
            pl.BlockSpec((1, tn), lambda i, j: (0, j)),
        ],
        out_specs=pl.BlockSpec((tm, tn), lambda i, j: (i, j)),
        out_shape=jax.ShapeDtypeStruct((n, QKV_COLS), BF16),
        scratch_shapes=[pltpu.VMEM((tm, d), BF16)],
        compiler_params=_params("parallel", "arbitrary"),
        name="qkv_proj",
    )(x, g, w_in, col_scale)


def _store_transposed_values(v_ref, vt_ref, tile):
    for n in range(vt_ref.shape[0]):
        vt_ref[n] = v_ref[n * tile:(n + 1) * tile, :].astype(F32).T.astype(BF16)


def _head_rows(vt, head_in_block):
    return vt[head_in_block * HEAD_DIM:(head_in_block + 1) * HEAD_DIM, :]


def _transposed_heads(q, scale):
    qt = q.astype(F32).T * scale
    out = []
    for hh in range(q.shape[1] // HEAD_DIM):
        lo = (hh * HEAD_DIM // MXU_DEPTH) * MXU_DEPTH
        lanes = slice(lo, min(lo + MXU_DEPTH, qt.shape[0]))
        slab = qt[lanes]
        row = lo + lax.broadcasted_iota(jnp.int32, slab.shape, 0)
        out.append((jnp.where((row // HEAD_DIM) == hh, slab, 0.0).astype(BF16), lanes))
    return out


def _head_dot(x, q_head):
    slab, lanes = q_head
    return _dot(x[:, lanes], slab)


def _store_heads(o_ref, outs_t):
    o_ref[...] = jnp.concatenate(outs_t, axis=0).T.astype(o_ref.dtype)


def _softplus_base2(z):
    return jnp.maximum(z, 0.0) + jnp.log(1.0 + jnp.exp2(-jnp.abs(z))) * LOG2_E


def _sb_tiles(qh_t, tiles, runs, later_ones, causal):
    chains = [(ti, hh) for ti in range(len(tiles)) for hh in range(len(qh_t))]
    z = {c: _head_dot(tiles[c[0]][0], qh_t[c[1]]) for c in chains}
    sp, first, parts = {}, {}, {}
    for c in chains:
        sp[c] = _softplus_base2(z[c])
        spm = jnp.where(causal, sp[c], 0.0) if tiles[c[0]][2] else sp[c]
        first[c] = spm[0:1, :]
        parts[c] = _split_bf16(spm)
    local = {c: _dot(later_ones, parts[c][0]) + _dot(later_ones, parts[c][1]) for c in chains}
    runs = list(runs)
    weights = {}
    for c in chains:
        hh = c[1]
        a = jnp.exp2(z[c] - sp[c] - (local[c] + runs[hh]))
        runs[hh] = runs[hh] + (local[c][0:1, :] + first[c])
        if tiles[c[0]][2]:
            a = jnp.where(causal, a, 0.0)
        weights[c] = a.astype(BF16)
    return {c: _dot(_head_rows(tiles[c[0]][1], c[1]), weights[c]) for c in chains}, runs


def _sb_kernel(q_ref, k_ref, v_ref, o_ref, vt_ref, acc_ref):
    t = SB_TILE
    i = pl.program_id(2)

    @pl.when(i == 0)
    def _():
        _store_transposed_values(v_ref, vt_ref, t)

    key = lax.broadcasted_iota(jnp.int32, (t, t), 0)
    qry = lax.broadcasted_iota(jnp.int32, (t, t), 1)
    later_ones = (qry > key).astype(BF16)
    causal = key < qry
    qh_t = _transposed_heads(q_ref[...], 1.0)
    heads = range(len(qh_t))
    prev = jnp.maximum(i - 1, 0)
    has_prev = jnp.where(i > 0, 1.0, 0.0)
    k_diag, vt_diag = k_ref[pl.ds(pl.multiple_of(i * t, t), t), :], vt_ref[i]
    k_prev, vt_prev = k_ref[pl.ds(pl.multiple_of(prev * t, t), t), :], vt_ref[prev]
    zeros = jnp.zeros((1, t), F32)
    pv, runs = _sb_tiles(qh_t, [(k_diag, vt_diag, True), (k_prev, vt_prev, False)], [zeros for _ in heads],
                         later_ones, causal)
    for hh in heads:
        acc_ref[hh] = pv[0, hh] + has_prev * pv[1, hh]

    def still_visible(runs):
        return jnp.min(functools.reduce(jnp.minimum, runs)) < SB_LOG2_FLOOR

    def cond(carry):
        j, _, go = carry
        return jnp.logical_and(j <= i, go)

    def body(carry):
        j, runs, _ = carry
        blk = i - j
        k, vt = k_ref[pl.ds(pl.multiple_of(blk * t, t), t), :], vt_ref[blk]
        pv, new_runs = _sb_tiles(qh_t, [(k, vt, False)], runs, later_ones, causal)
        for hh in heads:
            acc_ref[hh] += pv[0, hh]
        return j + 1, tuple(new_runs), still_visible(new_runs)

    lax.while_loop(cond, body, (jnp.int32(2), tuple(runs), still_visible(runs)))
    _store_heads(o_ref, [acc_ref[hh] for hh in heads])


def _sb_attention(qkv, bsz, seq):
    t = SB_TILE
    cb = lambda off: off // SB_LANES
    return pl.pallas_call(
        _sb_kernel,
        grid=(bsz, N_HEADS // SB_HEADS, seq // t),
        in_specs=[
            pl.BlockSpec((None, t, SB_LANES), lambda b, hg, i: (b, i, cb(COL_SB_Q) + hg)),
            pl.BlockSpec((None, seq, SB_LANES), lambda b, hg, i: (b, 0, cb(COL_SB_K) + hg)),
            pl.BlockSpec((None, seq, SB_LANES), lambda b, hg, i: (b, 0, cb(COL_SB_V) + hg)),
        ],
        out_specs=pl.BlockSpec((None, t, SB_LANES), lambda b, hg, i: (b, i, hg)),
        out_shape=jax.ShapeDtypeStruct((bsz, seq, BRANCH_W), BF16),
        scratch_shapes=[pltpu.VMEM((seq // t, SB_LANES, t), BF16),
                        pltpu.VMEM((SB_HEADS, HEAD_DIM, t), F32)],
        compiler_params=_params("parallel", "parallel", "arbitrary"),
        name="stick_breaking",
    )(qkv, qkv, qkv)


def _moba_kernel(slopes_ref, q_ref, k_ref, v_ref, o_ref, kmean_ref, vt_ref, chosen_ref, acc_ref, s_ref, p_ref,
                 *, n_blocks):
    t = MOBA_BLOCK
    head_group = pl.program_id(1)
    i = pl.program_id(2)
    nb_pad = kmean_ref.shape[0]
    vt_rows = acc_ref.shape[1]

    heads = range(ATTN_HEADS)
    slopes = [slopes_ref[head_group * ATTN_HEADS + hh] for hh in heads]
    key = lax.broadcasted_iota(jnp.int32, (t, t), 0)
    qry = lax.broadcasted_iota(jnp.int32, (t, t), 1)

    key_pos = lax.broadcasted_iota(jnp.int32, (1, t), 1).astype(F32)
    top = float(t - 1)

    @pl.when(i == 0)
    def _():
        for n in range(n_blocks):
            vt = v_ref[n * t:(n + 1) * t, :].astype(F32).T
            for hh in heads:
                key_w = jnp.exp2(slopes[hh] * key_pos)
                vt_ref[n, hh * vt_rows:hh * vt_rows + HEAD_DIM] = (_head_rows(vt, hh) * key_w).astype(BF16)
                vt_ref[n, hh * vt_rows + HEAD_DIM:(hh + 1) * vt_rows] = jnp.broadcast_to(
                    key_w, (vt_rows - HEAD_DIM, t)).astype(BF16)
        kmean_ref[...] = jnp.zeros_like(kmean_ref)
        for n in range(n_blocks):
            kb = k_ref[n * t:(n + 1) * t, :].astype(F32)
            kmean_ref[n:n + 1, :] = jnp.mean(kb, axis=0, keepdims=True)

    blk_id = lax.broadcasted_iota(jnp.int32, (nb_pad, t), 0)
    past = blk_id < i
    q_raw = q_ref[...]
    qh_t = _transposed_heads(q_raw, 1.0)
    km_hi, km_lo = _split_bf16(kmean_ref[...])

    def store_logits(slot, tile):
        k = k_ref[pl.ds(pl.multiple_of(tile * t, t), t), :]
        for hh in heads:
            s_ref[slot, hh] = _head_dot(k, qh_t[hh])

    def accumulate(slot, tile, alpha):
        vt = vt_ref[tile]
        for hh in heads:
            acc_ref[hh] = alpha[hh] * acc_ref[hh] + _dot(vt[hh * vt_rows:(hh + 1) * vt_rows, :], p_ref[slot, hh])

    store_logits(0, 0)
    k_own = k_ref[pl.ds(pl.multiple_of(i * t, t), t), :]
    s_own = [_head_dot(k_own, qh_t[hh]) for hh in heads]
    gates = [_head_dot(km_hi, qh_t[hh]) + _head_dot(km_lo, qh_t[hh]) for hh in heads]
    m = []
    for hh in heads:
        acc_ref[hh] = jnp.zeros((vt_rows, t), F32)
        gate = jnp.where(past, gates[hh], -jnp.inf)
        rank = jnp.zeros((nb_pad, t), jnp.int32)
        for blk in range(n_blocks):
            gb = gate[blk:blk + 1, :]
            ahead = (gb > gate) | ((gb == gate) & (blk_id > blk))
            rank += ahead.astype(jnp.int32)
        chosen_ref[hh] = jnp.where(past & (rank < MOBA_TOPK), 1.0, 0.0)

        s = jnp.where(key <= qry, s_own[hh], NEG_BIG)
        m0 = jnp.max(s, axis=0, keepdims=True) + slopes[hh] * top
        p_ref[1, hh] = jnp.exp2(s - m0).astype(BF16)
        m.append(m0)
    ones = jnp.ones((1, t), F32)

    def step(slot, n, carry):
        alpha_prev, m_old, prev_tile = carry
        accumulate(1 - slot, prev_tile, alpha_prev)
        store_logits(1 - slot, jnp.minimum(n + 1, i))
        offset = ((i - n) * t).astype(F32)
        alpha, m_new = [], []
        for hh in heads:
            picked = chosen_ref[hh, pl.ds(n, 1), :]
            shift = jnp.where(picked > 0.0, -slopes[hh] * offset, NEG_BIG)
            s = s_ref[slot, hh]
            bound = jnp.max(s, axis=0, keepdims=True) + (shift + slopes[hh] * top)
            m_new.append(jnp.maximum(m_old[hh], bound))
            alpha.append(jnp.exp2(m_old[hh] - m_new[hh]))
            p_ref[slot, hh] = jnp.exp2(s - (m_new[hh] - shift)).astype(BF16)
        return tuple(alpha), tuple(m_new), n

    def pair(j, carry):
        return step(1, 2 * j + 1, step(0, 2 * j, carry))

    init = (tuple(ones for _ in heads), tuple(m), i)
    alpha_last, _, last_tile = lax.fori_loop(0, (i + 1) // 2, pair, init)
    accumulate(1, last_tile, alpha_last)
    _store_heads(o_ref, [acc_ref[hh, :HEAD_DIM] / acc_ref[hh, HEAD_DIM:HEAD_DIM + 1] for hh in heads])


def _moba_attention(qkv, slopes, bsz, seq):
    t = MOBA_BLOCK
    n_blocks = seq // t
    nb_pad = -(-n_blocks // BF16_SUBLANES) * BF16_SUBLANES
    vt_rows = HEAD_DIM + BF16_SUBLANES
    assert seq % t == 0
    cb = lambda off: off // ATTN_LANES
    return pl.pallas_call(
        functools.partial(_moba_kernel, n_blocks=n_blocks),
        grid=(bsz, N_HEADS // ATTN_HEADS, n_blocks),
        in_specs=[
            pl.BlockSpec(memory_space=pltpu.SMEM),
            pl.BlockSpec((None, t, ATTN_LANES), lambda b, hg, i: (b, i, cb(COL_MOBA_Q) + hg)),
            pl.BlockSpec((None, seq, ATTN_LANES), lambda b, hg, i: (b, 0, cb(COL_MOBA_K) + hg)),
            pl.BlockSpec((None, seq, ATTN_LANES), lambda b, hg, i: (b, 0, cb(COL_MOBA_V) + hg)),
        ],
        out_specs=pl.BlockSpec((None, t, ATTN_LANES), lambda b, hg, i: (b, i, hg)),
        out_shape=jax.ShapeDtypeStruct((bsz, seq, BRANCH_W), BF16),
        scratch_shapes=[pltpu.VMEM((nb_pad, ATTN_LANES), F32),
                        pltpu.VMEM((n_blocks, ATTN_HEADS * vt_rows, t), BF16),
                        pltpu.VMEM((ATTN_HEADS, nb_pad, t), F32),
                        pltpu.VMEM((ATTN_HEADS, vt_rows, t), F32),
                        pltpu.VMEM((2, ATTN_HEADS, t, t), F32),
                        pltpu.VMEM((2, ATTN_HEADS, t, t), BF16)],
        compiler_params=_params("parallel", "parallel", "arbitrary"),
        name="moba",
    )(slopes, qkv, qkv, qkv)


def _swa_kernel(slope_ref, sink_ref, q_ref, k_ref, v_ref, o_ref, vt_ref, *, q_blocks):
    w = SWA_WINDOW
    tile = pl.program_id(1)
    group = N_HEADS // SWA_KV_HEADS
    gw = group * w

    @pl.when(tile == 0)
    def _():
        _store_transposed_values(v_ref, vt_ref, w)

    key = lax.broadcasted_iota(jnp.int32, (w, gw), 0)
    qry = lax.broadcasted_iota(jnp.int32, (w, gw), 1) % w
    dist = (qry - key).astype(F32)
    ok_cur = key <= qry
    in_window = key > qry
    no_dims = jnp.zeros((HEAD_DIM, w), F32)
    units = [(qb, kv) for qb in range(q_blocks) for kv in range(SWA_KV_HEADS)]

    blk, s_cur, s_prev = {}, {}, {}
    for qb in range(q_blocks):
        blk[qb] = tile * q_blocks + qb
        prev = jnp.maximum(blk[qb] - 1, 0)
        k_cur = k_ref[pl.ds(pl.multiple_of(blk[qb] * w, w), w), :]
        k_prev = k_ref[pl.ds(pl.multiple_of(prev * w, w), w), :]
        qt = q_ref[qb * w:(qb + 1) * w, :].astype(F32).T
        for kv in range(SWA_KV_HEADS):
            cols = []
            for j in range(group):
                h = kv * group + j
                qh = qt[h * HEAD_DIM:(h + 1) * HEAD_DIM, :]
                cols.append(jnp.concatenate([qh, no_dims] if kv == 0 else [no_dims, qh], axis=0))
            qg = jnp.concatenate(cols, axis=1).astype(BF16)
            s_cur[qb, kv] = _dot(k_cur, qg)
            s_prev[qb, kv] = _dot(k_prev, qg)

    e_cur, e_prev, denom = {}, {}, {}
    for u in units:
        qb, kv = u
        slope, sink = slope_ref[kv], sink_ref[kv]
        bias = slope * dist
        sc = jnp.where(ok_cur, s_cur[u] - bias, NEG_BIG)
        sp = jnp.where(in_window & (blk[qb] > 0), s_prev[u] - bias - slope * float(w), NEG_BIG)
        m = jnp.maximum(jnp.maximum(jnp.max(sc, axis=0, keepdims=True), jnp.max(sp, axis=0, keepdims=True)), sink)
        ec, ep = jnp.exp2(sc - m), jnp.exp2(sp - m)
        denom[u] = jnp.sum(ec, axis=0, keepdims=True) + jnp.sum(ep, axis=0, keepdims=True) + jnp.exp2(sink - m)
        e_cur[u], e_prev[u] = ec.astype(BF16), ep.astype(BF16)

    out_t = {}
    for u in units:
        qb, kv = u
        vt_cur = _head_rows(vt_ref[blk[qb]], kv)
        vt_prev = _head_rows(vt_ref[jnp.maximum(blk[qb] - 1, 0)], kv)
        out_t[u] = (_dot(vt_cur, e_cur[u]) + _dot(vt_prev, e_prev[u])) / denom[u]
    for qb in range(q_blocks):
        heads_t = [out_t[qb, kv][:, j * w:(j + 1) * w] for kv in range(SWA_KV_HEADS) for j in range(group)]
        o_ref[qb * w:(qb + 1) * w, :] = jnp.concatenate(heads_t, axis=0).T.astype(o_ref.dtype)


def _swa_attention(qkv, slopes, sinks, bsz, seq):
    w = SWA_WINDOW
    q_blocks = 8
    tq = q_blocks * w
    qw = N_HEADS * HEAD_DIM
    assert SWA_KV_HEADS * HEAD_DIM == LANES
    per_col = lambda v: jnp.repeat(v.reshape(SWA_KV_HEADS, N_HEADS // SWA_KV_HEADS), w, axis=1)[:, None, :]
    const_spec = pl.BlockSpec((SWA_KV_HEADS, 1, (N_HEADS // SWA_KV_HEADS) * w), lambda b, i: (0, 0, 0))
    return pl.pallas_call(
        functools.partial(_swa_kernel, q_blocks=q_blocks),
        grid=(bsz, seq // tq),
        in_specs=[
            const_spec,
            const_spec,
            pl.BlockSpec((None, tq, qw), lambda b, i: (b, i, COL_SWA_Q // qw)),
            pl.BlockSpec((None, seq, LANES), lambda b, i: (b, 0, COL_SWA_K // LANES)),
            pl.BlockSpec((None, seq, LANES), lambda b, i: (b, 0, COL_SWA_V // LANES)),
        ],
        out_specs=pl.BlockSpec((None, tq, qw), lambda b, i: (b, i, 0)),
        out_shape=jax.ShapeDtypeStruct((bsz, seq, BRANCH_W), BF16),
        scratch_shapes=[pltpu.VMEM((seq // w, LANES, w), BF16)],
        compiler_params=_params("parallel", "arbitrary"),
        name="swa",
    )(per_col(slopes * LOG2_E), per_col(sinks * LOG2_E), qkv, qkv, qkv)


def _conv_kernel(w_ref, b_ref, c_ref, h_ref, o_ref):
    u = c_ref[...].astype(F32) * h_ref[...].astype(F32)
    pos = lax.broadcasted_iota(jnp.int32, u.shape, 0)
    y = w_ref[CONV_K - 1:CONV_K, :] * u
    for back in range(1, CONV_K):
        shifted = jnp.where(pos >= back, pltpu.roll(u, back, 0), 0.0)
        y += w_ref[CONV_K - 1 - back:CONV_K - back, :] * shifted
    o_ref[...] = (b_ref[...].astype(F32) * y).astype(o_ref.dtype)


def _gated_conv(qkv, conv_w, bsz, seq):
    ch = conv_w.shape[1]
    cb = lambda off: off // LANES
    return pl.pallas_call(
        _conv_kernel,
        grid=(bsz, ch // LANES),
        in_specs=[
            pl.BlockSpec((CONV_K, LANES), lambda b, c: (0, c)),
            pl.BlockSpec((None, seq, LANES), lambda b, c: (b, 0, cb(COL_CONV_B) + c)),
            pl.BlockSpec((None, seq, LANES), lambda b, c: (b, 0, cb(COL_CONV_C) + c)),
            pl.BlockSpec((None, seq, LANES), lambda b, c: (b, 0, cb(COL_CONV_H) + c)),
        ],
        out_specs=pl.BlockSpec((None, seq, LANES), lambda b, c: (b, 0, c)),
        out_shape=jax.ShapeDtypeStruct((bsz, seq, ch), BF16),
        compiler_params=_params("parallel", "parallel"),
        name="gated_conv",
    )(conv_w, qkv, qkv, qkv)


def _merge_kernel(x_ref, g_ref, *refs):
    y_refs = refs[:N_BRANCH]
    wb_ref, wo_ref, o_ref, xn_ref = refs[-4:]
    wg_refs = refs[N_BRANCH:-4]
    parts = len(wg_refs) // N_BRANCH
    j = pl.program_id(1)

    def mixed(xn, rows):
        out = None
        ys = [y_ref[rows, :] for y_ref in y_refs]
        for p in range(parts):
            cols = slice(p * GATE_BLOCK, (p + 1) * GATE_BLOCK)
            merged = None
            for n in range(N_BRANCH):
                gate = jax.nn.sigmoid(_dot(xn, wg_refs[n * parts + p][0]))
                term = gate * _dot(ys[n], wb_ref[n, :, cols])
                merged = term if merged is None else merged + term
            part = _dot(merged.astype(BF16), wo_ref[cols, :])
            out = part if out is None else out + part
        return out

    @pl.when(j == 0)
    def _():
        x = x_ref[...]
        xn_ref[...] = _rms_norm(x, g_ref[...]).astype(BF16)
        o_ref[...] = x

    o_ref[...] += mixed(xn_ref[...], slice(None))


def _merge(x, g, ys, w_in, w_branch, w_out, layer):
    n, d = x.shape
    tm, tn = min(MERGE_TOKEN_TILE, n), MERGE_COL_TILE
    parts = tn // GATE_BLOCK
    assert QKV_COLS % LANES == 0 and tn % GATE_BLOCK == 0
    y_spec = pl.BlockSpec((tm, BRANCH_W), lambda i, j: (i, 0))

    def gate_spec(branch, part):
        first = QKV_COLS + branch * d + part * GATE_BLOCK
        return pl.BlockSpec((pl.Element(1), pl.Element(d), pl.Element(GATE_BLOCK)),
                            lambda i, j: (layer, 0, pl.multiple_of(first + j * tn, LANES)))

    gate_specs = [gate_spec(br, p) for br in range(N_BRANCH) for p in range(parts)]
    return pl.pallas_call(
        _merge_kernel,
        grid=(n // tm, d // tn),
        in_specs=[
            pl.BlockSpec((tm, d), lambda i, j: (i, 0)),
            pl.BlockSpec((1, d), lambda i, j: (0, 0)),
            y_spec, y_spec, y_spec, y_spec,
            *gate_specs,
            pl.BlockSpec((None, N_BRANCH, BRANCH_W, tn), lambda i, j: (layer, 0, 0, j)),
            pl.BlockSpec((None, tn, d), lambda i, j: (layer, j, 0)),
        ],
        out_specs=pl.BlockSpec((tm, d), lambda i, j: (i, 0)),
        out_shape=jax.ShapeDtypeStruct((n, d), F32),
        scratch_shapes=[pltpu.VMEM((tm, d), BF16)],
        compiler_params=_params("parallel", "arbitrary"),
        name="merge",
    )(x, g, *ys, *([w_in] * len(gate_specs)), w_branch, w_out)


def _alibi_slopes(n):
    return 2.0 ** (-8.0 * jnp.arange(1, n + 1, dtype=F32) / n)


def _mixer(h, bsz, seq, norm_g, w_in, conv_w, sinks, w_branch, w_out, layer):
    slopes = _alibi_slopes(2 * N_HEADS)
    qkv = _qkv_proj(h, norm_g, w_in, layer).reshape(bsz, seq, QKV_COLS)
    y_a = _moba_attention(qkv, slopes[N_HEADS:] * LOG2_E, bsz, seq)
    y_b = _sb_attention(qkv, bsz, seq)
    y_c = _swa_attention(qkv, slopes[:N_HEADS], sinks.astype(F32), bsz, seq)
    y_d = _gated_conv(qkv, conv_w.reshape(CONV_K, -1), bsz, seq)
    ys = [y.reshape(bsz * seq, BRANCH_W) for y in (y_a, y_b, y_c, y_d)]
    return _merge(h, norm_g, ys, w_in, w_branch, w_out, layer)


def kernel(x, ffn1_norm, ffn1_w1, ffn1_w3, ffn1_w2, mix_norm, w_in, conv_w, attn_sinks, w_branch, w_out,
           ffn2_norm, ffn2_w1, ffn2_w3, ffn2_w2, final_norm):
    bsz, seq, d = x.shape
    depth = w_in.shape[0]
    h = x.reshape(bsz * seq, d)
    row = lambda v: v.reshape(1, d)
    final_g = row(final_norm)
    ffn1 = [w.astype(BF16) for w in (ffn1_w1, ffn1_w3, ffn1_w2)]
    ffn2 = [w.astype(BF16) for w in (ffn2_w1, ffn2_w3, ffn2_w2)]
    w_in, w_branch, w_out = (w.astype(BF16) for w in (w_in, w_branch, w_out))
    for l in range(depth):
        h = _ffn(h, row(ffn1_norm[l]), *ffn1, final_g, layer=l, final_norm=False)
        h = _mixer(h, bsz, seq, row(mix_norm[l]), w_in, conv_w[l], attn_sinks[l], w_branch, w_out, l)
        h = _ffn(h, row(ffn2_norm[l]), *ffn2, final_g, layer=l, final_norm=(l == depth - 1))
    return h.reshape(bsz, seq, d)
```
